```python
import math
import jax, jax.numpy as jnp
from jax import lax
import numpy as np

D_MODEL = 1024
BATCH = 32
SEQ = 2048
DEPTH = 1

N_MEM = 256
RMS_EPS = 1e-6
POOL_WINDOWS = (2, 4, 8, 16)
POOL_WIDTH = D_MODEL // 4
POOL_GROUP = POOL_WIDTH // len(POOL_WINDOWS)
DA_HEADS = 4
DA_HEAD_DIM = 64
DA_V_DIM = 2 * DA_HEAD_DIM
DA_QK_WIDTH = DA_HEADS * 2 * DA_HEAD_DIM
DA_WIDTH = DA_HEADS * DA_V_DIM
Q_BLOCK = 128
MEM_HEADS = 4
MEM_HEAD_DIM = 64
MEM_WIDTH = MEM_HEADS * MEM_HEAD_DIM
N_BRANCH = 3
SPLIT_IDX = [POOL_WIDTH,
             POOL_WIDTH + DA_QK_WIDTH,
             POOL_WIDTH + 2 * DA_QK_WIDTH,
             POOL_WIDTH + 2 * DA_QK_WIDTH + DA_WIDTH,
             POOL_WIDTH + 2 * DA_QK_WIDTH + DA_WIDTH + MEM_WIDTH]
IN_COLS = SPLIT_IDX[-1] + N_BRANCH * D_MODEL
N_GROUPS = 4
EXPERTS_PER_GROUP = 4
N_EXPERTS = N_GROUPS * EXPERTS_PER_GROUP
TOP_K_INNER = 2
EXPERT_HIDDEN = D_MODEL // 2
MOE_BLOCK = 512

kernel_name = "hybrid_pool_diffattn_mem_hmoe"


def rmsnorm(x, g):
    xf = x.astype(jnp.float32)
    r = lax.rsqrt(jnp.mean(xf * xf, axis=-1, keepdims=True) + RMS_EPS)
    return (xf * r).astype(x.dtype) * g


def pool_mixer(u, w_group, scale):
    b, s, _ = u.shape
    ug = u.reshape(b, s, len(POOL_WINDOWS), POOL_GROUP).astype(jnp.float32)
    c = jnp.pad(jnp.cumsum(ug, axis=1), ((0, 0), (1, 0), (0, 0), (0, 0)))
    t = jnp.arange(s)
    means = []
    for gi, w in enumerate(POOL_WINDOWS):
        lo = jnp.maximum(t + 1 - w, 0)
        cnt = (t + 1 - lo).astype(jnp.float32)
        win_sum = c[:, 1:, gi] - c[:, lo, gi]
        means.append(win_sum / cnt[None, :, None])
    pooled = (jnp.stack(means, axis=2) - ug).astype(u.dtype)
    mixed = jnp.einsum('bsgc,gcd->bsgd', pooled, w_group)
    return mixed.reshape(b, s, POOL_WIDTH) * scale


def diff_attention(q, k, v, lam, subln_g, lambda_init):
    b, s = q.shape[0], q.shape[1]
    nb = s // Q_BLOCK
    scale = DA_HEAD_DIM ** -0.5
    qb = q.reshape(b, nb, Q_BLOCK, DA_HEADS, 2, DA_HEAD_DIM).transpose(1, 0, 2, 3, 4, 5)
    k_pos = jnp.arange(s)

    def one_block(args):
        q_blk, i = args
        scores = jnp.einsum('bqhmd,bkhmd->bhmqk', q_blk, k).astype(jnp.float32) * scale
        q_pos = i * Q_BLOCK + jnp.arange(Q_BLOCK)
        causal = k_pos[None, :] <= q_pos[:, None]
        scores = jnp.where(causal, scores, -jnp.inf)
        p = jax.nn.softmax(scores, axis=-1)
        a = p[:, :, 0] - lam * p[:, :, 1]
        return jnp.einsum('bhqk,bkhv->bqhv', a.astype(v.dtype), v)

    o = lax.map(one_block, (qb, jnp.arange(nb)))
    o = o.transpose(1, 0, 2, 3, 4).reshape(b, s, DA_HEADS, DA_V_DIM)
    o = rmsnorm(o, subln_g) * (1.0 - lambda_init)
    return o.reshape(b, s, DA_WIDTH)


def memory_attention(q, mem_n, w_kv):
    b, m, _ = mem_n.shape
    kv = mem_n @ w_kv
    mk = kv[..., :MEM_WIDTH].reshape(b, m, MEM_HEADS, MEM_HEAD_DIM)
    mv = kv[..., MEM_WIDTH:].reshape(b, m, MEM_HEADS, MEM_HEAD_DIM)
    scores = jnp.einsum('bshd,bmhd->bhsm', q, mk).astype(jnp.float32) * (MEM_HEAD_DIM ** -0.5)
    p = jax.nn.softmax(scores, axis=-1)
    o = jnp.einsum('bhsm,bmhd->bshd', p.astype(mv.dtype), mv)
    return o.reshape(q.shape[0], q.shape[1], MEM_WIDTH)


def hier_moe(h, w_rg, b_rg, w_re, b_re, w_gate, w_up, w_down):
    b, s, d = h.shape
    t = h.reshape(-1, d)
    n_tok = t.shape[0]
    g_prob = jax.nn.softmax((t @ w_rg).astype(jnp.float32) + b_rg, axis=-1)
    g_p, g_idx = lax.top_k(g_prob, 1)
    e_logits = jnp.einsum('td,dge->tge', t, w_re).astype(jnp.float32) + b_re
    e_sel = jnp.take_along_axis(e_logits, g_idx[:, :, None], axis=1)[:, 0]
    e_p, e_idx = lax.top_k(jax.nn.softmax(e_sel, axis=-1), TOP_K_INNER)
    weights = g_p * (e_p / jnp.sum(e_p, axis=-1, keepdims=True))
    expert_id = g_idx * EXPERTS_PER_GROUP + e_idx

    flat_e = expert_id.reshape(-1)
    flat_w = weights.reshape(-1)
    flat_tok = jnp.repeat(jnp.arange(n_tok), TOP_K_INNER)
    n_assign = flat_e.shape[0]
    order = jnp.argsort(flat_e)
    se, stok, sw = flat_e[order], flat_tok[order], flat_w[order]
    counts = jnp.zeros((N_EXPERTS,), jnp.int32).at[flat_e].add(1)
    padded = ((counts + MOE_BLOCK - 1) // MOE_BLOCK) * MOE_BLOCK
    start = jnp.cumsum(counts) - counts
    pend = jnp.cumsum(padded)
    pstart = pend - padded
    dest = pstart[se] + jnp.arange(n_assign) - start[se]
    n_blocks = -(-n_assign // MOE_BLOCK) + N_EXPERTS
    n_rows = n_blocks * MOE_BLOCK
    rows = jnp.zeros((n_rows, d), t.dtype).at[dest].set(t[stok])
    block_start = jnp.arange(n_blocks) * MOE_BLOCK
    block_expert = jnp.minimum(jnp.searchsorted(pend, block_start, side='right'), N_EXPERTS - 1)

    def expert_block(args):
        xb, e = args
        hid = jax.nn.silu(xb @ w_gate[e]) * (xb @ w_up[e])
        return hid @ w_down[e]

    y_rows = lax.map(expert_block, (rows.reshape(n_blocks, MOE_BLOCK, d), block_expert)).reshape(n_rows, d)
    contrib = (y_rows[dest] * sw[:, None]).astype(t.dtype)
    out = jnp.zeros((n_tok, d), t.dtype).at[stok].add(contrib)
    return out.reshape(b, s, d)


def setup_inputs(seed: int = 0) -> dict:
    key = jax.random.key(seed)
    ks = jax.random.split(key, 32)
    f32 = jnp.float32
    D, L = D_MODEL, DEPTH

    def nrm(k, shape, fan_in):
        return jax.random.normal(k, shape, f32) * (fan_in ** -0.5)

    def gain(k, shape):
        return 1.0 + 0.02 * jax.random.normal(k, shape, f32)

    return {
        "x": jax.random.normal(ks[0], (BATCH, SEQ, D), f32),
        "mem": jax.random.normal(ks[1], (BATCH, N_MEM, D), f32),
        "norm_mix_g": gain(ks[2], (L, D)),
        "w_in": nrm(ks[3], (L, D, IN_COLS), D),
        "pool_w": nrm(ks[4], (L, len(POOL_WINDOWS), POOL_GROUP, POOL_GROUP), POOL_GROUP),
        "pool_scale": gain(ks[5], (L, POOL_WIDTH)),
        "lam_q1": 0.1 * jax.random.normal(ks[6], (L, DA_HEAD_DIM), f32),
        "lam_k1": 0.1 * jax.random.normal(ks[7], (L, DA_HEAD_DIM), f32),
        "lam_q2": 0.1 * jax.random.normal(ks[8], (L, DA_HEAD_DIM), f32),
        "lam_k2": 0.1 * jax.random.normal(ks[9], (L, DA_HEAD_DIM), f32),
        "subln_g": gain(ks[10], (L, DA_V_DIM)),
        "norm_mem_g": gain(ks[11], (L, D)),
        "w_mem_kv": nrm(ks[12], (L, D, 2 * MEM_WIDTH), D),
        "p_pool": nrm(ks[13], (L, POOL_WIDTH, D), POOL_WIDTH),
        "p_diff": nrm(ks[14], (L, DA_WIDTH, D), DA_WIDTH),
        "p_mem": nrm(ks[15], (L, MEM_WIDTH, D), MEM_WIDTH),
        "w_o": nrm(ks[16], (L, D, D), D),
        "norm_ffn_g": gain(ks[17], (L, D)),
        "w_router_group": nrm(ks[18], (L, D, N_GROUPS), D),
        "b_router_group": 0.01 * jax.random.normal(ks[19], (L, N_GROUPS), f32),
        "w_router_expert": nrm(ks[20], (L, D, N_GROUPS, EXPERTS_PER_GROUP), D),
        "b_router_expert": 0.01 * jax.random.normal(ks[21], (L, N_GROUPS, EXPERTS_PER_GROUP), f32),
        "w_expert_gate": nrm(ks[22], (L, N_EXPERTS, D, EXPERT_HIDDEN), D),
        "w_expert_up": nrm(ks[23], (L, N_EXPERTS, D, EXPERT_HIDDEN), D),
        "w_expert_down": nrm(ks[24], (L, N_EXPERTS, EXPERT_HIDDEN, D), EXPERT_HIDDEN),
        "final_g": gain(ks[25], (D,)),
    }


def reference(x, mem, norm_mix_g, w_in, pool_w, pool_scale, lam_q1, lam_k1, lam_q2, lam_k2,
              subln_g, norm_mem_g, w_mem_kv, p_pool, p_diff, p_mem, w_o, norm_ffn_g,
              w_router_group, b_router_group, w_router_expert, b_router_expert,
              w_expert_gate, w_expert_up, w_expert_down, final_g):
    b, s, d = x.shape
    for l in range(DEPTH):
        lambda_init = 0.8 - 0.6 * math.exp(-0.3 * l)
        h = rmsnorm(x, norm_mix_g[l])
        proj = h @ w_in[l]
        u_pool, dq, dk, dv, mq, gates = jnp.split(proj, SPLIT_IDX, axis=-1)
        gates = jax.nn.sigmoid(gates.reshape(b, s, N_BRANCH, d))

        pool_out = pool_mixer(u_pool, pool_w[l], pool_scale[l])

        lam = (jnp.exp(jnp.sum(lam_q1[l] * lam_k1[l]).astype(jnp.float32))
               - jnp.exp(jnp.sum(lam_q2[l] * lam_k2[l]).astype(jnp.float32)) + lambda_init)
        diff_out = diff_attention(dq.reshape(b, s, DA_HEADS, 2, DA_HEAD_DIM),
                                  dk.reshape(b, s, DA_HEADS, 2, DA_HEAD_DIM),
                                  dv.reshape(b, s, DA_HEADS, DA_V_DIM),
                                  lam, subln_g[l], lambda_init)

        mem_n = rmsnorm(mem, norm_mem_g[l])
        mem_out = memory_attention(mq.reshape(b, s, MEM_HEADS, MEM_HEAD_DIM), mem_n, w_mem_kv[l])

        merged = (gates[:, :, 0] * (pool_out @ p_pool[l])
                  + gates[:, :, 1] * (diff_out @ p_diff[l])
                  + gates[:, :, 2] * (mem_out @ p_mem[l]))
        x = x + merged @ w_o[l]
        x = x + hier_moe(rmsnorm(x, norm_ffn_g[l]), w_router_group[l], b_router_group[l],
                         w_router_expert[l], b_router_expert[l],
                         w_expert_gate[l], w_expert_up[l], w_expert_down[l])
    return rmsnorm(x, final_g)
```

```python
import functools

import jax
import jax.numpy as jnp
from jax import lax
from jax.experimental import pallas as pl
from jax.experimental.pallas import tpu as pltpu

F32 = jnp.float32
BF16 = jnp.bfloat16

RMS_EPS = 1e-6
POOL_WINDOWS = (2, 4, 8, 16)
POOL_GROUP = 64
POOL_WIDTH = 256
POOL_HALO = 16
DA_HEADS = 4
DA_HEAD_DIM = 64
DA_V_DIM = 128
DA_QK_WIDTH = 512
DA_WIDTH = 512
MEM_HEADS = 4
MEM_HEAD_DIM = 64
MEM_WIDTH = 256
N_GROUPS = 4
EXPERTS_PER_GROUP = 4
N_EXPERTS = 16
MOE_BLOCK = 512
LAMBDA_INIT = 0.2
QK_SCALE = 0.125
ROUTE_LANES = 128

PROJ_COLS = POOL_WIDTH + 3 * DA_QK_WIDTH + MEM_WIDTH

VMEM_LIMIT = 56 * 1024 * 1024


def _cparams(n_axes):
    return pltpu.CompilerParams(dimension_semantics=("arbitrary",) * n_axes, vmem_limit_bytes=VMEM_LIMIT)


def _rms(x):
    return x * lax.rsqrt(jnp.mean(x * x, axis=-1, keepdims=True) + RMS_EPS)


def _proj_kernel(x_ref, g_ref, w_ref, pool_ref, qkv_ref, mq_ref):
    h = (_rms(x_ref[...]) * g_ref[...]).astype(BF16)
    p = jnp.dot(h, w_ref[...], preferred_element_type=F32)
    pool_ref[...] = p[:, :POOL_WIDTH]
    q_end = POOL_WIDTH + DA_QK_WIDTH
    kv_end = q_end + 2 * DA_QK_WIDTH
    qkv_ref[:, :DA_QK_WIDTH] = (p[:, POOL_WIDTH:q_end] * QK_SCALE).astype(BF16)
    qkv_ref[:, DA_QK_WIDTH:] = p[:, q_end:kv_end].astype(BF16)
    mq_ref[...] = (p[:, kv_end:] * QK_SCALE).astype(BF16)


def _proj(x2, g, w, tm):
    t, d = x2.shape
    return pl.pallas_call(
        _proj_kernel,
        grid=(t // tm,),
        in_specs=[
            pl.BlockSpec((tm, d), lambda i: (i, 0)),
            pl.BlockSpec((1, d), lambda i: (0, 0)),
            pl.BlockSpec((d, PROJ_COLS), lambda i: (0, 0)),
        ],
        out_specs=[
            pl.BlockSpec((tm, POOL_WIDTH), lambda i: (i, 0)),
            pl.BlockSpec((tm, 3 * DA_QK_WIDTH), lambda i: (i, 0)),
            pl.BlockSpec((tm, MEM_WIDTH), lambda i: (i, 0)),
        ],
        out_shape=[
            jax.ShapeDtypeStruct((t, POOL_WIDTH), F32),
            jax.ShapeDtypeStruct((t, 3 * DA_QK_WIDTH), BF16),
            jax.ShapeDtypeStruct((t, MEM_WIDTH), BF16),
        ],
        compiler_params=_cparams(1),
        name="proj",
    )(x2, g, w)


def _memkv_kernel(m_ref, g_ref, w_ref, kv_ref):
    h = (_rms(m_ref[...]) * g_ref[...]).astype(BF16)
    kv_ref[...] = jnp.dot(h, w_ref[...], preferred_element_type=F32).astype(BF16)


def _memkv(mem2, g, w, tm):
    t, d = mem2.shape
    n = w.shape[1]
    return pl.pallas_call(
        _memkv_kernel,
        grid=(t // tm,),
        in_specs=[
            pl.BlockSpec((tm, d), lambda i: (i, 0)),
            pl.BlockSpec((1, d), lambda i: (0, 0)),
            pl.BlockSpec((d, n), lambda i: (0, 0)),
        ],
        out_specs=pl.BlockSpec((tm, n), lambda i: (i, 0)),
        out_shape=jax.ShapeDtypeStruct((t, n), BF16),
        compiler_params=_cparams(1),
        name="mem_kv",
    )(mem2, g, w)


def _diff_attn_kernel(q_ref, k_ref, v_ref, lq1_ref, lk1_ref, lq2_ref, lk2_ref, sg_ref, o_ref, *, tq):
    i = pl.program_id(2)
    q = q_ref[...]
    lane = lax.broadcasted_iota(jnp.int32, q.shape, 1)
    zero = jnp.zeros_like(q)
    q2 = jnp.concatenate([jnp.where(lane < DA_HEAD_DIM, q, zero), jnp.where(lane >= DA_HEAD_DIM, q, zero)], axis=0)
    rows = 2 * tq

    def chunk(j, carry, masked):
        m, l, acc = carry
        start = pl.multiple_of(j * tq, tq)
        k = k_ref[pl.ds(start, tq), :]
        v = v_ref[pl.ds(start, tq), :]
        s = lax.dot_general(q2, k, (((1,), (1,)), ((), ())), preferred_element_type=F32)
        if masked:
            r = lax.broadcasted_iota(jnp.int32, s.shape, 0)
            r = jnp.where(r >= tq, r - tq, r)
            c = lax.broadcasted_iota(jnp.int32, s.shape, 1)
            s = jnp.where(c <= r, s, -jnp.inf)
        m_new = jnp.maximum(m, jnp.max(s, axis=-1, keepdims=True))
        alpha = jnp.exp(m - m_new)
        p = jnp.exp(s - m_new)
        l = alpha * l + jnp.sum(p, axis=-1, keepdims=True)
        acc = alpha * acc + jnp.dot(p.astype(BF16), v, preferred_element_type=F32)
        return m_new, l, acc

    init = (jnp.full((rows, 1), -jnp.inf, F32), jnp.zeros((rows, 1), F32), jnp.zeros((rows, DA_V_DIM), F32))
    carry = lax.fori_loop(0, i, lambda j, c: chunk(j, c, False), init)
    m, l, acc = chunk(i, carry, True)

    lam = (jnp.exp(jnp.sum(lq1_ref[...] * lk1_ref[...], axis=-1, keepdims=True))
           - jnp.exp(jnp.sum(lq2_ref[...] * lk2_ref[...], axis=-1, keepdims=True)) + LAMBDA_INIT)
    o = acc / l
    o = o[:tq] - lam * o[tq:]
    o_ref[...] = (_rms(o) * sg_ref[...] * (1.0 - LAMBDA_INIT)).astype(o_ref.dtype)


def _diff_attn(qkv, lq1, lk1, lq2, lk2, subln_g, batch, seq, tq):
    t = qkv.shape[0]
    nq = seq // tq
    lam_spec = pl.BlockSpec((1, DA_HEAD_DIM), lambda b, h, i: (0, 0))
    return pl.pallas_call(
        functools.partial(_diff_attn_kernel, tq=tq),
        grid=(batch, DA_HEADS, nq),
        in_specs=[
            pl.BlockSpec((tq, 2 * DA_HEAD_DIM), lambda b, h, i: (b * nq + i, h)),
            pl.BlockSpec((seq, 2 * DA_HEAD_DIM), lambda b, h, i: (b, DA_HEADS + h)),
            pl.BlockSpec((seq, DA_V_DIM), lambda b, h, i: (b, 2 * DA_HEADS + h)),
            lam_spec, lam_spec, lam_spec, lam_spec,
            pl.BlockSpec((1, DA_V_DIM), lambda b, h, i: (0, 0)),
        ],
        out_specs=pl.BlockSpec((tq, DA_V_DIM), lambda b, h, i: (b * nq + i, h)),
        out_shape=jax.ShapeDtypeStruct((t, DA_WIDTH), BF16),
        compiler_params=_cparams(3),
        name="diff_attn",
    )(qkv, qkv, qkv, lq1, lk1, lq2, lk2, subln_g)


def _pool_mixer(u, halo, seq_pos0):
    tm = u.shape[0]
    ext = jnp.concatenate([halo, u], axis=0)
    n = ext.shape[0]
    sums = []
    acc = ext
    for shift in (1, 2, 4, 8):
        acc = acc + pltpu.roll(acc, shift, 0)
        sums.append(acc[POOL_HALO:n])
    lane = lax.broadcasted_iota(jnp.int32, (tm, POOL_WIDTH), 1)
    pos = seq_pos0 + lax.broadcasted_iota(jnp.int32, (tm, POOL_WIDTH), 0)
    win_sum = sums[3]
    win = jnp.full((tm, POOL_WIDTH), POOL_WINDOWS[3], jnp.int32)
    for gi in (2, 1, 0):
        in_group = lane < (gi + 1) * POOL_GROUP
        win_sum = jnp.where(in_group, sums[gi], win_sum)
        win = jnp.where(in_group, POOL_WINDOWS[gi], win)
    cnt = jnp.minimum(pos + 1, win).astype(F32)
    return win_sum / cnt - u


def _mem_attention(mq, kv):
    tm = mq.shape[0]
    head = lax.broadcasted_iota(jnp.int32, mq.shape, 1) // MEM_HEAD_DIM
    zero = jnp.zeros_like(mq)
    q4 = jnp.concatenate([jnp.where(head == h, mq, zero) for h in range(MEM_HEADS)], axis=0)
    mk = kv[:, :MEM_WIDTH]
    mv = kv[:, MEM_WIDTH:]
    s = lax.dot_general(q4, mk, (((1,), (1,)), ((), ())), preferred_element_type=F32)
    p = jnp.exp(s - jnp.max(s, axis=-1, keepdims=True))
    l = jnp.sum(p, axis=-1, keepdims=True)
    o4 = jnp.dot(p.astype(BF16), mv, preferred_element_type=F32) / l
    out = jnp.zeros((tm, MEM_WIDTH), F32)
    for h in range(MEM_HEADS):
        out = jnp.where(head == h, o4[h * tm:(h + 1) * tm], out)
    return out


def _router(logits):
    lane = lax.broadcasted_iota(jnp.int32, logits.shape, 1).astype(F32)
    neg = -jnp.inf

    def first_argmax(vals, vmax):
        return jnp.min(jnp.where(vals == vmax, lane, float(ROUTE_LANES)), axis=-1, keepdims=True)

    lg = jnp.where(lane < N_GROUPS, logits, neg)
    mg = jnp.max(lg, axis=-1, keepdims=True)
    g_p = 1.0 / jnp.sum(jnp.exp(lg - mg), axis=-1, keepdims=True)
    g_idx = first_argmax(lg, mg)
    lo = N_GROUPS + EXPERTS_PER_GROUP * g_idx
    le = jnp.where((lane >= lo) & (lane < lo + EXPERTS_PER_GROUP), logits, neg)
    m1 = jnp.max(le, axis=-1, keepdims=True)
    i1 = first_argmax(le, m1)
    le2 = jnp.where(lane == i1, neg, le)
    m2 = jnp.max(le2, axis=-1, keepdims=True)
    i2 = first_argmax(le2, m2)
    r = jnp.exp(m2 - m1)
    w1 = g_p / (1.0 + r)
    w2 = w1 * r
    e1 = i1 - N_GROUPS
    e2 = i2 - N_GROUPS
    out = jnp.where(lane == 0, e1, 0.0)
    out = jnp.where(lane == 1, e2, out)
    out = jnp.where(lane == 2, w1, out)
    out = jnp.where(lane == 3, w2, out)
    return out


def _merge_kernel(x_ref, u_ref, halo_ref, mq_ref, kv_ref, diff_ref,
                  gmix_ref, wg_ref, wbd_ref, pscale_ref, pp_ref, pd_ref, pm_ref, wo_ref,
                  gffn_ref, wr_ref, br_ref,
                  x1_ref, hn_ref, route_ref, *, tm, seq):
    i = pl.program_id(0)
    d = x_ref.shape[1]
    x = x_ref[...]
    h = (_rms(x) * gmix_ref[...]).astype(BF16)

    seq_pos0 = (i * tm) % seq
    halo = jnp.where(seq_pos0 == 0, 0.0, halo_ref[...])
    pooled = _pool_mixer(u_ref[...], halo, seq_pos0).astype(BF16)
    pool_out = jnp.dot(pooled, wbd_ref[...], preferred_element_type=F32) * pscale_ref[...]
    mem_out = _mem_attention(mq_ref[...], kv_ref[...])

    branches = ((pool_out.astype(BF16), pp_ref), (diff_ref[...], pd_ref), (mem_out.astype(BF16), pm_ref))
    merged = jnp.zeros((tm, d), F32)
    for n, (b_in, p_ref) in enumerate(branches):
        gate = jax.nn.sigmoid(jnp.dot(h, wg_ref[:, n * d:(n + 1) * d], preferred_element_type=F32))
        merged = merged + gate * jnp.dot(b_in, p_ref[...], preferred_element_type=F32)
    x1 = x + jnp.dot(merged.astype(BF16), wo_ref[...], preferred_element_type=F32)
    x1_ref[...] = x1

    hn = _rms(x1) * gffn_ref[...]
    hn_ref[...] = hn
    logits = jnp.dot(hn.astype(BF16), wr_ref[...], preferred_element_type=F32) + br_ref[...]
    route_ref[...] = _router(logits)


def _merge(x2, u_pool, mq, kv, diff, gmix, wg, wbd, pscale, pp, pd, pm, wo, gffn, wr, br, seq, n_mem, tm):
    t, d = x2.shape
    tiles_per_seq = seq // tm
    halo_blocks = tm // POOL_HALO

    def tile(i):
        return (i, 0)

    def const(i):
        return (0, 0)

    def wspec(a):
        return pl.BlockSpec(a.shape, const)

    return pl.pallas_call(
        functools.partial(_merge_kernel, tm=tm, seq=seq),
        grid=(t // tm,),
        in_specs=[
            pl.BlockSpec((tm, d), tile),
            pl.BlockSpec((tm, POOL_WIDTH), tile),
            pl.BlockSpec((POOL_HALO, POOL_WIDTH), lambda i: (jnp.maximum(i * halo_blocks - 1, 0), 0)),
            pl.BlockSpec((tm, MEM_WIDTH), tile),
            pl.BlockSpec((n_mem, 2 * MEM_WIDTH), lambda i: (i // tiles_per_seq, 0)),
            pl.BlockSpec((tm, DA_WIDTH), tile),
            wspec(gmix), wspec(wg), wspec(wbd), wspec(pscale), wspec(pp), wspec(pd), wspec(pm), wspec(wo),
            wspec(gffn), wspec(wr), wspec(br),
        ],
        out_specs=[
            pl.BlockSpec((tm, d), tile),
            pl.BlockSpec((tm, d), tile),
            pl.BlockSpec((tm, ROUTE_LANES), tile),
        ],
        out_shape=[
            jax.ShapeDtypeStruct((t, d), F32),
            jax.ShapeDtypeStruct((t, d), F32),
            jax.ShapeDtypeStruct((t, ROUTE_LANES), F32),
        ],
        compiler_params=_cparams(1),
        name="merge",
    )(x2, u_pool, u_pool, mq, kv, diff, gmix, wg, wbd, pscale, pp, pd, pm, wo, gffn, wr, br)


def _gather_kernel(idx_ref, src_ref, dst_ref, sem, *, chunk):
    i = pl.program_id(0)
    base = i * chunk

    def row_copy(j):
        return pltpu.make_async_copy(src_ref.at[pl.ds(idx_ref[0, j], 1)], dst_ref.at[pl.ds(base + j, 1)], sem)

    def issue(j, carry):
        row_copy(j).start()
        return carry

    lax.fori_loop(0, chunk, issue, 0, unroll=8)
    pltpu.make_async_copy(src_ref.at[pl.ds(0, chunk)], dst_ref.at[pl.ds(base, chunk)], sem).wait()


def _gather_rows(src, idx, chunk):
    n = idx.shape[0]
    d = src.shape[1]
    idx3 = idx.reshape(n // chunk, 1, chunk)
    return pl.pallas_call(
        functools.partial(_gather_kernel, chunk=chunk),
        grid=(n // chunk,),
        in_specs=[
            pl.BlockSpec((None, 1, chunk), lambda i: (i, 0, 0), memory_space=pltpu.SMEM),
            pl.BlockSpec(memory_space=pl.ANY),
        ],
        out_specs=pl.BlockSpec(memory_space=pl.ANY),
        out_shape=jax.ShapeDtypeStruct((n, d), src.dtype),
        scratch_shapes=[pltpu.SemaphoreType.DMA(())],
        compiler_params=_cparams(1),
        name="gather_rows",
    )(idx3, src)


def _ffn_kernel(be_ref, nused_ref, rows_ref, wg_ref, wu_ref, wd_ref, y_ref):
    i = pl.program_id(0)

    @pl.when(i < nused_ref[0])
    def _():
        xb = rows_ref[...].astype(BF16)
        hg = jnp.dot(xb, wg_ref[...], preferred_element_type=F32)
        hu = jnp.dot(xb, wu_ref[...], preferred_element_type=F32)
        hid = (hg * jax.nn.sigmoid(hg) * hu).astype(BF16)
        y_ref[...] = jnp.dot(hid, wd_ref[...], preferred_element_type=F32)

    @pl.when(i >= nused_ref[0])
    def _():
        y_ref[...] = jnp.zeros_like(y_ref)


def _ffn(block_expert, n_used, rows, wg, wu, wd):
    n_rows, d = rows.shape
    hidden = wg.shape[2]
    grid_spec = pltpu.PrefetchScalarGridSpec(
        num_scalar_prefetch=2,
        grid=(n_rows // MOE_BLOCK,),
        in_specs=[
            pl.BlockSpec((MOE_BLOCK, d), lambda i, be, nu: (i, 0)),
            pl.BlockSpec((None, d, hidden), lambda i, be, nu: (be[i], 0, 0)),
            pl.BlockSpec((None, d, hidden), lambda i, be, nu: (be[i], 0, 0)),
            pl.BlockSpec((None, hidden, d), lambda i, be, nu: (be[i], 0, 0)),
        ],
        out_specs=pl.BlockSpec((MOE_BLOCK, d), lambda i, be, nu: (i, 0)),
    )
    return pl.pallas_call(
        _ffn_kernel,
        grid_spec=grid_spec,
        out_shape=jax.ShapeDtypeStruct((n_rows, d), F32),
        compiler_params=_cparams(1),
        name="ffn",
    )(block_expert, n_used, rows, wg, wu, wd)


def _final_kernel(x1_ref, y_ref, route_ref, g_ref, o_ref):
    d = x1_ref.shape[1]
    route = route_ref[...]
    y = y_ref[...]
    out = x1_ref[...] + (route[:, 2:3] * y[:, :d] + route[:, 3:4] * y[:, d:])
    o_ref[...] = _rms(out) * g_ref[...]


def _final(x1, y2, route, g, tm):
    t, d = x1.shape
    return pl.pallas_call(
        _final_kernel,
        grid=(t // tm,),
        in_specs=[
            pl.BlockSpec((tm, d), lambda i: (i, 0)),
            pl.BlockSpec((tm, 2 * d), lambda i: (i, 0)),
            pl.BlockSpec((tm, ROUTE_LANES), lambda i: (i, 0)),
            pl.BlockSpec((1, d), lambda i: (0, 0)),
        ],
        out_specs=pl.BlockSpec((tm, d), lambda i: (i, 0)),
        out_shape=jax.ShapeDtypeStruct((t, d), F32),
        compiler_params=_cparams(1),
        name="final",
    )(x1, y2, route, g)


def _dispatch_plan(expert_id):
    n_assign = expert_id.size
    flat_e = expert_id.reshape(-1)
    order = jnp.argsort(flat_e, stable=True).astype(jnp.int32)
    se = flat_e[order]
    counts = jnp.sum(flat_e[:, None] == jnp.arange(N_EXPERTS, dtype=jnp.int32)[None, :], axis=0, dtype=jnp.int32)
    padded = ((counts + MOE_BLOCK - 1) // MOE_BLOCK) * MOE_BLOCK
    start = jnp.cumsum(counts) - counts
    pend = jnp.cumsum(padded)
    pstart = pend - padded
    n_blocks = n_assign // MOE_BLOCK + N_EXPERTS
    n_rows = n_blocks * MOE_BLOCK
    dest = pstart[se] + jnp.arange(n_assign, dtype=jnp.int32) - start[se]
    block_start = jnp.arange(n_blocks, dtype=jnp.int32) * MOE_BLOCK
    block_expert = jnp.minimum(jnp.searchsorted(pend, block_start, side="right"), N_EXPERTS - 1).astype(jnp.int32)
    r = jnp.arange(n_rows, dtype=jnp.int32)
    row_e = block_expert[r // MOE_BLOCK]
    off = r - pstart[row_e]
    valid = off < counts[row_e]
    sorted_pos = jnp.clip(start[row_e] + off, 0, n_assign - 1)
    row_tok = jnp.where(valid, order[sorted_pos] // 2, 0).astype(jnp.int32)
    assign_row = jnp.zeros((n_assign,), jnp.int32).at[order].set(dest.astype(jnp.int32))
    n_used = (pend[-1] // MOE_BLOCK).astype(jnp.int32).reshape(1)
    return row_tok, assign_row, block_expert, n_used


def kernel(x, mem, norm_mix_g, w_in, pool_w, pool_scale, lam_q1, lam_k1, lam_q2, lam_k2, subln_g, norm_mem_g,
           w_mem_kv, p_pool, p_diff, p_mem, w_o, norm_ffn_g, w_router_group, b_router_group, w_router_expert,
           b_router_expert, w_expert_gate, w_expert_up, w_expert_down, final_g):
    b, s, d = x.shape
    n_mem = mem.shape[1]
    t = b * s
    layer = 0
    x2 = x.reshape(t, d)
    mem2 = mem.reshape(b * n_mem, d)

    w_in_l = w_in[layer]
    w_proj = w_in_l[:, :PROJ_COLS].astype(BF16)
    w_gates = w_in_l[:, PROJ_COLS:].astype(BF16)
    n_win = len(POOL_WINDOWS)
    eye = jnp.eye(n_win, dtype=F32)
    w_bd = (pool_w[layer][:, :, None, :] * eye[:, None, :, None]).reshape(POOL_WIDTH, POOL_WIDTH).astype(BF16)
    n_route = N_GROUPS + N_EXPERTS
    w_r = jnp.concatenate([w_router_group[layer], w_router_expert[layer].reshape(d, N_EXPERTS),
                           jnp.zeros((d, ROUTE_LANES - n_route), F32)], axis=1).astype(BF16)
    b_r = jnp.concatenate([b_router_group[layer], b_router_expert[layer].reshape(N_EXPERTS),
                           jnp.zeros((ROUTE_LANES - n_route,), F32)]).reshape(1, ROUTE_LANES)

    def row(v):
        return v.reshape(1, -1)

    u_pool, qkv, mq = _proj(x2, row(norm_mix_g[layer]), w_proj, tm=512)
    kv = _memkv(mem2, row(norm_mem_g[layer]), w_mem_kv[layer].astype(BF16), tm=512)
    diff = _diff_attn(qkv, row(lam_q1[layer]), row(lam_k1[layer]), row(lam_q2[layer]), row(lam_k2[layer]),
                      row(subln_g[layer]), b, s, tq=256)
    x1, hn, route = _merge(
        x2, u_pool, mq, kv, diff, row(norm_mix_g[layer]), w_gates, w_bd, row(pool_scale[layer]),
        p_pool[layer].astype(BF16), p_diff[layer].astype(BF16), p_mem[layer].astype(BF16), w_o[layer].astype(BF16),
        row(norm_ffn_g[layer]), w_r, b_r, s, n_mem, tm=256)

    expert_id = route[:, :2].astype(jnp.int32)
    row_tok, assign_row, block_expert, n_used = _dispatch_plan(expert_id)
    rows = _gather_rows(hn, row_tok, chunk=1024)
    y_rows = _ffn(block_expert, n_used, rows, w_expert_gate[layer].astype(BF16), w_expert_up[layer].astype(BF16),
                  w_expert_down[layer].astype(BF16))
    y2 = _gather_rows(y_rows, assign_row, chunk=1024).reshape(t, 2 * d)
    out = _final(x1, y2, route, row(final_g), tm=256)
    return out.reshape(b, s, d)
```

```python
import functools

import jax
import jax.numpy as jnp
from jax import lax
from jax.experimental import pallas as pl
from jax.experimental.pallas import tpu as pltpu

F32 = jnp.float32
BF16 = jnp.bfloat16

RMS_EPS = 1e-6
POOL_WINDOWS = (2, 4, 8, 16)
POOL_GROUP = 64
POOL_WIDTH = 256
POOL_HALO = 16
DA_HEADS = 4
DA_HEAD_DIM = 64
DA_V_DIM = 128
DA_QK_WIDTH = 512
DA_WIDTH = 512
MEM_HEADS = 4
MEM_HEAD_DIM = 64
MEM_WIDTH = 256
N_GROUPS = 4
EXPERTS_PER_GROUP = 4
N_EXPERTS = 16
MOE_BLOCK = 512
MOE_TILE = 256
GROUP_ALIGN = 8
DISPATCH_CHUNK = 32
COMBINE_CHUNK = 16
LAMBDA_INIT = 0.2
QK_SCALE = 0.125
ROUTE_LANES = 128

PROJ_COLS = POOL_WIDTH + 3 * DA_QK_WIDTH + MEM_WIDTH

VMEM_LIMIT = 56 * 1024 * 1024


def _cparams(n_axes):
    return pltpu.CompilerParams(dimension_semantics=("arbitrary",) * n_axes, vmem_limit_bytes=VMEM_LIMIT)


def _rms(x):
    return x * lax.rsqrt(jnp.mean(x * x, axis=-1, keepdims=True) + RMS_EPS)


def _proj_kernel(x_ref, g_ref, w_ref, pool_ref, qkv_ref, mq_ref):
    h = (_rms(x_ref[...]) * g_ref[...]).astype(BF16)
    p = jnp.dot(h, w_ref[...], preferred_element_type=F32)
    pool_ref[...] = p[:, :POOL_WIDTH]
    q_end = POOL_WIDTH + DA_QK_WIDTH
    kv_end = q_end + 2 * DA_QK_WIDTH
    qkv_ref[:, :DA_QK_WIDTH] = (p[:, POOL_WIDTH:q_end] * QK_SCALE).astype(BF16)
    qkv_ref[:, DA_QK_WIDTH:] = p[:, q_end:kv_end].astype(BF16)
    mq_ref[...] = (p[:, kv_end:] * QK_SCALE).astype(BF16)


def _proj(x2, g, w, tm):
    t, d = x2.shape
    return pl.pallas_call(
        _proj_kernel,
        grid=(t // tm,),
        in_specs=[
            pl.BlockSpec((tm, d), lambda i: (i, 0)),
            pl.BlockSpec((1, d), lambda i: (0, 0)),
            pl.BlockSpec((d, PROJ_COLS), lambda i: (0, 0)),
        ],
        out_specs=[
            pl.BlockSpec((tm, POOL_WIDTH), lambda i: (i, 0)),
            pl.BlockSpec((tm, 3 * DA_QK_WIDTH), lambda i: (i, 0)),
            pl.BlockSpec((tm, MEM_WIDTH), lambda i: (i, 0)),
        ],
        out_shape=[
            jax.ShapeDtypeStruct((t, POOL_WIDTH), F32),
            jax.ShapeDtypeStruct((t, 3 * DA_QK_WIDTH), BF16),
            jax.ShapeDtypeStruct((t, MEM_WIDTH), BF16),
        ],
        compiler_params=_cparams(1),
        name="proj",
    )(x2, g, w)


def _memkv_kernel(m_ref, g_ref, w_ref, kv_ref):
    h = (_rms(m_ref[...]) * g_ref[...]).astype(BF16)
    kv_ref[...] = jnp.dot(h, w_ref[...], preferred_element_type=F32).astype(BF16)


def _memkv(mem2, g, w, tm):
    t, d = mem2.shape
    n = w.shape[1]
    return pl.pallas_call(
        _memkv_kernel,
        grid=(t // tm,),
        in_specs=[
            pl.BlockSpec((tm, d), lambda i: (i, 0)),
            pl.BlockSpec((1, d), lambda i: (0, 0)),
            pl.BlockSpec((d, n), lambda i: (0, 0)),
        ],
        out_specs=pl.BlockSpec((tm, n), lambda i: (i, 0)),
        out_shape=jax.ShapeDtypeStruct((t, n), BF16),
        compiler_params=_cparams(1),
        name="mem_kv",
    )(mem2, g, w)


def _diff_attn_kernel(q_ref, k_ref, v_ref, lq1_ref, lk1_ref, lq2_ref, lk2_ref, sg_ref, o_ref, *, tq):
    i = pl.program_id(2)
    q = q_ref[...]
    lane = lax.broadcasted_iota(jnp.int32, q.shape, 1)
    zero = jnp.zeros_like(q)
    q2 = jnp.concatenate([jnp.where(lane < DA_HEAD_DIM, q, zero), jnp.where(lane >= DA_HEAD_DIM, q, zero)], axis=0)
    rows = 2 * tq

    def chunk(j, carry, masked):
        m, l, acc = carry
        start = pl.multiple_of(j * tq, tq)
        k = k_ref[pl.ds(start, tq), :]
        v = v_ref[pl.ds(start, tq), :]
        s = lax.dot_general(q2, k, (((1,), (1,)), ((), ())), preferred_element_type=F32)
        if masked:
            r = lax.broadcasted_iota(jnp.int32, s.shape, 0)
            r = jnp.where(r >= tq, r - tq, r)
            c = lax.broadcasted_iota(jnp.int32, s.shape, 1)
            s = jnp.where(c <= r, s, -jnp.inf)
        m_new = jnp.maximum(m, jnp.max(s, axis=-1, keepdims=True))
        alpha = jnp.exp(m - m_new)
        p = jnp.exp(s - m_new)
        l = alpha * l + jnp.sum(p, axis=-1, keepdims=True)
        acc = alpha * acc + jnp.dot(p.astype(BF16), v, preferred_element_type=F32)
        return m_new, l, acc

    init = (jnp.full((rows, 1), -jnp.inf, F32), jnp.zeros((rows, 1), F32), jnp.zeros((rows, DA_V_DIM), F32))
    carry = lax.fori_loop(0, i, lambda j, c: chunk(j, c, False), init)
    m, l, acc = chunk(i, carry, True)

    lam = (jnp.exp(jnp.sum(lq1_ref[...] * lk1_ref[...], axis=-1, keepdims=True))
           - jnp.exp(jnp.sum(lq2_ref[...] * lk2_ref[...], axis=-1, keepdims=True)) + LAMBDA_INIT)
    o = acc / l
    o = o[:tq] - lam * o[tq:]
    o_ref[...] = (_rms(o) * sg_ref[...] * (1.0 - LAMBDA_INIT)).astype(o_ref.dtype)


def _diff_attn(qkv, lq1, lk1, lq2, lk2, subln_g, batch, seq, tq):
    t = qkv.shape[0]
    nq = seq // tq
    lam_spec = pl.BlockSpec((1, DA_HEAD_DIM), lambda b, h, i: (0, 0))
    return pl.pallas_call(
        functools.partial(_diff_attn_kernel, tq=tq),
        grid=(batch, DA_HEADS, nq),
        in_specs=[
            pl.BlockSpec((tq, 2 * DA_HEAD_DIM), lambda b, h, i: (b * nq + i, h)),
            pl.BlockSpec((seq, 2 * DA_HEAD_DIM), lambda b, h, i: (b, DA_HEADS + h)),
            pl.BlockSpec((seq, DA_V_DIM), lambda b, h, i: (b, 2 * DA_HEADS + h)),
            lam_spec, lam_spec, lam_spec, lam_spec,
            pl.BlockSpec((1, DA_V_DIM), lambda b, h, i: (0, 0)),
        ],
        out_specs=pl.BlockSpec((tq, DA_V_DIM), lambda b, h, i: (b * nq + i, h)),
        out_shape=jax.ShapeDtypeStruct((t, DA_WIDTH), BF16),
        compiler_params=_cparams(3),
        name="diff_attn",
    )(qkv, qkv, qkv, lq1, lk1, lq2, lk2, subln_g)


def _pool_mixer(u, halo, seq_pos0):
    tm = u.shape[0]
    ext = jnp.concatenate([halo, u], axis=0)
    n = ext.shape[0]
    sums = []
    acc = ext
    for shift in (1, 2, 4, 8):
        acc = acc + pltpu.roll(acc, shift, 0)
        sums.append(acc[POOL_HALO:n])
    lane = lax.broadcasted_iota(jnp.int32, (tm, POOL_WIDTH), 1)
    pos = seq_pos0 + lax.broadcasted_iota(jnp.int32, (tm, POOL_WIDTH), 0)
    win_sum = sums[3]
    win = jnp.full((tm, POOL_WIDTH), POOL_WINDOWS[3], jnp.int32)
    for gi in (2, 1, 0):
        in_group = lane < (gi + 1) * POOL_GROUP
        win_sum = jnp.where(in_group, sums[gi], win_sum)
        win = jnp.where(in_group, POOL_WINDOWS[gi], win)
    cnt = jnp.minimum(pos + 1, win).astype(F32)
    return win_sum / cnt - u


def _mem_attention(mq, kv):
    tm = mq.shape[0]
    head = lax.broadcasted_iota(jnp.int32, mq.shape, 1) // MEM_HEAD_DIM
    zero = jnp.zeros_like(mq)
    q4 = jnp.concatenate([jnp.where(head == h, mq, zero) for h in range(MEM_HEADS)], axis=0)
    mk = kv[:, :MEM_WIDTH]
    mv = kv[:, MEM_WIDTH:]
    s = lax.dot_general(q4, mk, (((1,), (1,)), ((), ())), preferred_element_type=F32)
    p = jnp.exp(s - jnp.max(s, axis=-1, keepdims=True))
    l = jnp.sum(p, axis=-1, keepdims=True)
    o4 = jnp.dot(p.astype(BF16), mv, preferred_element_type=F32) / l
    out = jnp.zeros((tm, MEM_WIDTH), F32)
    for h in range(MEM_HEADS):
        out = jnp.where(head == h, o4[h * tm:(h + 1) * tm], out)
    return out


def _router(logits):
    tm = logits.shape[0]
    lane = lax.broadcasted_iota(jnp.int32, logits.shape, 1).astype(F32)
    neg = -jnp.inf

    def first_argmax(vals, vmax):
        return jnp.min(jnp.where(vals == vmax, lane, float(ROUTE_LANES)), axis=-1, keepdims=True)

    lg = jnp.where(lane < N_GROUPS, logits, neg)
    mg = jnp.max(lg, axis=-1, keepdims=True)
    g_p = 1.0 / jnp.sum(jnp.exp(lg - mg), axis=-1, keepdims=True)
    g_idx = first_argmax(lg, mg)
    lo = N_GROUPS + EXPERTS_PER_GROUP * g_idx
    le = jnp.where((lane >= lo) & (lane < lo + EXPERTS_PER_GROUP), logits, neg)
    m1 = jnp.max(le, axis=-1, keepdims=True)
    i1 = first_argmax(le, m1)
    le2 = jnp.where(lane == i1, neg, le)
    m2 = jnp.max(le2, axis=-1, keepdims=True)
    i2 = first_argmax(le2, m2)
    r = jnp.exp(m2 - m1)
    w1 = g_p / (1.0 + r)
    w2 = w1 * r
    e1 = i1 - N_GROUPS
    e2 = i2 - N_GROUPS

    oh1 = lane == e1
    oh2 = lane == e2
    oh = jnp.where(oh1 | oh2, 1.0, 0.0)
    earlier = (lax.broadcasted_iota(jnp.int32, (tm, tm), 0) > lax.broadcasted_iota(jnp.int32, (tm, tm), 1))
    rank = jnp.dot(earlier.astype(BF16), oh.astype(BF16), preferred_element_type=F32)
    cnt = jnp.sum(oh, axis=0, keepdims=True)

    def round_up(v, m):
        return jnp.floor((v + (m - 1)) * (1.0 / m)) * m

    lower = (lax.broadcasted_iota(jnp.int32, (ROUTE_LANES, ROUTE_LANES), 0)
             < lax.broadcasted_iota(jnp.int32, (ROUTE_LANES, ROUTE_LANES), 1)).astype(BF16)
    both = jnp.concatenate([jnp.broadcast_to(round_up(cnt, GROUP_ALIGN), (8, ROUTE_LANES)),
                            jnp.broadcast_to(round_up(cnt, COMBINE_CHUNK), (8, ROUTE_LANES))], axis=0)
    offs = jnp.dot(both.astype(BF16), lower, preferred_element_type=F32)
    pos_d = offs[0:1] + rank
    pos_c = offs[8:9] + rank

    def pick(onehot, vals):
        return jnp.sum(jnp.where(onehot, vals, 0.0), axis=-1, keepdims=True)

    cols = (e1, e2, w1, w2, pick(oh1, pos_d), pick(oh2, pos_d), pick(oh1, pos_c), pick(oh2, pos_c))
    out = jnp.zeros_like(logits)
    for n, col in enumerate(cols):
        out = jnp.where(lane == n, col, out)
    return out, cnt


def _merge_kernel(x_ref, u_ref, halo_ref, mq_ref, kv_ref, diff_ref,
                  gmix_ref, wg_ref, wbd_ref, pscale_ref, pp_ref, pd_ref, pm_ref, wo_ref,
                  gffn_ref, wr_ref, br_ref,
                  x1_ref, hn_ref, route_ref, cnt_ref, *, tm, seq):
    i = pl.program_id(0)
    d = x_ref.shape[1]
    x = x_ref[...]
    h = (_rms(x) * gmix_ref[...]).astype(BF16)

    seq_pos0 = (i * tm) % seq
    halo = jnp.where(seq_pos0 == 0, 0.0, halo_ref[...])
    pooled = _pool_mixer(u_ref[...], halo, seq_pos0).astype(BF16)
    pool_out = jnp.dot(pooled, wbd_ref[...], preferred_element_type=F32) * pscale_ref[...]
    mem_out = _mem_attention(mq_ref[...], kv_ref[...])

    branches = ((pool_out.astype(BF16), pp_ref), (diff_ref[...], pd_ref), (mem_out.astype(BF16), pm_ref))
    merged = jnp.zeros((tm, d), F32)
    for n, (b_in, p_ref) in enumerate(branches):
        gate = jax.nn.sigmoid(jnp.dot(h, wg_ref[:, n * d:(n + 1) * d], preferred_element_type=F32))
        merged = merged + gate * jnp.dot(b_in, p_ref[...], preferred_element_type=F32)
    x1 = x + jnp.dot(merged.astype(BF16), wo_ref[...], preferred_element_type=F32)
    x1_ref[...] = x1

    hn = (_rms(x1) * gffn_ref[...]).astype(BF16)
    hn_ref[...] = hn
    logits = jnp.dot(hn, wr_ref[...], preferred_element_type=F32) + br_ref[...]
    route, cnt = _router(logits)
    route_ref[...] = route
    cnt_ref[...] = jnp.broadcast_to(cnt, cnt_ref.shape)


def _merge(x2, u_pool, mq, kv, diff, gmix, wg, wbd, pscale, pp, pd, pm, wo, gffn, wr, br, seq, n_mem, tm):
    t, d = x2.shape
    nt = t // tm
    tiles_per_seq = seq // tm
    halo_blocks = tm // POOL_HALO

    def tile(i):
        return (i, 0)

    def const(i):
        return (0, 0)

    def wspec(a):
        return pl.BlockSpec(a.shape, const)

    return pl.pallas_call(
        functools.partial(_merge_kernel, tm=tm, seq=seq),
        grid=(nt,),
        in_specs=[
            pl.BlockSpec((tm, d), tile),
            pl.BlockSpec((tm, POOL_WIDTH), tile),
            pl.BlockSpec((POOL_HALO, POOL_WIDTH), lambda i: (jnp.maximum(i * halo_blocks - 1, 0), 0)),
            pl.BlockSpec((tm, MEM_WIDTH), tile),
            pl.BlockSpec((n_mem, 2 * MEM_WIDTH), lambda i: (i // tiles_per_seq, 0)),
            pl.BlockSpec((tm, DA_WIDTH), tile),
            wspec(gmix), wspec(wg), wspec(wbd), wspec(pscale), wspec(pp), wspec(pd), wspec(pm), wspec(wo),
            wspec(gffn), wspec(wr), wspec(br),
        ],
        out_specs=[
            pl.BlockSpec((tm, d), tile),
            pl.BlockSpec((tm, d), tile),
            pl.BlockSpec((tm, ROUTE_LANES), tile),
            pl.BlockSpec((8, ROUTE_LANES), tile),
        ],
        out_shape=[
            jax.ShapeDtypeStruct((t, d), F32),
            jax.ShapeDtypeStruct((t, d), BF16),
            jax.ShapeDtypeStruct((t, ROUTE_LANES), F32),
            jax.ShapeDtypeStruct((nt * 8, ROUTE_LANES), F32),
        ],
        compiler_params=_cparams(1),
        name="merge",
    )(x2, u_pool, u_pool, mq, kv, diff, gmix, wg, wbd, pscale, pp, pd, pm, wo, gffn, wr, br)


def _n_chunks(cnt_ref, step, chunk):
    total = jnp.int32(0)
    for e in range(N_EXPERTS):
        total = total + (cnt_ref[step * N_EXPERTS + e] + (chunk - 1)) // chunk
    return total


def _dispatch_kernel(base_ref, cnt_ref, hn_ref, route_ref, rows_in_ref, rows_ref, rbuf, sem, *, tm):
    del rows_in_ref
    i = pl.program_id(0)
    nt = pl.num_programs(0)
    k = rbuf.shape[1] - DISPATCH_CHUNK
    slot = i % 2

    @pl.when(i == 0)
    def _():
        rbuf[:, k:, :] = jnp.zeros((2, DISPATCH_CHUNK, rbuf.shape[2]), F32)

    route = route_ref[...]
    kio = lax.broadcasted_iota(jnp.int32, (tm, k), 1).astype(F32)
    onehot = jnp.where((kio == route[:, 4:5]) | (kio == route[:, 5:6]), 1.0, 0.0).astype(BF16)
    rbuf[slot, :k, :] = lax.dot_general(onehot, hn_ref[...], (((0,), (0,)), ((), ())), preferred_element_type=F32)

    def chunk_copy(src_row, dst_row):
        return pltpu.make_async_copy(rbuf.at[slot, pl.ds(src_row, DISPATCH_CHUNK)],
                                     rows_ref.at[pl.ds(dst_row, DISPATCH_CHUNK)], sem)

    def drain(step):
        def body(c, carry):
            chunk_copy(0, 0).wait()
            return carry
        lax.fori_loop(0, _n_chunks(cnt_ref, step, DISPATCH_CHUNK), body, 0)

    @pl.when(i > 0)
    def _():
        drain(i - 1)

    loc = jnp.int32(0)
    for e in range(N_EXPERTS):
        n = cnt_ref[i * N_EXPERTS + e]
        base = base_ref[i * N_EXPERTS + e]

        def body(c, carry, loc=loc, base=base):
            chunk_copy(pl.multiple_of(loc + c * DISPATCH_CHUNK, GROUP_ALIGN),
                       pl.multiple_of(base + c * DISPATCH_CHUNK, GROUP_ALIGN)).start()
            return carry

        lax.fori_loop(0, (n + (DISPATCH_CHUNK - 1)) // DISPATCH_CHUNK, body, 0)
        loc = loc + ((n + (GROUP_ALIGN - 1)) // GROUP_ALIGN) * GROUP_ALIGN

    @pl.when(i == nt - 1)
    def _():
        drain(i)


def _dispatch(tile_base, tile_cnt, hn, route, rows_zero, tm):
    t, d = hn.shape
    k = 2 * tm + N_EXPERTS * GROUP_ALIGN
    grid_spec = pltpu.PrefetchScalarGridSpec(
        num_scalar_prefetch=2,
        grid=(t // tm,),
        in_specs=[
            pl.BlockSpec((tm, d), lambda i, b, c: (i, 0)),
            pl.BlockSpec((tm, ROUTE_LANES), lambda i, b, c: (i, 0)),
            pl.BlockSpec(memory_space=pl.ANY),
        ],
        out_specs=pl.BlockSpec(memory_space=pl.ANY),
        scratch_shapes=[pltpu.VMEM((2, k + DISPATCH_CHUNK, d), F32), pltpu.SemaphoreType.DMA(())],
    )
    return pl.pallas_call(
        functools.partial(_dispatch_kernel, tm=tm),
        grid_spec=grid_spec,
        out_shape=jax.ShapeDtypeStruct(rows_zero.shape, rows_zero.dtype),
        input_output_aliases={4: 0},
        compiler_params=_cparams(1),
        name="dispatch",
    )(tile_base, tile_cnt, hn, route, rows_zero)


def _ffn_kernel(be_ref, nused_ref, rows_ref, wg_ref, wu_ref, wd_ref, y_ref):
    in_use = pl.program_id(0) < nused_ref[0]

    @pl.when(in_use)
    def _():
        xb = rows_ref[...].astype(BF16)
        hg = jnp.dot(xb, wg_ref[...], preferred_element_type=F32)
        hu = jnp.dot(xb, wu_ref[...], preferred_element_type=F32)
        hid = (hg * jax.nn.sigmoid(hg) * hu).astype(BF16)
        y_ref[...] = jnp.dot(hid, wd_ref[...], preferred_element_type=F32)

    @pl.when(jnp.logical_not(in_use))
    def _():
        y_ref[...] = jnp.zeros(y_ref.shape, y_ref.dtype)


def _ffn(block_expert, n_used, rows, wg, wu, wd):
    n_rows, d = rows.shape
    hidden = wg.shape[2]

    def blk(i, be, nu):
        return (jnp.minimum(i, nu[0] - 1), 0)

    def wblk(i, be, nu):
        return (be[jnp.minimum(i, nu[0] - 1)], 0, 0)

    grid_spec = pltpu.PrefetchScalarGridSpec(
        num_scalar_prefetch=2,
        grid=(n_rows // MOE_BLOCK,),
        in_specs=[
            pl.BlockSpec((MOE_BLOCK, d), blk),
            pl.BlockSpec((None, d, hidden), wblk),
            pl.BlockSpec((None, d, hidden), wblk),
            pl.BlockSpec((None, hidden, d), wblk),
        ],
        out_specs=pl.BlockSpec((MOE_BLOCK, d), lambda i, be, nu: (i, 0)),
    )
    return pl.pallas_call(
        _ffn_kernel,
        grid_spec=grid_spec,
        out_shape=jax.ShapeDtypeStruct((n_rows, d), F32),
        compiler_params=_cparams(1),
        name="ffn",
    )(block_expert, n_used, rows, wg, wu, wd)


def _combine_kernel(base_ref, cnt_ref, x1_ref, route_ref, g_ref, y_ref, o_ref, ybuf, sem, *, tm):
    i = pl.program_id(0)
    nt = pl.num_programs(0)
    slot = i % 2
    kbuf = ybuf.shape[1]

    def chunk_copy(src_row, dst_slot, dst_row):
        return pltpu.make_async_copy(y_ref.at[pl.ds(src_row, COMBINE_CHUNK)],
                                     ybuf.at[dst_slot, pl.ds(dst_row, COMBINE_CHUNK)], sem.at[dst_slot])

    def fetch(step, dst_slot):
        loc = jnp.int32(0)
        for e in range(N_EXPERTS):
            n = cnt_ref[step * N_EXPERTS + e]
            base = base_ref[step * N_EXPERTS + e]
            n_ch = (n + (COMBINE_CHUNK - 1)) // COMBINE_CHUNK

            def body(c, carry, loc=loc, base=base):
                chunk_copy(pl.multiple_of(base + c * COMBINE_CHUNK, GROUP_ALIGN), dst_slot,
                           pl.multiple_of(loc + c * COMBINE_CHUNK, COMBINE_CHUNK)).start()
                return carry

            lax.fori_loop(0, n_ch, body, 0)
            loc = loc + n_ch * COMBINE_CHUNK

    @pl.when(i == 0)
    def _():
        ybuf[...] = jnp.zeros(ybuf.shape, F32)
        fetch(0, 0)

    @pl.when(i + 1 < nt)
    def _():
        fetch(i + 1, 1 - slot)

    def wait_body(c, carry):
        chunk_copy(0, slot, 0).wait()
        return carry

    lax.fori_loop(0, _n_chunks(cnt_ref, i, COMBINE_CHUNK), wait_body, 0)

    route = route_ref[...]
    kio = lax.broadcasted_iota(jnp.int32, (tm, kbuf), 1).astype(F32)
    pw = jnp.where(kio == route[:, 6:7], route[:, 2:3], 0.0) + jnp.where(kio == route[:, 7:8], route[:, 3:4], 0.0)
    moe = jnp.dot(pw.astype(BF16), ybuf[slot].astype(BF16), preferred_element_type=F32)
    o_ref[...] = _rms(x1_ref[...] + moe) * g_ref[...]


def _combine(tile_base, tile_cnt, x1, route, g, y_rows, tm):
    t, d = x1.shape
    kbuf = 2 * tm + N_EXPERTS * COMBINE_CHUNK
    grid_spec = pltpu.PrefetchScalarGridSpec(
        num_scalar_prefetch=2,
        grid=(t // tm,),
        in_specs=[
            pl.BlockSpec((tm, d), lambda i, b, c: (i, 0)),
            pl.BlockSpec((tm, ROUTE_LANES), lambda i, b, c: (i, 0)),
            pl.BlockSpec((1, d), lambda i, b, c: (0, 0)),
            pl.BlockSpec(memory_space=pl.ANY),
        ],
        out_specs=pl.BlockSpec((tm, d), lambda i, b, c: (i, 0)),
        scratch_shapes=[pltpu.VMEM((2, kbuf, d), F32), pltpu.SemaphoreType.DMA((2,))],
    )
    return pl.pallas_call(
        functools.partial(_combine_kernel, tm=tm),
        grid_spec=grid_spec,
        out_shape=jax.ShapeDtypeStruct((t, d), F32),
        compiler_params=_cparams(1),
        name="combine",
    )(tile_base, tile_cnt, x1, route, g, y_rows)


def _dispatch_plan(cnt_tiles, n_blocks):
    group = ((cnt_tiles + GROUP_ALIGN - 1) // GROUP_ALIGN) * GROUP_ALIGN
    total = jnp.sum(group, axis=0)
    cap = ((total + DISPATCH_CHUNK + MOE_BLOCK - 1) // MOE_BLOCK) * MOE_BLOCK
    pend = jnp.cumsum(cap)
    pstart = pend - cap
    tile_base = pstart[None, :] + jnp.cumsum(group, axis=0) - group
    block_start = jnp.arange(n_blocks, dtype=jnp.int32) * MOE_BLOCK
    block_expert = jnp.minimum(jnp.sum(pend[None, :] <= block_start[:, None], axis=1), N_EXPERTS - 1)
    n_used = (pend[-1] // MOE_BLOCK).reshape(1)
    return (tile_base.reshape(-1).astype(jnp.int32), block_expert.astype(jnp.int32), n_used.astype(jnp.int32))


def kernel(x, mem, norm_mix_g, w_in, pool_w, pool_scale, lam_q1, lam_k1, lam_q2, lam_k2, subln_g, norm_mem_g,
           w_mem_kv, p_pool, p_diff, p_mem, w_o, norm_ffn_g, w_router_group, b_router_group, w_router_expert,
           b_router_expert, w_expert_gate, w_expert_up, w_expert_down, final_g):
    b, s, d = x.shape
    n_mem = mem.shape[1]
    t = b * s
    layer = 0
    x2 = x.reshape(t, d)
    mem2 = mem.reshape(b * n_mem, d)

    w_in_l = w_in[layer]
    w_proj = w_in_l[:, :PROJ_COLS].astype(BF16)
    w_gates = w_in_l[:, PROJ_COLS:].astype(BF16)
    n_win = len(POOL_WINDOWS)
    eye = jnp.eye(n_win, dtype=F32)
    w_bd = (pool_w[layer][:, :, None, :] * eye[:, None, :, None]).reshape(POOL_WIDTH, POOL_WIDTH).astype(BF16)
    n_route = N_GROUPS + N_EXPERTS
    w_r = jnp.concatenate([w_router_group[layer], w_router_expert[layer].reshape(d, N_EXPERTS),
                           jnp.zeros((d, ROUTE_LANES - n_route), F32)], axis=1).astype(BF16)
    b_r = jnp.concatenate([b_router_group[layer], b_router_expert[layer].reshape(N_EXPERTS),
                           jnp.zeros((ROUTE_LANES - n_route,), F32)]).reshape(1, ROUTE_LANES)

    def row(v):
        return v.reshape(1, -1)

    tm = MOE_TILE
    u_pool, qkv, mq = _proj(x2, row(norm_mix_g[layer]), w_proj, tm=512)
    kv = _memkv(mem2, row(norm_mem_g[layer]), w_mem_kv[layer].astype(BF16), tm=512)
    diff = _diff_attn(qkv, row(lam_q1[layer]), row(lam_k1[layer]), row(lam_q2[layer]), row(lam_k2[layer]),
                      row(subln_g[layer]), b, s, tq=256)
    x1, hn, route, cnt = _merge(
        x2, u_pool, mq, kv, diff, row(norm_mix_g[layer]), w_gates, w_bd, row(pool_scale[layer]),
        p_pool[layer].astype(BF16), p_diff[layer].astype(BF16), p_mem[layer].astype(BF16), w_o[layer].astype(BF16),
        row(norm_ffn_g[layer]), w_r, b_r, s, n_mem, tm=tm)

    cnt_tiles = cnt[::8, :N_EXPERTS].astype(jnp.int32)
    tile_cnt = cnt_tiles.reshape(-1)
    n_groups = (t // tm) * N_EXPERTS
    n_blocks = (2 * t + n_groups * (GROUP_ALIGN - 1) + N_EXPERTS * (DISPATCH_CHUNK + MOE_BLOCK - 1)) // MOE_BLOCK + 1
    tile_base, block_expert, n_used = _dispatch_plan(cnt_tiles, n_blocks)
    rows = _dispatch(tile_base, tile_cnt, hn, route, jnp.zeros((n_blocks * MOE_BLOCK, d), F32), tm=tm)
    y_rows = _ffn(block_expert, n_used, rows, w_expert_gate[layer].astype(BF16), w_expert_up[layer].astype(BF16),
                  w_expert_down[layer].astype(BF16))
    out = _combine(tile_base, tile_cnt, x1, route, row(final_g), y_rows, tm=tm)
    return out.reshape(b, s, d)
```

```python
import functools

import jax
import jax.numpy as jnp
from jax import lax
from jax.experimental import pallas as pl
from jax.experimental.pallas import tpu as pltpu

F32 = jnp.float32
BF16 = jnp.bfloat16

RMS_EPS = 1e-6
POOL_WINDOWS = (2, 4, 8, 16)
POOL_GROUP = 64
POOL_WIDTH = 256
POOL_HALO = 16
DA_HEADS = 4
DA_HEAD_DIM = 64
DA_V_DIM = 128
DA_QK_WIDTH = 512
DA_WIDTH = 512
MEM_HEADS = 4
MEM_HEAD_DIM = 64
MEM_WIDTH = 256
N_GROUPS = 4
EXPERTS_PER_GROUP = 4
N_EXPERTS = 16
MOE_BLOCK = 512
MOE_TILE = 256
GROUP_ROWS = 16
LAMBDA_INIT = 0.2
QK_SCALE = 0.125
LOG2_E = 1.4426950408889634
ROUTE_LANES = 128

PROJ_COLS = POOL_WIDTH + 3 * DA_QK_WIDTH + MEM_WIDTH

VMEM_LIMIT = 56 * 1024 * 1024


def _cparams(n_axes):
    return pltpu.CompilerParams(dimension_semantics=("arbitrary",) * n_axes, vmem_limit_bytes=VMEM_LIMIT)


def _rms(x):
    return x * lax.rsqrt(jnp.mean(x * x, axis=-1, keepdims=True) + RMS_EPS)


def _proj_kernel(x_ref, g_ref, w_ref, pool_ref, qkv_ref, mq_ref):
    h = (_rms(x_ref[...]) * g_ref[...]).astype(BF16)
    p = jnp.dot(h, w_ref[...], preferred_element_type=F32)
    pool_ref[...] = p[:, :POOL_WIDTH]
    q_end = POOL_WIDTH + DA_QK_WIDTH
    kv_end = q_end + 2 * DA_QK_WIDTH
    qkv_ref[:, :DA_QK_WIDTH] = (p[:, POOL_WIDTH:q_end] * (QK_SCALE * LOG2_E)).astype(BF16)
    qkv_ref[:, DA_QK_WIDTH:] = p[:, q_end:kv_end].astype(BF16)
    mq_ref[...] = (p[:, kv_end:] * QK_SCALE).astype(BF16)


def _proj(x2, g, w, tm):
    t, d = x2.shape
    return pl.pallas_call(
        _proj_kernel,
        grid=(t // tm,),
        in_specs=[
            pl.BlockSpec((tm, d), lambda i: (i, 0)),
            pl.BlockSpec((1, d), lambda i: (0, 0)),
            pl.BlockSpec((d, PROJ_COLS), lambda i: (0, 0)),
        ],
        out_specs=[
            pl.BlockSpec((tm, POOL_WIDTH), lambda i: (i, 0)),
            pl.BlockSpec((tm, 3 * DA_QK_WIDTH), lambda i: (i, 0)),
            pl.BlockSpec((tm, MEM_WIDTH), lambda i: (i, 0)),
        ],
        out_shape=[
            jax.ShapeDtypeStruct((t, POOL_WIDTH), F32),
            jax.ShapeDtypeStruct((t, 3 * DA_QK_WIDTH), BF16),
            jax.ShapeDtypeStruct((t, MEM_WIDTH), BF16),
        ],
        compiler_params=_cparams(1),
        name="proj",
    )(x2, g, w)


def _memkv_kernel(m_ref, g_ref, w_ref, kv_ref):
    h = (_rms(m_ref[...]) * g_ref[...]).astype(BF16)
    kv_ref[...] = jnp.dot(h, w_ref[...], preferred_element_type=F32).astype(BF16)


def _memkv(mem2, g, w, tm):
    t, d = mem2.shape
    n = w.shape[1]
    return pl.pallas_call(
        _memkv_kernel,
        grid=(t // tm,),
        in_specs=[
            pl.BlockSpec((tm, d), lambda i: (i, 0)),
            pl.BlockSpec((1, d), lambda i: (0, 0)),
            pl.BlockSpec((d, n), lambda i: (0, 0)),
        ],
        out_specs=pl.BlockSpec((tm, n), lambda i: (i, 0)),
        out_shape=jax.ShapeDtypeStruct((t, n), BF16),
        compiler_params=_cparams(1),
        name="mem_kv",
    )(mem2, g, w)


def _diff_attn_kernel(q_ref, k_ref, v_ref, lq1_ref, lk1_ref, lq2_ref, lk2_ref, sg_ref, o_ref,
                      vt_ref, s_ref, m_ref, l_ref, acc_ref, *, tq):
    i = pl.program_id(1)
    n_chunks = vt_ref.shape[1]
    hw = 2 * DA_HEAD_DIM

    @pl.when(i == 0)
    def _():
        for h in range(DA_HEADS):
            for c in range(n_chunks):
                vt_ref[h, c] = v_ref[c * tq:(c + 1) * tq, h * hw:(h + 1) * hw].astype(F32).T.astype(BF16)

    lane = lax.broadcasted_iota(jnp.int32, (tq, hw), 1)
    q2 = []
    for h in range(DA_HEADS):
        q = q_ref[:, h * hw:(h + 1) * hw]
        zero = jnp.zeros_like(q)
        q2.append(jnp.concatenate([jnp.where(lane < DA_HEAD_DIM, q, zero), jnp.where(lane >= DA_HEAD_DIM, q, zero)],
                                  axis=0))

    m_ref[...] = jnp.full(m_ref.shape, -jnp.inf, F32)
    l_ref[...] = jnp.zeros(l_ref.shape, F32)
    acc_ref[...] = jnp.zeros(acc_ref.shape, F32)

    heads = range(DA_HEADS)

    def score_chunk(j, slot):
        start = pl.multiple_of(j * tq, tq)
        for h in heads:
            k = k_ref[pl.ds(start, tq), h * hw:(h + 1) * hw]
            s_ref[slot, h] = lax.dot_general(k, q2[h], (((1,), (1,)), ((), ())), preferred_element_type=F32)

    def softmax_chunk(slot, masked):
        probs, alphas = [], []
        for h in heads:
            s = s_ref[slot, h]
            if masked:
                key = lax.broadcasted_iota(jnp.int32, s.shape, 0)
                qry = lax.broadcasted_iota(jnp.int32, s.shape, 1)
                qry = jnp.where(qry >= tq, qry - tq, qry)
                s = jnp.where(key <= qry, s, -jnp.inf)
            m = m_ref[h]
            m_new = jnp.maximum(m, jnp.max(s, axis=0, keepdims=True))
            alpha = jnp.exp2(m - m_new)
            p = jnp.exp2(s - m_new)
            m_ref[h] = m_new
            l_ref[h] = alpha * l_ref[h] + jnp.sum(p, axis=0, keepdims=True)
            alphas.append(alpha)
            probs.append(p.astype(BF16))
        return probs, alphas

    def value_chunk(j, probs, alphas):
        for h in heads:
            acc_ref[h] = alphas[h] * acc_ref[h] + jnp.dot(vt_ref[h, j], probs[h], preferred_element_type=F32)

    def step(j, slot, masked, prefetch):
        probs, alphas = softmax_chunk(slot, masked)
        if prefetch:
            score_chunk(j + 1, 1 - slot)
        value_chunk(j, probs, alphas)

    score_chunk(0, 0)

    def body(jj, carry):
        step(2 * jj, 0, False, True)
        step(2 * jj + 1, 1, False, True)
        return carry

    lax.fori_loop(0, i // 2, body, 0)

    @pl.when(i % 2 == 1)
    def _():
        step(i - 1, 0, False, True)
        step(i, 1, True, False)

    @pl.when(i % 2 == 0)
    def _():
        step(i, 0, True, False)

    lam = (jnp.exp(jnp.sum(lq1_ref[...] * lk1_ref[...], axis=-1, keepdims=True))
           - jnp.exp(jnp.sum(lq2_ref[...] * lk2_ref[...], axis=-1, keepdims=True)) + LAMBDA_INIT)
    for h in range(DA_HEADS):
        o = acc_ref[h] / l_ref[h]
        o = o[:, :tq] - lam * o[:, tq:]
        o = o * lax.rsqrt(jnp.mean(o * o, axis=0, keepdims=True) + RMS_EPS) * sg_ref[...] * (1.0 - LAMBDA_INIT)
        o_ref[:, h * DA_V_DIM:(h + 1) * DA_V_DIM] = o.T.astype(o_ref.dtype)


def _diff_attn(qkv, lq1, lk1, lq2, lk2, subln_g_col, batch, seq, tq):
    t = qkv.shape[0]
    nq = seq // tq
    lam_spec = pl.BlockSpec((1, DA_HEAD_DIM), lambda b, i: (0, 0))
    return pl.pallas_call(
        functools.partial(_diff_attn_kernel, tq=tq),
        grid=(batch, nq),
        in_specs=[
            pl.BlockSpec((tq, DA_QK_WIDTH), lambda b, i: (b * nq + i, 0)),
            pl.BlockSpec((seq, DA_QK_WIDTH), lambda b, i: (b, 1)),
            pl.BlockSpec((seq, DA_WIDTH), lambda b, i: (b, 2)),
            lam_spec, lam_spec, lam_spec, lam_spec,
            pl.BlockSpec((DA_V_DIM, 1), lambda b, i: (0, 0)),
        ],
        out_specs=pl.BlockSpec((tq, DA_WIDTH), lambda b, i: (b * nq + i, 0)),
        out_shape=jax.ShapeDtypeStruct((t, DA_WIDTH), BF16),
        scratch_shapes=[
            pltpu.VMEM((DA_HEADS, nq, DA_V_DIM, tq), BF16),
            pltpu.VMEM((2, DA_HEADS, tq, 2 * tq), F32),
            pltpu.VMEM((DA_HEADS, 1, 2 * tq), F32),
            pltpu.VMEM((DA_HEADS, 1, 2 * tq), F32),
            pltpu.VMEM((DA_HEADS, DA_V_DIM, 2 * tq), F32),
        ],
        compiler_params=_cparams(2),
        name="diff_attn",
    )(qkv, qkv, qkv, lq1, lk1, lq2, lk2, subln_g_col)


def _pool_mixer(u, halo, seq_pos0):
    tm = u.shape[0]
    ext = jnp.concatenate([halo, u], axis=0)
    n = ext.shape[0]
    sums = []
    acc = ext
    for shift in (1, 2, 4, 8):
        acc = acc + pltpu.roll(acc, shift, 0)
        sums.append(acc[POOL_HALO:n])
    lane = lax.broadcasted_iota(jnp.int32, (tm, POOL_WIDTH), 1)
    pos = seq_pos0 + lax.broadcasted_iota(jnp.int32, (tm, POOL_WIDTH), 0)
    win_sum = sums[3]
    win = jnp.full((tm, POOL_WIDTH), POOL_WINDOWS[3], jnp.int32)
    for gi in (2, 1, 0):
        in_group = lane < (gi + 1) * POOL_GROUP
        win_sum = jnp.where(in_group, sums[gi], win_sum)
        win = jnp.where(in_group, POOL_WINDOWS[gi], win)
    cnt = jnp.minimum(pos + 1, win).astype(F32)
    return win_sum / cnt - u


def _mem_attention(mq, kv):
    tm = mq.shape[0]
    head = lax.broadcasted_iota(jnp.int32, mq.shape, 1) // MEM_HEAD_DIM
    zero = jnp.zeros_like(mq)
    q4 = jnp.concatenate([jnp.where(head == h, mq, zero) for h in range(MEM_HEADS)], axis=0)
    mk = kv[:, :MEM_WIDTH]
    mv = kv[:, MEM_WIDTH:]
    s = lax.dot_general(q4, mk, (((1,), (1,)), ((), ())), preferred_element_type=F32)
    p = jnp.exp(s - jnp.max(s, axis=-1, keepdims=True))
    l = jnp.sum(p, axis=-1, keepdims=True)
    o4 = jnp.dot(p.astype(BF16), mv, preferred_element_type=F32) / l
    out = jnp.zeros((tm, MEM_WIDTH), F32)
    for h in range(MEM_HEADS):
        out = jnp.where(head == h, o4[h * tm:(h + 1) * tm], out)
    return out


def _router(logits):
    tm = logits.shape[0]
    lane = lax.broadcasted_iota(jnp.int32, logits.shape, 1).astype(F32)
    neg = -jnp.inf

    def first_argmax(vals, vmax):
        return jnp.min(jnp.where(vals == vmax, lane, float(ROUTE_LANES)), axis=-1, keepdims=True)

    lg = jnp.where(lane < N_GROUPS, logits, neg)
    mg = jnp.max(lg, axis=-1, keepdims=True)
    g_p = 1.0 / jnp.sum(jnp.exp(lg - mg), axis=-1, keepdims=True)
    g_idx = first_argmax(lg, mg)
    lo = N_GROUPS + EXPERTS_PER_GROUP * g_idx
    le = jnp.where((lane >= lo) & (lane < lo + EXPERTS_PER_GROUP), logits, neg)
    m1 = jnp.max(le, axis=-1, keepdims=True)
    i1 = first_argmax(le, m1)
    le2 = jnp.where(lane == i1, neg, le)
    m2 = jnp.max(le2, axis=-1, keepdims=True)
    i2 = first_argmax(le2, m2)
    r = jnp.exp(m2 - m1)
    w1 = g_p / (1.0 + r)
    w2 = w1 * r
    e1 = i1 - N_GROUPS
    e2 = i2 - N_GROUPS

    oh1 = lane == e1
    oh2 = lane == e2
    oh = jnp.where(oh1 | oh2, 1.0, 0.0)
    earlier = (lax.broadcasted_iota(jnp.int32, (tm, tm), 0) > lax.broadcasted_iota(jnp.int32, (tm, tm), 1))
    rank = jnp.dot(earlier.astype(BF16), oh.astype(BF16), preferred_element_type=F32)
    cnt = jnp.sum(oh, axis=0, keepdims=True)
    group = jnp.floor((cnt + (GROUP_ROWS - 1)) * (1.0 / GROUP_ROWS)) * GROUP_ROWS
    lower = (lax.broadcasted_iota(jnp.int32, (ROUTE_LANES, ROUTE_LANES), 0)
             < lax.broadcasted_iota(jnp.int32, (ROUTE_LANES, ROUTE_LANES), 1)).astype(BF16)
    offs = jnp.dot(jnp.broadcast_to(group, (8, ROUTE_LANES)).astype(BF16), lower,
                   preferred_element_type=F32)
    pos = offs[0:1] + rank

    def pick(onehot):
        return jnp.sum(jnp.where(onehot, pos, 0.0), axis=-1, keepdims=True)

    cols = (e1, e2, w1, w2, pick(oh1), pick(oh2))
    out = jnp.zeros_like(logits)
    for n, col in enumerate(cols):
        out = jnp.where(lane == n, col, out)
    return out, cnt


def _merge_kernel(x_ref, u_ref, halo_ref, mq_ref, kv_ref, diff_ref,
                  gmix_ref, wg_ref, wbd_ref, pscale_ref, pp_ref, pd_ref, pm_ref, wo_ref,
                  gffn_ref, wr_ref, br_ref,
                  x1_ref, hn_ref, route_ref, cnt_ref, *, tm, seq):
    i = pl.program_id(0)
    d = x_ref.shape[1]
    x = x_ref[...]
    h = (_rms(x) * gmix_ref[...]).astype(BF16)

    seq_pos0 = (i * tm) % seq
    halo = jnp.where(seq_pos0 == 0, 0.0, halo_ref[...])
    pooled = _pool_mixer(u_ref[...], halo, seq_pos0).astype(BF16)
    pool_out = jnp.dot(pooled, wbd_ref[...], preferred_element_type=F32) * pscale_ref[...]
    mem_out = _mem_attention(mq_ref[...], kv_ref[...])

    branches = ((pool_out.astype(BF16), pp_ref), (diff_ref[...], pd_ref), (mem_out.astype(BF16), pm_ref))
    merged = jnp.zeros((tm, d), F32)
    for n, (b_in, p_ref) in enumerate(branches):
        gate = jax.nn.sigmoid(jnp.dot(h, wg_ref[:, n * d:(n + 1) * d], preferred_element_type=F32))
        merged = merged + gate * jnp.dot(b_in, p_ref[...], preferred_element_type=F32)
    x1 = x + jnp.dot(merged.astype(BF16), wo_ref[...], preferred_element_type=F32)
    x1_ref[...] = x1

    hn = (_rms(x1) * gffn_ref[...]).astype(BF16)
    hn_ref[...] = hn
    logits = jnp.dot(hn, wr_ref[...], preferred_element_type=F32) + br_ref[...]
    route, cnt = _router(logits)
    route_ref[...] = route
    cnt_ref[...] = jnp.broadcast_to(cnt, cnt_ref.shape)


def _merge(x2, u_pool, mq, kv, diff, gmix, wg, wbd, pscale, pp, pd, pm, wo, gffn, wr, br, seq, n_mem, tm):
    t, d = x2.shape
    nt = t // tm
    tiles_per_seq = seq // tm
    halo_blocks = tm // POOL_HALO

    def tile(i):
        return (i, 0)

    def const(i):
        return (0, 0)

    def wspec(a):
        return pl.BlockSpec(a.shape, const)

    return pl.pallas_call(
        functools.partial(_merge_kernel, tm=tm, seq=seq),
        grid=(nt,),
        in_specs=[
            pl.BlockSpec((tm, d), tile),
            pl.BlockSpec((tm, POOL_WIDTH), tile),
            pl.BlockSpec((POOL_HALO, POOL_WIDTH), lambda i: (jnp.maximum(i * halo_blocks - 1, 0), 0)),
            pl.BlockSpec((tm, MEM_WIDTH), tile),
            pl.BlockSpec((n_mem, 2 * MEM_WIDTH), lambda i: (i // tiles_per_seq, 0)),
            pl.BlockSpec((tm, DA_WIDTH), tile),
            wspec(gmix), wspec(wg), wspec(wbd), wspec(pscale), wspec(pp), wspec(pd), wspec(pm), wspec(wo),
            wspec(gffn), wspec(wr), wspec(br),
        ],
        out_specs=[
            pl.BlockSpec((tm, d), tile),
            pl.BlockSpec((tm, d), tile),
            pl.BlockSpec((tm, ROUTE_LANES), tile),
            pl.BlockSpec((8, ROUTE_LANES), tile),
        ],
        out_shape=[
            jax.ShapeDtypeStruct((t, d), F32),
            jax.ShapeDtypeStruct((t, d), BF16),
            jax.ShapeDtypeStruct((t, ROUTE_LANES), F32),
            jax.ShapeDtypeStruct((nt * 8, ROUTE_LANES), F32),
        ],
        compiler_params=_cparams(1),
        name="merge",
    )(x2, u_pool, u_pool, mq, kv, diff, gmix, wg, wbd, pscale, pp, pd, pm, wo, gffn, wr, br)


def _n_copies(cnt_ref, step):
    total = jnp.int32(0)
    for e in range(N_EXPERTS):
        total = total + (cnt_ref[step * N_EXPERTS + e] + (GROUP_ROWS - 1)) // GROUP_ROWS
    return total


def _dispatch_kernel(base_ref, cnt_ref, fill_ref, hn_ref, route_ref, rows_ref, rbuf, zbuf, sem, zsem, *, tm, n_blocks):
    i = pl.program_id(0)
    nt = pl.num_programs(0)
    k = rbuf.shape[1]
    slot = i % 2

    route = route_ref[...]
    kio = lax.broadcasted_iota(jnp.int32, (tm, k), 1).astype(F32)
    onehot = jnp.where((kio == route[:, 4:5]) | (kio == route[:, 5:6]), 1.0, 0.0).astype(BF16)
    rbuf[slot] = lax.dot_general(onehot, hn_ref[...], (((0,), (0,)), ((), ())),
                                 preferred_element_type=F32).astype(BF16)

    def group_copy(src_row, dst_row):
        return pltpu.make_async_copy(rbuf.at[slot, pl.ds(src_row, GROUP_ROWS)],
                                     rows_ref.at[pl.ds(dst_row, GROUP_ROWS)], sem)

    def drain(step):
        def body(c, carry):
            group_copy(0, 0).wait()
            return carry
        lax.fori_loop(0, _n_copies(cnt_ref, step), body, 0)

    @pl.when(i > 0)
    def _():
        drain(i - 1)

    loc = jnp.int32(0)
    for e in range(N_EXPERTS):
        n_cp = (cnt_ref[i * N_EXPERTS + e] + (GROUP_ROWS - 1)) // GROUP_ROWS
        base = base_ref[i * N_EXPERTS + e]

        def body(c, carry, loc=loc, base=base):
            group_copy(pl.multiple_of(loc + c * GROUP_ROWS, GROUP_ROWS),
                       pl.multiple_of(base + c * GROUP_ROWS, GROUP_ROWS)).start()
            return carry

        lax.fori_loop(0, n_cp, body, 0)
        loc = loc + n_cp * GROUP_ROWS

    @pl.when(i == nt - 1)
    def _():
        drain(i)
        zbuf[...] = jnp.zeros(zbuf.shape, zbuf.dtype)

        def tail_copy(dst_row):
            return pltpu.make_async_copy(zbuf.at[pl.ds(0, GROUP_ROWS)], rows_ref.at[pl.ds(dst_row, GROUP_ROWS)], zsem.at[0])

        def block_copy(blk):
            return pltpu.make_async_copy(zbuf, rows_ref.at[pl.ds(pl.multiple_of(blk * MOE_BLOCK, MOE_BLOCK), MOE_BLOCK)],
                                         zsem.at[1])

        n_used = fill_ref[2 * N_EXPERTS]
        for e in range(N_EXPERTS):
            start = fill_ref[e]

            def body(c, carry, start=start):
                tail_copy(pl.multiple_of(start + c * GROUP_ROWS, GROUP_ROWS)).start()
                return carry

            lax.fori_loop(0, fill_ref[N_EXPERTS + e], body, 0)

        def blk_body(b, carry):
            block_copy(b).start()
            return carry

        lax.fori_loop(n_used, n_blocks, blk_body, 0)

        for e in range(N_EXPERTS):
            def wbody(c, carry):
                tail_copy(0).wait()
                return carry

            lax.fori_loop(0, fill_ref[N_EXPERTS + e], wbody, 0)

        def blk_wait(b, carry):
            block_copy(0).wait()
            return carry

        lax.fori_loop(n_used, n_blocks, blk_wait, 0)


def _dispatch(tile_base, tile_cnt, fill, hn, route, n_blocks, tm):
    t, d = hn.shape
    k = 2 * tm + N_EXPERTS * GROUP_ROWS
    grid_spec = pltpu.PrefetchScalarGridSpec(
        num_scalar_prefetch=3,
        grid=(t // tm,),
        in_specs=[
            pl.BlockSpec((tm, d), lambda i, b, c, f: (i, 0)),
            pl.BlockSpec((tm, ROUTE_LANES), lambda i, b, c, f: (i, 0)),
        ],
        out_specs=pl.BlockSpec(memory_space=pl.ANY),
        scratch_shapes=[pltpu.VMEM((2, k, d), BF16), pltpu.VMEM((MOE_BLOCK, d), BF16),
                        pltpu.SemaphoreType.DMA(()), pltpu.SemaphoreType.DMA((2,))],
    )
    return pl.pallas_call(
        functools.partial(_dispatch_kernel, tm=tm, n_blocks=n_blocks),
        grid_spec=grid_spec,
        out_shape=jax.ShapeDtypeStruct((n_blocks * MOE_BLOCK, d), BF16),
        compiler_params=_cparams(1),
        name="dispatch",
    )(tile_base, tile_cnt, fill, hn, route)


def _ffn_kernel(be_ref, nused_ref, rows_ref, wg_ref, wu_ref, wd_ref, y_ref):
    in_use = pl.program_id(0) < nused_ref[0]

    @pl.when(in_use)
    def _():
        xb = rows_ref[...]
        hg = jnp.dot(xb, wg_ref[...], preferred_element_type=F32)
        hu = jnp.dot(xb, wu_ref[...], preferred_element_type=F32)
        hid = (hg * jax.nn.sigmoid(hg) * hu).astype(BF16)
        y_ref[...] = jnp.dot(hid, wd_ref[...], preferred_element_type=F32).astype(y_ref.dtype)

    @pl.when(jnp.logical_not(in_use))
    def _():
        y_ref[...] = jnp.zeros(y_ref.shape, y_ref.dtype)


def _ffn(block_expert, n_used, rows, wg, wu, wd):
    n_rows, d = rows.shape
    hidden = wg.shape[2]

    def blk(i, be, nu):
        return (jnp.minimum(i, nu[0] - 1), 0)

    def wblk(i, be, nu):
        return (be[jnp.minimum(i, nu[0] - 1)], 0, 0)

    grid_spec = pltpu.PrefetchScalarGridSpec(
        num_scalar_prefetch=2,
        grid=(n_rows // MOE_BLOCK,),
        in_specs=[
            pl.BlockSpec((MOE_BLOCK, d), blk),
            pl.BlockSpec((None, d, hidden), wblk),
            pl.BlockSpec((None, d, hidden), wblk),
            pl.BlockSpec((None, hidden, d), wblk),
        ],
        out_specs=pl.BlockSpec((MOE_BLOCK, d), lambda i, be, nu: (i, 0)),
    )
    return pl.pallas_call(
        _ffn_kernel,
        grid_spec=grid_spec,
        out_shape=jax.ShapeDtypeStruct((n_rows, d), BF16),
        compiler_params=_cparams(1),
        name="ffn",
    )(block_expert, n_used, rows, wg, wu, wd)


def _combine_kernel(base_ref, cnt_ref, x1_ref, route_ref, g_ref, y_ref, o_ref, ybuf, sem, *, tm):
    i = pl.program_id(0)
    nt = pl.num_programs(0)
    slot = i % 2
    kbuf = ybuf.shape[1]

    def group_copy(src_row, dst_slot, dst_row):
        return pltpu.make_async_copy(y_ref.at[pl.ds(src_row, GROUP_ROWS)],
                                     ybuf.at[dst_slot, pl.ds(dst_row, GROUP_ROWS)], sem.at[dst_slot])

    def fetch(step, dst_slot):
        loc = jnp.int32(0)
        for e in range(N_EXPERTS):
            n_cp = (cnt_ref[step * N_EXPERTS + e] + (GROUP_ROWS - 1)) // GROUP_ROWS
            base = base_ref[step * N_EXPERTS + e]

            def body(c, carry, loc=loc, base=base):
                group_copy(pl.multiple_of(base + c * GROUP_ROWS, GROUP_ROWS), dst_slot,
                           pl.multiple_of(loc + c * GROUP_ROWS, GROUP_ROWS)).start()
                return carry

            lax.fori_loop(0, n_cp, body, 0)
            loc = loc + n_cp * GROUP_ROWS

    @pl.when(i == 0)
    def _():
        ybuf[...] = jnp.zeros(ybuf.shape, ybuf.dtype)
        fetch(0, 0)

    @pl.when(i + 1 < nt)
    def _():
        fetch(i + 1, 1 - slot)

    def wait_body(c, carry):
        group_copy(0, slot, 0).wait()
        return carry

    lax.fori_loop(0, _n_copies(cnt_ref, i), wait_body, 0)

    route = route_ref[...]
    kio = lax.broadcasted_iota(jnp.int32, (tm, kbuf), 1).astype(F32)
    pw = jnp.where(kio == route[:, 4:5], route[:, 2:3], 0.0) + jnp.where(kio == route[:, 5:6], route[:, 3:4], 0.0)
    moe = jnp.dot(pw.astype(BF16), ybuf[slot], preferred_element_type=F32)
    o_ref[...] = _rms(x1_ref[...] + moe) * g_ref[...]


def _combine(tile_base, tile_cnt, x1, route, g, y_rows, tm):
    t, d = x1.shape
    kbuf = 2 * tm + N_EXPERTS * GROUP_ROWS
    grid_spec = pltpu.PrefetchScalarGridSpec(
        num_scalar_prefetch=2,
        grid=(t // tm,),
        in_specs=[
            pl.BlockSpec((tm, d), lambda i, b, c: (i, 0)),
            pl.BlockSpec((tm, ROUTE_LANES), lambda i, b, c: (i, 0)),
            pl.BlockSpec((1, d), lambda i, b, c: (0, 0)),
            pl.BlockSpec(memory_space=pl.ANY),
        ],
        out_specs=pl.BlockSpec((tm, d), lambda i, b, c: (i, 0)),
        scratch_shapes=[pltpu.VMEM((2, kbuf, d), BF16), pltpu.SemaphoreType.DMA((2,))],
    )
    return pl.pallas_call(
        functools.partial(_combine_kernel, tm=tm),
        grid_spec=grid_spec,
        out_shape=jax.ShapeDtypeStruct((t, d), F32),
        compiler_params=_cparams(1),
        name="combine",
    )(tile_base, tile_cnt, x1, route, g, y_rows)


def _dispatch_plan(cnt_tiles, n_blocks):
    group = ((cnt_tiles + GROUP_ROWS - 1) // GROUP_ROWS) * GROUP_ROWS
    total = jnp.sum(group, axis=0)
    cap = ((total + MOE_BLOCK - 1) // MOE_BLOCK) * MOE_BLOCK
    pend = jnp.cumsum(cap)
    pstart = pend - cap
    tile_base = pstart[None, :] + jnp.cumsum(group, axis=0) - group
    block_start = jnp.arange(n_blocks, dtype=jnp.int32) * MOE_BLOCK
    block_expert = jnp.minimum(jnp.sum(pend[None, :] <= block_start[:, None], axis=1), N_EXPERTS - 1)
    n_used = jnp.maximum(pend[-1] // MOE_BLOCK, 1).reshape(1)
    fill = jnp.concatenate([pstart + total, (cap - total) // GROUP_ROWS, n_used])
    return (tile_base.reshape(-1).astype(jnp.int32), fill.astype(jnp.int32), block_expert.astype(jnp.int32),
            n_used.astype(jnp.int32))


def kernel(x, mem, norm_mix_g, w_in, pool_w, pool_scale, lam_q1, lam_k1, lam_q2, lam_k2, subln_g, norm_mem_g,
           w_mem_kv, p_pool, p_diff, p_mem, w_o, norm_ffn_g, w_router_group, b_router_group, w_router_expert,
           b_router_expert, w_expert_gate, w_expert_up, w_expert_down, final_g):
    b, s, d = x.shape
    n_mem = mem.shape[1]
    t = b * s
    layer = 0
    x2 = x.reshape(t, d)
    mem2 = mem.reshape(b * n_mem, d)

    w_in_l = w_in[layer]
    w_proj = w_in_l[:, :PROJ_COLS].astype(BF16)
    w_gates = w_in_l[:, PROJ_COLS:].astype(BF16)
    n_win = len(POOL_WINDOWS)
    eye = jnp.eye(n_win, dtype=F32)
    w_bd = (pool_w[layer][:, :, None, :] * eye[:, None, :, None]).reshape(POOL_WIDTH, POOL_WIDTH).astype(BF16)
    n_route = N_GROUPS + N_EXPERTS
    w_r = jnp.concatenate([w_router_group[layer], w_router_expert[layer].reshape(d, N_EXPERTS),
                           jnp.zeros((d, ROUTE_LANES - n_route), F32)], axis=1).astype(BF16)
    b_r = jnp.concatenate([b_router_group[layer], b_router_expert[layer].reshape(N_EXPERTS),
                           jnp.zeros((ROUTE_LANES - n_route,), F32)]).reshape(1, ROUTE_LANES)

    def row(v):
        return v.reshape(1, -1)

    tm = MOE_TILE
    u_pool, qkv, mq = _proj(x2, row(norm_mix_g[layer]), w_proj, tm=512)
    kv = _memkv(mem2, row(norm_mem_g[layer]), w_mem_kv[layer].astype(BF16), tm=512)
    diff = _diff_attn(qkv, row(lam_q1[layer]), row(lam_k1[layer]), row(lam_q2[layer]), row(lam_k2[layer]),
                      subln_g[layer].reshape(-1, 1), b, s, tq=256)
    x1, hn, route, cnt = _merge(
        x2, u_pool, mq, kv, diff, row(norm_mix_g[layer]), w_gates, w_bd, row(pool_scale[layer]),
        p_pool[layer].astype(BF16), p_diff[layer].astype(BF16), p_mem[layer].astype(BF16), w_o[layer].astype(BF16),
        row(norm_ffn_g[layer]), w_r, b_r, s, n_mem, tm=tm)

    cnt_tiles = cnt[::8, :N_EXPERTS].astype(jnp.int32)
    tile_cnt = cnt_tiles.reshape(-1)
    n_groups = (t // tm) * N_EXPERTS
    n_blocks = (2 * t + n_groups * (GROUP_ROWS - 1) + N_EXPERTS * (MOE_BLOCK - 1)) // MOE_BLOCK + 1
    tile_base, fill, block_expert, n_used = _dispatch_plan(cnt_tiles, n_blocks)
    rows = _dispatch(tile_base, tile_cnt, fill, hn, route, n_blocks, tm=tm)
    y_rows = _ffn(block_expert, n_used, rows, w_expert_gate[layer].astype(BF16), w_expert_up[layer].astype(BF16),
                  w_expert_down[layer].astype(BF16))
    out = _combine(tile_base, tile_cnt, x1, route, row(final_g), y_rows, tm=tm)
    return out.reshape(b, s, d)
```

```python
import functools

import jax
import jax.numpy as jnp
from jax import lax
from jax.experimental import pallas as pl
from jax.experimental.pallas import tpu as pltpu

F32 = jnp.float32
BF16 = jnp.bfloat16

RMS_EPS = 1e-6
POOL_WINDOWS = (2, 4, 8, 16)
POOL_GROUP = 64
POOL_WIDTH = 256
POOL_HALO = 16
DA_HEADS = 4
DA_HEAD_DIM = 64
DA_V_DIM = 128
DA_QK_WIDTH = 512
DA_WIDTH = 512
MEM_HEADS = 4
MEM_HEAD_DIM = 64
MEM_WIDTH = 256
N_GROUPS = 4
EXPERTS_PER_GROUP = 4
N_EXPERTS = 16
MOE_BLOCK = 512
MOE_TILE = 256
GROUP_ROWS = 16
LAMBDA_INIT = 0.2
QK_SCALE = 0.125
LOG2_E = 1.4426950408889634
ROUTE_LANES = 128

PROJ_COLS = POOL_WIDTH + 3 * DA_QK_WIDTH + MEM_WIDTH

VMEM_LIMIT = 56 * 1024 * 1024


def _cparams(n_axes):
    return pltpu.CompilerParams(dimension_semantics=("arbitrary",) * n_axes, vmem_limit_bytes=VMEM_LIMIT)


def _rms(x):
    return x * lax.rsqrt(jnp.mean(x * x, axis=-1, keepdims=True) + RMS_EPS)


def _proj_kernel(x_ref, g_ref, w_ref, pool_ref, qkv_ref, mq_ref):
    h = (_rms(x_ref[...]) * g_ref[...]).astype(BF16)
    p = jnp.dot(h, w_ref[...], preferred_element_type=F32)
    pool_ref[...] = p[:, :POOL_WIDTH]
    q_end = POOL_WIDTH + DA_QK_WIDTH
    kv_end = q_end + 2 * DA_QK_WIDTH
    qkv_ref[:, :DA_QK_WIDTH] = (p[:, POOL_WIDTH:q_end] * (QK_SCALE * LOG2_E)).astype(BF16)
    qkv_ref[:, DA_QK_WIDTH:] = p[:, q_end:kv_end].astype(BF16)
    mq_ref[...] = (p[:, kv_end:] * QK_SCALE).astype(BF16)


def _proj(x2, g, w, tm):
    t, d = x2.shape
    return pl.pallas_call(
        _proj_kernel,
        grid=(t // tm,),
        in_specs=[
            pl.BlockSpec((tm, d), lambda i: (i, 0)),
            pl.BlockSpec((1, d), lambda i: (0, 0)),
            pl.BlockSpec((d, PROJ_COLS), lambda i: (0, 0)),
        ],
        out_specs=[
            pl.BlockSpec((tm, POOL_WIDTH), lambda i: (i, 0)),
            pl.BlockSpec((tm, 3 * DA_QK_WIDTH), lambda i: (i, 0)),
            pl.BlockSpec((tm, MEM_WIDTH), lambda i: (i, 0)),
        ],
        out_shape=[
            jax.ShapeDtypeStruct((t, POOL_WIDTH), F32),
            jax.ShapeDtypeStruct((t, 3 * DA_QK_WIDTH), BF16),
            jax.ShapeDtypeStruct((t, MEM_WIDTH), BF16),
        ],
        compiler_params=_cparams(1),
        name="proj",
    )(x2, g, w)


def _memkv_kernel(m_ref, g_ref, w_ref, kv_ref):
    h = (_rms(m_ref[...]) * g_ref[...]).astype(BF16)
    kv_ref[...] = jnp.dot(h, w_ref[...], preferred_element_type=F32).astype(BF16)


def _memkv(mem2, g, w, tm):
    t, d = mem2.shape
    n = w.shape[1]
    return pl.pallas_call(
        _memkv_kernel,
        grid=(t // tm,),
        in_specs=[
            pl.BlockSpec((tm, d), lambda i: (i, 0)),
            pl.BlockSpec((1, d), lambda i: (0, 0)),
            pl.BlockSpec((d, n), lambda i: (0, 0)),
        ],
        out_specs=pl.BlockSpec((tm, n), lambda i: (i, 0)),
        out_shape=jax.ShapeDtypeStruct((t, n), BF16),
        compiler_params=_cparams(1),
        name="mem_kv",
    )(mem2, g, w)


def _diff_attn_kernel(q_ref, k_ref, v_ref, lq1_ref, lk1_ref, lq2_ref, lk2_ref, sg_ref, o_ref,
                      vt_ref, s_ref, m_ref, l_ref, acc_ref, *, tq):
    i = pl.program_id(1)
    n_chunks = vt_ref.shape[1]
    hw = 2 * DA_HEAD_DIM

    @pl.when(i == 0)
    def _():
        for h in range(DA_HEADS):
            for c in range(n_chunks):
                vt_ref[h, c] = v_ref[c * tq:(c + 1) * tq, h * hw:(h + 1) * hw].astype(F32).T.astype(BF16)

    lane = lax.broadcasted_iota(jnp.int32, (tq, hw), 1)
    q2 = []
    for h in range(DA_HEADS):
        q = q_ref[:, h * hw:(h + 1) * hw]
        zero = jnp.zeros_like(q)
        q2.append(jnp.concatenate([jnp.where(lane < DA_HEAD_DIM, q, zero), jnp.where(lane >= DA_HEAD_DIM, q, zero)],
                                  axis=0))

    m_ref[...] = jnp.full(m_ref.shape, -jnp.inf, F32)
    l_ref[...] = jnp.zeros(l_ref.shape, F32)
    acc_ref[...] = jnp.zeros(acc_ref.shape, F32)

    heads = range(DA_HEADS)

    def score_chunk(j, slot):
        start = pl.multiple_of(j * tq, tq)
        for h in heads:
            k = k_ref[pl.ds(start, tq), h * hw:(h + 1) * hw]
            s_ref[slot, h] = lax.dot_general(k, q2[h], (((1,), (1,)), ((), ())), preferred_element_type=F32)

    def softmax_chunk(slot, masked):
        probs, alphas = [], []
        for h in heads:
            s = s_ref[slot, h]
            if masked:
                key = lax.broadcasted_iota(jnp.int32, s.shape, 0)
                qry = lax.broadcasted_iota(jnp.int32, s.shape, 1)
                qry = jnp.where(qry >= tq, qry - tq, qry)
                s = jnp.where(key <= qry, s, -jnp.inf)
            m = m_ref[h]
            m_new = jnp.maximum(m, jnp.max(s, axis=0, keepdims=True))
            alpha = jnp.exp2(m - m_new)
            p = jnp.exp2(s - m_new)
            m_ref[h] = m_new
            l_ref[h] = alpha * l_ref[h] + jnp.sum(p, axis=0, keepdims=True)
            alphas.append(alpha)
            probs.append(p.astype(BF16))
        return probs, alphas

    def value_chunk(j, probs, alphas):
        for h in heads:
            acc_ref[h] = alphas[h] * acc_ref[h] + jnp.dot(vt_ref[h, j], probs[h], preferred_element_type=F32)

    def step(j, slot, masked, prefetch):
        probs, alphas = softmax_chunk(slot, masked)
        if prefetch:
            score_chunk(j + 1, 1 - slot)
        value_chunk(j, probs, alphas)

    score_chunk(0, 0)

    def body(jj, carry):
        step(2 * jj, 0, False, True)
        step(2 * jj + 1, 1, False, True)
        return carry

    lax.fori_loop(0, i // 2, body, 0)

    @pl.when(i % 2 == 1)
    def _():
        step(i - 1, 0, False, True)
        step(i, 1, True, False)

    @pl.when(i % 2 == 0)
    def _():
        step(i, 0, True, False)

    lam = (jnp.exp(jnp.sum(lq1_ref[...] * lk1_ref[...], axis=-1, keepdims=True))
           - jnp.exp(jnp.sum(lq2_ref[...] * lk2_ref[...], axis=-1, keepdims=True)) + LAMBDA_INIT)
    for h in range(DA_HEADS):
        o = acc_ref[h] / l_ref[h]
        o = o[:, :tq] - lam * o[:, tq:]
        o = o * lax.rsqrt(jnp.mean(o * o, axis=0, keepdims=True) + RMS_EPS) * sg_ref[...] * (1.0 - LAMBDA_INIT)
        o_ref[:, h * DA_V_DIM:(h + 1) * DA_V_DIM] = o.T.astype(o_ref.dtype)


def _diff_attn(qkv, lq1, lk1, lq2, lk2, subln_g_col, batch, seq, tq):
    t = qkv.shape[0]
    nq = seq // tq
    lam_spec = pl.BlockSpec((1, DA_HEAD_DIM), lambda b, i: (0, 0))
    return pl.pallas_call(
        functools.partial(_diff_attn_kernel, tq=tq),
        grid=(batch, nq),
        in_specs=[
            pl.BlockSpec((tq, DA_QK_WIDTH), lambda b, i: (b * nq + i, 0)),
            pl.BlockSpec((seq, DA_QK_WIDTH), lambda b, i: (b, 1)),
            pl.BlockSpec((seq, DA_WIDTH), lambda b, i: (b, 2)),
            lam_spec, lam_spec, lam_spec, lam_spec,
            pl.BlockSpec((DA_V_DIM, 1), lambda b, i: (0, 0)),
        ],
        out_specs=pl.BlockSpec((tq, DA_WIDTH), lambda b, i: (b * nq + i, 0)),
        out_shape=jax.ShapeDtypeStruct((t, DA_WIDTH), BF16),
        scratch_shapes=[
            pltpu.VMEM((DA_HEADS, nq, DA_V_DIM, tq), BF16),
            pltpu.VMEM((2, DA_HEADS, tq, 2 * tq), F32),
            pltpu.VMEM((DA_HEADS, 1, 2 * tq), F32),
            pltpu.VMEM((DA_HEADS, 1, 2 * tq), F32),
            pltpu.VMEM((DA_HEADS, DA_V_DIM, 2 * tq), F32),
        ],
        compiler_params=_cparams(2),
        name="diff_attn",
    )(qkv, qkv, qkv, lq1, lk1, lq2, lk2, subln_g_col)


def _pool_mixer(u, halo, seq_pos0):
    tm = u.shape[0]
    ext = jnp.concatenate([halo, u], axis=0)
    n = ext.shape[0]
    sums = []
    acc = ext
    for shift in (1, 2, 4, 8):
        acc = acc + pltpu.roll(acc, shift, 0)
        sums.append(acc[POOL_HALO:n])
    lane = lax.broadcasted_iota(jnp.int32, (tm, POOL_WIDTH), 1)
    pos = seq_pos0 + lax.broadcasted_iota(jnp.int32, (tm, POOL_WIDTH), 0)
    win_sum = sums[3]
    win = jnp.full((tm, POOL_WIDTH), POOL_WINDOWS[3], jnp.int32)
    for gi in (2, 1, 0):
        in_group = lane < (gi + 1) * POOL_GROUP
        win_sum = jnp.where(in_group, sums[gi], win_sum)
        win = jnp.where(in_group, POOL_WINDOWS[gi], win)
    cnt = jnp.minimum(pos + 1, win).astype(F32)
    return win_sum / cnt - u


def _mem_attention(mq, kv):
    tm = mq.shape[0]
    head = lax.broadcasted_iota(jnp.int32, mq.shape, 1) // MEM_HEAD_DIM
    zero = jnp.zeros_like(mq)
    q4 = jnp.concatenate([jnp.where(head == h, mq, zero) for h in range(MEM_HEADS)], axis=0)
    mk = kv[:, :MEM_WIDTH]
    mv = kv[:, MEM_WIDTH:]
    s = lax.dot_general(q4, mk, (((1,), (1,)), ((), ())), preferred_element_type=F32)
    p = jnp.exp(s - jnp.max(s, axis=-1, keepdims=True))
    l = jnp.sum(p, axis=-1, keepdims=True)
    o4 = jnp.dot(p.astype(BF16), mv, preferred_element_type=F32) / l
    out = jnp.zeros((tm, MEM_WIDTH), F32)
    for h in range(MEM_HEADS):
        out = jnp.where(head == h, o4[h * tm:(h + 1) * tm], out)
    return out


def _router_select(logits):
    lane = lax.broadcasted_iota(jnp.int32, logits.shape, 1).astype(F32)
    neg = -jnp.inf

    def first_argmax(vals, vmax):
        return jnp.min(jnp.where(vals == vmax, lane, float(ROUTE_LANES)), axis=-1, keepdims=True)

    lg = jnp.where(lane < N_GROUPS, logits, neg)
    mg = jnp.max(lg, axis=-1, keepdims=True)
    g_p = 1.0 / jnp.sum(jnp.exp(lg - mg), axis=-1, keepdims=True)
    g_idx = first_argmax(lg, mg)
    lo = N_GROUPS + EXPERTS_PER_GROUP * g_idx
    le = jnp.where((lane >= lo) & (lane < lo + EXPERTS_PER_GROUP), logits, neg)
    m1 = jnp.max(le, axis=-1, keepdims=True)
    i1 = first_argmax(le, m1)
    le2 = jnp.where(lane == i1, neg, le)
    m2 = jnp.max(le2, axis=-1, keepdims=True)
    i2 = first_argmax(le2, m2)
    r = jnp.exp(m2 - m1)
    w1 = g_p / (1.0 + r)
    w2 = w1 * r
    return i1 - N_GROUPS, i2 - N_GROUPS, w1, w2


def _router_place(e1, e2, w1, w2):
    tm = e1.shape[0]
    lane = lax.broadcasted_iota(jnp.int32, (tm, ROUTE_LANES), 1).astype(F32)
    oh1 = lane == e1
    oh2 = lane == e2
    oh = jnp.where(oh1 | oh2, 1.0, 0.0)
    earlier = (lax.broadcasted_iota(jnp.int32, (tm, tm), 0) > lax.broadcasted_iota(jnp.int32, (tm, tm), 1))
    rank = jnp.dot(earlier.astype(BF16), oh.astype(BF16), preferred_element_type=F32)
    cnt = jnp.sum(oh, axis=0, keepdims=True)
    group = jnp.floor((cnt + (GROUP_ROWS - 1)) * (1.0 / GROUP_ROWS)) * GROUP_ROWS
    lower = (lax.broadcasted_iota(jnp.int32, (ROUTE_LANES, ROUTE_LANES), 0)
             < lax.broadcasted_iota(jnp.int32, (ROUTE_LANES, ROUTE_LANES), 1)).astype(BF16)
    offs = jnp.dot(jnp.broadcast_to(group, (8, ROUTE_LANES)).astype(BF16), lower,
                   preferred_element_type=F32)
    pos = offs[0:1] + rank

    def pick(onehot):
        return jnp.sum(jnp.where(onehot, pos, 0.0), axis=-1, keepdims=True)

    cols = (e1, e2, w1, w2, pick(oh1), pick(oh2))
    out = jnp.zeros((tm, ROUTE_LANES), F32)
    for n, col in enumerate(cols):
        out = jnp.where(lane == n, col, out)
    return out, cnt


def _merge_kernel(x_ref, u_ref, halo_ref, mq_ref, kv_ref, diff_ref,
                  gmix_ref, wg_ref, wbd_ref, pscale_ref, pp_ref, pd_ref, pm_ref, wo_ref,
                  gffn_ref, wr_ref, br_ref,
                  x1_ref, hn_ref, route_ref, cnt_ref, x1_prev_ref, *, tm, seq):
    i = pl.program_id(0)
    d = x_ref.shape[1]

    @pl.when(i == 0)
    def _():
        x1_prev_ref[...] = jnp.zeros(x1_prev_ref.shape, F32)

    hn = (_rms(x1_prev_ref[...]) * gffn_ref[...]).astype(BF16)
    hn_ref[...] = hn
    logits = jnp.dot(hn, wr_ref[...], preferred_element_type=F32) + br_ref[...]
    selection = _router_select(logits)

    x = x_ref[...]
    h = (_rms(x) * gmix_ref[...]).astype(BF16)

    seq_pos0 = (jnp.minimum(i, pl.num_programs(0) - 2) * tm) % seq
    halo = jnp.where(seq_pos0 == 0, 0.0, halo_ref[...])
    pooled = _pool_mixer(u_ref[...], halo, seq_pos0).astype(BF16)
    pool_out = jnp.dot(pooled, wbd_ref[...], preferred_element_type=F32) * pscale_ref[...]
    mem_out = _mem_attention(mq_ref[...], kv_ref[...])

    branches = ((pool_out.astype(BF16), pp_ref), (diff_ref[...], pd_ref), (mem_out.astype(BF16), pm_ref))
    merged = jnp.zeros((tm, d), F32)
    for n, (b_in, p_ref) in enumerate(branches):
        gate = jax.nn.sigmoid(jnp.dot(h, wg_ref[:, n * d:(n + 1) * d], preferred_element_type=F32))
        merged = merged + gate * jnp.dot(b_in, p_ref[...], preferred_element_type=F32)
    route, cnt = _router_place(*selection)
    route_ref[...] = route
    cnt_ref[...] = jnp.broadcast_to(cnt, cnt_ref.shape)

    x1 = x + jnp.dot(merged.astype(BF16), wo_ref[...], preferred_element_type=F32)
    x1_ref[...] = x1
    x1_prev_ref[...] = x1


def _merge(x2, u_pool, mq, kv, diff, gmix, wg, wbd, pscale, pp, pd, pm, wo, gffn, wr, br, seq, n_mem, tm):
    t, d = x2.shape
    nt = t // tm
    tiles_per_seq = seq // tm
    halo_blocks = tm // POOL_HALO

    def tile(i):
        return (jnp.minimum(i, nt - 1), 0)

    def routed(i):
        return (jnp.maximum(i - 1, 0), 0)

    def const(i):
        return (0, 0)

    def wspec(a):
        return pl.BlockSpec(a.shape, const)

    return pl.pallas_call(
        functools.partial(_merge_kernel, tm=tm, seq=seq),
        grid=(nt + 1,),
        in_specs=[
            pl.BlockSpec((tm, d), tile),
            pl.BlockSpec((tm, POOL_WIDTH), tile),
            pl.BlockSpec((POOL_HALO, POOL_WIDTH), lambda i: (jnp.maximum(jnp.minimum(i, nt - 1) * halo_blocks - 1, 0), 0)),
            pl.BlockSpec((tm, MEM_WIDTH), tile),
            pl.BlockSpec((n_mem, 2 * MEM_WIDTH), lambda i: (jnp.minimum(i, nt - 1) // tiles_per_seq, 0)),
            pl.BlockSpec((tm, DA_WIDTH), tile),
            wspec(gmix), wspec(wg), wspec(wbd), wspec(pscale), wspec(pp), wspec(pd), wspec(pm), wspec(wo),
            wspec(gffn), wspec(wr), wspec(br),
        ],
        out_specs=[
            pl.BlockSpec((tm, d), tile),
            pl.BlockSpec((tm, d), routed),
            pl.BlockSpec((tm, ROUTE_LANES), routed),
            pl.BlockSpec((8, ROUTE_LANES), routed),
        ],
        out_shape=[
            jax.ShapeDtypeStruct((t, d), F32),
            jax.ShapeDtypeStruct((t, d), BF16),
            jax.ShapeDtypeStruct((t, ROUTE_LANES), F32),
            jax.ShapeDtypeStruct((nt * 8, ROUTE_LANES), F32),
        ],
        scratch_shapes=[pltpu.VMEM((tm, d), F32)],
        compiler_params=_cparams(1),
        name="merge",
    )(x2, u_pool, u_pool, mq, kv, diff, gmix, wg, wbd, pscale, pp, pd, pm, wo, gffn, wr, br)


def _n_copies(cnt_ref, step):
    total = jnp.int32(0)
    for e in range(N_EXPERTS):
        total = total + (cnt_ref[step * N_EXPERTS + e] + (GROUP_ROWS - 1)) // GROUP_ROWS
    return total


def _dispatch_kernel(base_ref, cnt_ref, fill_ref, hn_ref, route_ref, rows_ref, rbuf, zbuf, sem, zsem, *, tm, n_blocks):
    i = pl.program_id(0)
    nt = pl.num_programs(0)
    k = rbuf.shape[1]
    slot = i % 2

    route = route_ref[...]
    kio = lax.broadcasted_iota(jnp.int32, (tm, k), 1).astype(F32)
    onehot = jnp.where((kio == route[:, 4:5]) | (kio == route[:, 5:6]), 1.0, 0.0).astype(BF16)
    rbuf[slot] = lax.dot_general(onehot, hn_ref[...], (((0,), (0,)), ((), ())),
                                 preferred_element_type=F32).astype(BF16)

    def group_copy(src_row, dst_row):
        return pltpu.make_async_copy(rbuf.at[slot, pl.ds(src_row, GROUP_ROWS)],
                                     rows_ref.at[pl.ds(dst_row, GROUP_ROWS)], sem)

    def drain(step):
        def body(c, carry):
            group_copy(0, 0).wait()
            return carry
        lax.fori_loop(0, _n_copies(cnt_ref, step), body, 0)

    @pl.when(i > 0)
    def _():
        drain(i - 1)

    loc = jnp.int32(0)
    for e in range(N_EXPERTS):
        n_cp = (cnt_ref[i * N_EXPERTS + e] + (GROUP_ROWS - 1)) // GROUP_ROWS
        base = base_ref[i * N_EXPERTS + e]

        def body(c, carry, loc=loc, base=base):
            group_copy(pl.multiple_of(loc + c * GROUP_ROWS, GROUP_ROWS),
                       pl.multiple_of(base + c * GROUP_ROWS, GROUP_ROWS)).start()
            return carry

        lax.fori_loop(0, n_cp, body, 0)
        loc = loc + n_cp * GROUP_ROWS

    @pl.when(i == nt - 1)
    def _():
        drain(i)
        zbuf[...] = jnp.zeros(zbuf.shape, zbuf.dtype)

        def tail_copy(dst_row):
            return pltpu.make_async_copy(zbuf.at[pl.ds(0, GROUP_ROWS)], rows_ref.at[pl.ds(dst_row, GROUP_ROWS)], zsem.at[0])

        def block_copy(blk):
            return pltpu.make_async_copy(zbuf, rows_ref.at[pl.ds(pl.multiple_of(blk * MOE_BLOCK, MOE_BLOCK), MOE_BLOCK)],
                                         zsem.at[1])

        n_used = fill_ref[2 * N_EXPERTS]
        for e in range(N_EXPERTS):
            start = fill_ref[e]

            def body(c, carry, start=start):
                tail_copy(pl.multiple_of(start + c * GROUP_ROWS, GROUP_ROWS)).start()
                return carry

            lax.fori_loop(0, fill_ref[N_EXPERTS + e], body, 0)

        def blk_body(b, carry):
            block_copy(b).start()
            return carry

        lax.fori_loop(n_used, n_blocks, blk_body, 0)

        for e in range(N_EXPERTS):
            def wbody(c, carry):
                tail_copy(0).wait()
                return carry

            lax.fori_loop(0, fill_ref[N_EXPERTS + e], wbody, 0)

        def blk_wait(b, carry):
            block_copy(0).wait()
            return carry

        lax.fori_loop(n_used, n_blocks, blk_wait, 0)


def _dispatch(tile_base, tile_cnt, fill, hn, route, n_blocks, tm):
    t, d = hn.shape
    k = 2 * tm + N_EXPERTS * GROUP_ROWS
    grid_spec = pltpu.PrefetchScalarGridSpec(
        num_scalar_prefetch=3,
        grid=(t // tm,),
        in_specs=[
            pl.BlockSpec((tm, d), lambda i, b, c, f: (i, 0)),
            pl.BlockSpec((tm, ROUTE_LANES), lambda i, b, c, f: (i, 0)),
        ],
        out_specs=pl.BlockSpec(memory_space=pl.ANY),
        scratch_shapes=[pltpu.VMEM((2, k, d), BF16), pltpu.VMEM((MOE_BLOCK, d), BF16),
                        pltpu.SemaphoreType.DMA(()), pltpu.SemaphoreType.DMA((2,))],
    )
    return pl.pallas_call(
        functools.partial(_dispatch_kernel, tm=tm, n_blocks=n_blocks),
        grid_spec=grid_spec,
        out_shape=jax.ShapeDtypeStruct((n_blocks * MOE_BLOCK, d), BF16),
        compiler_params=_cparams(1),
        name="dispatch",
    )(tile_base, tile_cnt, fill, hn, route)


def _ffn_kernel(be_ref, nused_ref, rows_ref, wg_ref, wu_ref, wd_ref, y_ref):
    in_use = pl.program_id(0) < nused_ref[0]

    @pl.when(in_use)
    def _():
        xb = rows_ref[...]
        hg = jnp.dot(xb, wg_ref[...], preferred_element_type=F32)
        hu = jnp.dot(xb, wu_ref[...], preferred_element_type=F32)
        hid = (hg * jax.nn.sigmoid(hg) * hu).astype(BF16)
        y_ref[...] = jnp.dot(hid, wd_ref[...], preferred_element_type=F32).astype(y_ref.dtype)

    @pl.when(jnp.logical_not(in_use))
    def _():
        y_ref[...] = jnp.zeros(y_ref.shape, y_ref.dtype)


def _ffn(block_expert, n_used, rows, wg, wu, wd):
    n_rows, d = rows.shape
    hidden = wg.shape[2]

    def blk(i, be, nu):
        return (jnp.minimum(i, nu[0] - 1), 0)

    def wblk(i, be, nu):
        return (be[jnp.minimum(i, nu[0] - 1)], 0, 0)

    grid_spec = pltpu.PrefetchScalarGridSpec(
        num_scalar_prefetch=2,
        grid=(n_rows // MOE_BLOCK,),
        in_specs=[
            pl.BlockSpec((MOE_BLOCK, d), blk),
            pl.BlockSpec((None, d, hidden), wblk),
            pl.BlockSpec((None, d, hidden), wblk),
            pl.BlockSpec((None, hidden, d), wblk),
        ],
        out_specs=pl.BlockSpec((MOE_BLOCK, d), lambda i, be, nu: (i, 0)),
    )
    return pl.pallas_call(
        _ffn_kernel,
        grid_spec=grid_spec,
        out_shape=jax.ShapeDtypeStruct((n_rows, d), BF16),
        compiler_params=_cparams(1),
        name="ffn",
    )(block_expert, n_used, rows, wg, wu, wd)


def _combine_kernel(base_ref, cnt_ref, x1_ref, route_ref, g_ref, y_ref, o_ref, ybuf, sem, *, tm):
    i = pl.program_id(0)
    nt = pl.num_programs(0)
    slot = i % 2
    kbuf = ybuf.shape[1]

    def group_copy(src_row, dst_slot, dst_row):
        return pltpu.make_async_copy(y_ref.at[pl.ds(src_row, GROUP_ROWS)],
                                     ybuf.at[dst_slot, pl.ds(dst_row, GROUP_ROWS)], sem.at[dst_slot])

    def fetch(step, dst_slot):
        loc = jnp.int32(0)
        for e in range(N_EXPERTS):
            n_cp = (cnt_ref[step * N_EXPERTS + e] + (GROUP_ROWS - 1)) // GROUP_ROWS
            base = base_ref[step * N_EXPERTS + e]

            def body(c, carry, loc=loc, base=base):
                group_copy(pl.multiple_of(base + c * GROUP_ROWS, GROUP_ROWS), dst_slot,
                           pl.multiple_of(loc + c * GROUP_ROWS, GROUP_ROWS)).start()
                return carry

            lax.fori_loop(0, n_cp, body, 0)
            loc = loc + n_cp * GROUP_ROWS

    @pl.when(i == 0)
    def _():
        ybuf[...] = jnp.zeros(ybuf.shape, ybuf.dtype)
        fetch(0, 0)

    @pl.when(i + 1 < nt)
    def _():
        fetch(i + 1, 1 - slot)

    def wait_body(c, carry):
        group_copy(0, slot, 0).wait()
        return carry

    lax.fori_loop(0, _n_copies(cnt_ref, i), wait_body, 0)

    route = route_ref[...]
    kio = lax.broadcasted_iota(jnp.int32, (tm, kbuf), 1).astype(F32)
    pw = jnp.where(kio == route[:, 4:5], route[:, 2:3], 0.0) + jnp.where(kio == route[:, 5:6], route[:, 3:4], 0.0)
    moe = jnp.dot(pw.astype(BF16), ybuf[slot], preferred_element_type=F32)
    o_ref[...] = _rms(x1_ref[...] + moe) * g_ref[...]


def _combine(tile_base, tile_cnt, x1, route, g, y_rows, tm):
    t, d = x1.shape
    kbuf = 2 * tm + N_EXPERTS * GROUP_ROWS
    grid_spec = pltpu.PrefetchScalarGridSpec(
        num_scalar_prefetch=2,
        grid=(t // tm,),
        in_specs=[
            pl.BlockSpec((tm, d), lambda i, b, c: (i, 0)),
            pl.BlockSpec((tm, ROUTE_LANES), lambda i, b, c: (i, 0)),
            pl.BlockSpec((1, d), lambda i, b, c: (0, 0)),
            pl.BlockSpec(memory_space=pl.ANY),
        ],
        out_specs=pl.BlockSpec((tm, d), lambda i, b, c: (i, 0)),
        scratch_shapes=[pltpu.VMEM((2, kbuf, d), BF16), pltpu.SemaphoreType.DMA((2,))],
    )
    return pl.pallas_call(
        functools.partial(_combine_kernel, tm=tm),
        grid_spec=grid_spec,
        out_shape=jax.ShapeDtypeStruct((t, d), F32),
        compiler_params=_cparams(1),
        name="combine",
    )(tile_base, tile_cnt, x1, route, g, y_rows)


def _dispatch_plan(cnt_tiles, n_blocks):
    group = ((cnt_tiles + GROUP_ROWS - 1) // GROUP_ROWS) * GROUP_ROWS
    total = jnp.sum(group, axis=0)
    cap = ((total + MOE_BLOCK - 1) // MOE_BLOCK) * MOE_BLOCK
    pend = jnp.cumsum(cap)
    pstart = pend - cap
    tile_base = pstart[None, :] + jnp.cumsum(group, axis=0) - group
    block_start = jnp.arange(n_blocks, dtype=jnp.int32) * MOE_BLOCK
    block_expert = jnp.minimum(jnp.sum(pend[None, :] <= block_start[:, None], axis=1), N_EXPERTS - 1)
    n_used = jnp.maximum(pend[-1] // MOE_BLOCK, 1).reshape(1)
    fill = jnp.concatenate([pstart + total, (cap - total) // GROUP_ROWS, n_used])
    return (tile_base.reshape(-1).astype(jnp.int32), fill.astype(jnp.int32), block_expert.astype(jnp.int32),
            n_used.astype(jnp.int32))


def kernel(x, mem, norm_mix_g, w_in, pool_w, pool_scale, lam_q1, lam_k1, lam_q2, lam_k2, subln_g, norm_mem_g,
           w_mem_kv, p_pool, p_diff, p_mem, w_o, norm_ffn_g, w_router_group, b_router_group, w_router_expert,
           b_router_expert, w_expert_gate, w_expert_up, w_expert_down, final_g):
    b, s, d = x.shape
    n_mem = mem.shape[1]
    t = b * s
    layer = 0
    x2 = x.reshape(t, d)
    mem2 = mem.reshape(b * n_mem, d)

    w_in_l = w_in[layer]
    w_proj = w_in_l[:, :PROJ_COLS].astype(BF16)
    w_gates = w_in_l[:, PROJ_COLS:].astype(BF16)
    n_win = len(POOL_WINDOWS)
    eye = jnp.eye(n_win, dtype=F32)
    w_bd = (pool_w[layer][:, :, None, :] * eye[:, None, :, None]).reshape(POOL_WIDTH, POOL_WIDTH).astype(BF16)
    n_route = N_GROUPS + N_EXPERTS
    w_r = jnp.concatenate([w_router_group[layer], w_router_expert[layer].reshape(d, N_EXPERTS),
                           jnp.zeros((d, ROUTE_LANES - n_route), F32)], axis=1).astype(BF16)
    b_r = jnp.concatenate([b_router_group[layer], b_router_expert[layer].reshape(N_EXPERTS),
                           jnp.zeros((ROUTE_LANES - n_route,), F32)]).reshape(1, ROUTE_LANES)

    def row(v):
        return v.reshape(1, -1)

    tm = MOE_TILE
    u_pool, qkv, mq = _proj(x2, row(norm_mix_g[layer]), w_proj, tm=512)
    kv = _memkv(mem2, row(norm_mem_g[layer]), w_mem_kv[layer].astype(BF16), tm=512)
    diff = _diff_attn(qkv, row(lam_q1[layer]), row(lam_k1[layer]), row(lam_q2[layer]), row(lam_k2[layer]),
                      subln_g[layer].reshape(-1, 1), b, s, tq=256)
    x1, hn, route, cnt = _merge(
        x2, u_pool, mq, kv, diff, row(norm_mix_g[layer]), w_gates, w_bd, row(pool_scale[layer]),
        p_pool[layer].astype(BF16), p_diff[layer].astype(BF16), p_mem[layer].astype(BF16), w_o[layer].astype(BF16),
        row(norm_ffn_g[layer]), w_r, b_r, s, n_mem, tm=tm)

    cnt_tiles = cnt[::8, :N_EXPERTS].astype(jnp.int32)
    tile_cnt = cnt_tiles.reshape(-1)
    n_groups = (t // tm) * N_EXPERTS
    n_blocks = (2 * t + n_groups * (GROUP_ROWS - 1) + N_EXPERTS * (MOE_BLOCK - 1)) // MOE_BLOCK + 1
    tile_base, fill, block_expert, n_used = _dispatch_plan(cnt_tiles, n_blocks)
    rows = _dispatch(tile_base, tile_cnt, fill, hn, route, n_blocks, tm=tm)
    y_rows = _ffn(block_expert, n_used, rows, w_expert_gate[layer].astype(BF16), w_expert_up[layer].astype(BF16),
                  w_expert_down[layer].astype(BF16))
    out = _combine(tile_base, tile_cnt, x1, route, row(final_g), y_rows, tm=tm)
    return out.reshape(b, s, d)
```

```python
import functools

import jax
import jax.numpy as jnp
from jax import lax
from jax.experimental import pallas as pl
from jax.experimental.pallas import tpu as pltpu

F32 = jnp.float32
BF16 = jnp.bfloat16

RMS_EPS = 1e-6
POOL_WINDOWS = (2, 4, 8, 16)
POOL_GROUP = 64
POOL_WIDTH = 256
POOL_HALO = 16
DA_HEADS = 4
DA_HEAD_DIM = 64
DA_V_DIM = 128
SUM_ROWS = 16
DA_QK_WIDTH = 512
DA_WIDTH = 512
MEM_HEADS = 4
MEM_HEAD_DIM = 64
MEM_WIDTH = 256
N_GROUPS = 4
EXPERTS_PER_GROUP = 4
N_EXPERTS = 16
MOE_BLOCK = 512
MOE_TILE = 256
GROUP_ROWS = 16
LAMBDA_INIT = 0.2
QK_SCALE = 0.125
LOG2_E = 1.4426950408889634
ROUTE_LANES = 128

PROJ_COLS = POOL_WIDTH + 3 * DA_QK_WIDTH + MEM_WIDTH

VMEM_LIMIT = 56 * 1024 * 1024


def _cparams(n_axes):
    return pltpu.CompilerParams(dimension_semantics=("arbitrary",) * n_axes, vmem_limit_bytes=VMEM_LIMIT)


def _rms(x):
    return x * lax.rsqrt(jnp.mean(x * x, axis=-1, keepdims=True) + RMS_EPS)


def _proj_kernel(x_ref, g_ref, w_ref, pool_ref, qkv_ref, mq_ref):
    h = (_rms(x_ref[...]) * g_ref[...]).astype(BF16)
    p = jnp.dot(h, w_ref[...], preferred_element_type=F32)
    pool_ref[...] = p[:, :POOL_WIDTH]
    q_end = POOL_WIDTH + DA_QK_WIDTH
    kv_end = q_end + 2 * DA_QK_WIDTH
    qkv_ref[:, :DA_QK_WIDTH] = (p[:, POOL_WIDTH:q_end] * (QK_SCALE * LOG2_E)).astype(BF16)
    qkv_ref[:, DA_QK_WIDTH:] = p[:, q_end:kv_end].astype(BF16)
    mq_ref[...] = (p[:, kv_end:] * QK_SCALE).astype(BF16)


def _proj(x2, g, w, tm):
    t, d = x2.shape
    return pl.pallas_call(
        _proj_kernel,
        grid=(t // tm,),
        in_specs=[
            pl.BlockSpec((tm, d), lambda i: (i, 0)),
            pl.BlockSpec((1, d), lambda i: (0, 0)),
            pl.BlockSpec((d, PROJ_COLS), lambda i: (0, 0)),
        ],
        out_specs=[
            pl.BlockSpec((tm, POOL_WIDTH), lambda i: (i, 0)),
            pl.BlockSpec((tm, 3 * DA_QK_WIDTH), lambda i: (i, 0)),
            pl.BlockSpec((tm, MEM_WIDTH), lambda i: (i, 0)),
        ],
        out_shape=[
            jax.ShapeDtypeStruct((t, POOL_WIDTH), F32),
            jax.ShapeDtypeStruct((t, 3 * DA_QK_WIDTH), BF16),
            jax.ShapeDtypeStruct((t, MEM_WIDTH), BF16),
        ],
        compiler_params=_cparams(1),
        name="proj",
    )(x2, g, w)


def _memkv_kernel(m_ref, g_ref, w_ref, kv_ref):
    h = (_rms(m_ref[...]) * g_ref[...]).astype(BF16)
    kv_ref[...] = jnp.dot(h, w_ref[...], preferred_element_type=F32).astype(BF16)


def _memkv(mem2, g, w, tm):
    t, d = mem2.shape
    n = w.shape[1]
    return pl.pallas_call(
        _memkv_kernel,
        grid=(t // tm,),
        in_specs=[
            pl.BlockSpec((tm, d), lambda i: (i, 0)),
            pl.BlockSpec((1, d), lambda i: (0, 0)),
            pl.BlockSpec((d, n), lambda i: (0, 0)),
        ],
        out_specs=pl.BlockSpec((tm, n), lambda i: (i, 0)),
        out_shape=jax.ShapeDtypeStruct((t, n), BF16),
        compiler_params=_cparams(1),
        name="mem_kv",
    )(mem2, g, w)


def _diff_attn_kernel(q_ref, k_ref, v_ref, lq1_ref, lk1_ref, lq2_ref, lk2_ref, sg_ref, o_ref,
                      vt_ref, s_ref, m_ref, acc_ref, *, tq):
    i = pl.program_id(1)
    n_chunks = vt_ref.shape[1]
    hw = 2 * DA_HEAD_DIM

    @pl.when(i == 0)
    def _():
        for h in range(DA_HEADS):
            for c in range(n_chunks):
                vt_ref[h, c, :DA_V_DIM] = v_ref[c * tq:(c + 1) * tq, h * hw:(h + 1) * hw].astype(F32).T.astype(BF16)
                vt_ref[h, c, DA_V_DIM:] = jnp.ones((SUM_ROWS, tq), BF16)

    lane = lax.broadcasted_iota(jnp.int32, (tq, hw), 1)
    q2 = []
    for h in range(DA_HEADS):
        q = q_ref[:, h * hw:(h + 1) * hw]
        zero = jnp.zeros_like(q)
        q2.append(jnp.concatenate([jnp.where(lane < DA_HEAD_DIM, q, zero), jnp.where(lane >= DA_HEAD_DIM, q, zero)],
                                  axis=0))

    m_ref[...] = jnp.full(m_ref.shape, -jnp.inf, F32)
    acc_ref[...] = jnp.zeros(acc_ref.shape, F32)

    heads = range(DA_HEADS)

    def score_chunk(j, slot):
        start = pl.multiple_of(j * tq, tq)
        for h in heads:
            k = k_ref[pl.ds(start, tq), h * hw:(h + 1) * hw]
            s_ref[slot, h] = lax.dot_general(k, q2[h], (((1,), (1,)), ((), ())), preferred_element_type=F32)

    def softmax_chunk(slot, masked):
        probs, alphas = [], []
        for h in heads:
            s = s_ref[slot, h]
            if masked:
                key = lax.broadcasted_iota(jnp.int32, s.shape, 0)
                qry = lax.broadcasted_iota(jnp.int32, s.shape, 1)
                qry = jnp.where(qry >= tq, qry - tq, qry)
                s = jnp.where(key <= qry, s, -jnp.inf)
            m = m_ref[h]
            m_new = jnp.maximum(m, jnp.max(s, axis=0, keepdims=True))
            alpha = jnp.exp2(m - m_new)
            p = jnp.exp2(s - m_new)
            m_ref[h] = m_new
            alphas.append(alpha)
            probs.append(p.astype(BF16))
        return probs, alphas

    def value_chunk(j, probs, alphas):
        for h in heads:
            acc_ref[h] = alphas[h] * acc_ref[h] + jnp.dot(vt_ref[h, j], probs[h], preferred_element_type=F32)

    def step(j, slot, masked, prefetch):
        probs, alphas = softmax_chunk(slot, masked)
        if prefetch:
            score_chunk(j + 1, 1 - slot)
        value_chunk(j, probs, alphas)

    score_chunk(0, 0)

    def body(jj, carry):
        step(2 * jj, 0, False, True)
        step(2 * jj + 1, 1, False, True)
        return carry

    lax.fori_loop(0, i // 2, body, 0)

    @pl.when(i % 2 == 1)
    def _():
        step(i - 1, 0, False, True)
        step(i, 1, True, False)

    @pl.when(i % 2 == 0)
    def _():
        step(i, 0, True, False)

    lam = (jnp.exp(jnp.sum(lq1_ref[...] * lk1_ref[...], axis=-1, keepdims=True))
           - jnp.exp(jnp.sum(lq2_ref[...] * lk2_ref[...], axis=-1, keepdims=True)) + LAMBDA_INIT)
    for h in range(DA_HEADS):
        o = acc_ref[h, :DA_V_DIM] / acc_ref[h, DA_V_DIM:DA_V_DIM + 1]
        o = o[:, :tq] - lam * o[:, tq:]
        o = o * lax.rsqrt(jnp.mean(o * o, axis=0, keepdims=True) + RMS_EPS) * sg_ref[...] * (1.0 - LAMBDA_INIT)
        o_ref[:, h * DA_V_DIM:(h + 1) * DA_V_DIM] = o.T.astype(o_ref.dtype)


def _diff_attn(qkv, lq1, lk1, lq2, lk2, subln_g_col, batch, seq, tq):
    t = qkv.shape[0]
    nq = seq // tq
    lam_spec = pl.BlockSpec((1, DA_HEAD_DIM), lambda b, i: (0, 0))
    return pl.pallas_call(
        functools.partial(_diff_attn_kernel, tq=tq),
        grid=(batch, nq),
        in_specs=[
            pl.BlockSpec((tq, DA_QK_WIDTH), lambda b, i: (b * nq + i, 0)),
            pl.BlockSpec((seq, DA_QK_WIDTH), lambda b, i: (b, 1)),
            pl.BlockSpec((seq, DA_WIDTH), lambda b, i: (b, 2)),
            lam_spec, lam_spec, lam_spec, lam_spec,
            pl.BlockSpec((DA_V_DIM, 1), lambda b, i: (0, 0)),
        ],
        out_specs=pl.BlockSpec((tq, DA_WIDTH), lambda b, i: (b * nq + i, 0)),
        out_shape=jax.ShapeDtypeStruct((t, DA_WIDTH), BF16),
        scratch_shapes=[
            pltpu.VMEM((DA_HEADS, nq, DA_V_DIM + SUM_ROWS, tq), BF16),
            pltpu.VMEM((2, DA_HEADS, tq, 2 * tq), F32),
            pltpu.VMEM((DA_HEADS, 1, 2 * tq), F32),
            pltpu.VMEM((DA_HEADS, DA_V_DIM + SUM_ROWS, 2 * tq), F32),
        ],
        compiler_params=_cparams(2),
        name="diff_attn",
    )(qkv, qkv, qkv, lq1, lk1, lq2, lk2, subln_g_col)


def _pool_mixer(u, halo, seq_pos0):
    tm = u.shape[0]
    ext = jnp.concatenate([halo, u], axis=0)
    n = ext.shape[0]
    sums = []
    acc = ext
    for shift in (1, 2, 4, 8):
        acc = acc + pltpu.roll(acc, shift, 0)
        sums.append(acc[POOL_HALO:n])
    lane = lax.broadcasted_iota(jnp.int32, (tm, POOL_WIDTH), 1)
    pos = seq_pos0 + lax.broadcasted_iota(jnp.int32, (tm, POOL_WIDTH), 0)
    win_sum = sums[3]
    win = jnp.full((tm, POOL_WIDTH), POOL_WINDOWS[3], jnp.int32)
    for gi in (2, 1, 0):
        in_group = lane < (gi + 1) * POOL_GROUP
        win_sum = jnp.where(in_group, sums[gi], win_sum)
        win = jnp.where(in_group, POOL_WINDOWS[gi], win)
    cnt = jnp.minimum(pos + 1, win).astype(F32)
    return win_sum / cnt - u


def _mem_attention(mq, kv):
    tm = mq.shape[0]
    head = lax.broadcasted_iota(jnp.int32, mq.shape, 1) // MEM_HEAD_DIM
    zero = jnp.zeros_like(mq)
    q4 = jnp.concatenate([jnp.where(head == h, mq, zero) for h in range(MEM_HEADS)], axis=0)
    mk = kv[:, :MEM_WIDTH]
    mv = kv[:, MEM_WIDTH:]
    s = lax.dot_general(q4, mk, (((1,), (1,)), ((), ())), preferred_element_type=F32)
    p = jnp.exp(s - jnp.max(s, axis=-1, keepdims=True))
    l = jnp.sum(p, axis=-1, keepdims=True)
    o4 = jnp.dot(p.astype(BF16), mv, preferred_element_type=F32) / l
    out = jnp.zeros((tm, MEM_WIDTH), F32)
    for h in range(MEM_HEADS):
        out = jnp.where(head == h, o4[h * tm:(h + 1) * tm], out)
    return out


def _router_select(logits):
    lane = lax.broadcasted_iota(jnp.int32, logits.shape, 1).astype(F32)
    neg = -jnp.inf

    def first_argmax(vals, vmax):
        return jnp.min(jnp.where(vals == vmax, lane, float(ROUTE_LANES)), axis=-1, keepdims=True)

    lg = jnp.where(lane < N_GROUPS, logits, neg)
    mg = jnp.max(lg, axis=-1, keepdims=True)
    g_p = 1.0 / jnp.sum(jnp.exp(lg - mg), axis=-1, keepdims=True)
    g_idx = first_argmax(lg, mg)
    lo = N_GROUPS + EXPERTS_PER_GROUP * g_idx
    le = jnp.where((lane >= lo) & (lane < lo + EXPERTS_PER_GROUP), logits, neg)
    m1 = jnp.max(le, axis=-1, keepdims=True)
    i1 = first_argmax(le, m1)
    le2 = jnp.where(lane == i1, neg, le)
    m2 = jnp.max(le2, axis=-1, keepdims=True)
    i2 = first_argmax(le2, m2)
    r = jnp.exp(m2 - m1)
    w1 = g_p / (1.0 + r)
    w2 = w1 * r
    return i1 - N_GROUPS, i2 - N_GROUPS, w1, w2


def _router_place(e1, e2, w1, w2):
    tm = e1.shape[0]
    lane = lax.broadcasted_iota(jnp.int32, (tm, ROUTE_LANES), 1).astype(F32)
    oh1 = lane == e1
    oh2 = lane == e2
    oh = jnp.where(oh1 | oh2, 1.0, 0.0)
    earlier = (lax.broadcasted_iota(jnp.int32, (tm, tm), 0) > lax.broadcasted_iota(jnp.int32, (tm, tm), 1))
    rank = jnp.dot(earlier.astype(BF16), oh.astype(BF16), preferred_element_type=F32)
    cnt = jnp.sum(oh, axis=0, keepdims=True)
    group = jnp.floor((cnt + (GROUP_ROWS - 1)) * (1.0 / GROUP_ROWS)) * GROUP_ROWS
    lower = (lax.broadcasted_iota(jnp.int32, (ROUTE_LANES, ROUTE_LANES), 0)
             < lax.broadcasted_iota(jnp.int32, (ROUTE_LANES, ROUTE_LANES), 1)).astype(BF16)
    offs = jnp.dot(jnp.broadcast_to(group, (8, ROUTE_LANES)).astype(BF16), lower,
                   preferred_element_type=F32)
    pos = offs[0:1] + rank

    def pick(onehot):
        return jnp.sum(jnp.where(onehot, pos, 0.0), axis=-1, keepdims=True)

    cols = (e1, e2, w1, w2, pick(oh1), pick(oh2))
    out = jnp.zeros((tm, ROUTE_LANES), F32)
    for n, col in enumerate(cols):
        out = jnp.where(lane == n, col, out)
    return out, cnt


def _merge_kernel(x_ref, u_ref, halo_ref, mq_ref, kv_ref, diff_ref,
                  gmix_ref, wg_ref, wbd_ref, pscale_ref, pp_ref, pd_ref, pm_ref, wo_ref,
                  gffn_ref, wr_ref, br_ref,
                  x1_ref, hn_ref, route_ref, cnt_ref, x1_prev_ref, *, tm, seq):
    i = pl.program_id(0)
    d = x_ref.shape[1]

    @pl.when(i == 0)
    def _():
        x1_prev_ref[...] = jnp.zeros(x1_prev_ref.shape, F32)

    hn = (_rms(x1_prev_ref[...]) * gffn_ref[...]).astype(BF16)
    hn_ref[...] = hn
    logits = jnp.dot(hn, wr_ref[...], preferred_element_type=F32) + br_ref[...]
    selection = _router_select(logits)

    x = x_ref[...]
    h = (_rms(x) * gmix_ref[...]).astype(BF16)

    seq_pos0 = (jnp.minimum(i, pl.num_programs(0) - 2) * tm) % seq
    halo = jnp.where(seq_pos0 == 0, 0.0, halo_ref[...])
    pooled = _pool_mixer(u_ref[...], halo, seq_pos0).astype(BF16)
    pool_out = jnp.dot(pooled, wbd_ref[...], preferred_element_type=F32) * pscale_ref[...]
    mem_out = _mem_attention(mq_ref[...], kv_ref[...])

    branches = ((pool_out.astype(BF16), pp_ref), (diff_ref[...], pd_ref), (mem_out.astype(BF16), pm_ref))
    merged = jnp.zeros((tm, d), F32)
    for n, (b_in, p_ref) in enumerate(branches):
        gate = jax.nn.sigmoid(jnp.dot(h, wg_ref[:, n * d:(n + 1) * d], preferred_element_type=F32))
        merged = merged + gate * jnp.dot(b_in, p_ref[...], preferred_element_type=F32)
    route, cnt = _router_place(*selection)
    route_ref[...] = route
    cnt_ref[...] = jnp.broadcast_to(cnt, cnt_ref.shape)

    x1 = x + jnp.dot(merged.astype(BF16), wo_ref[...], preferred_element_type=F32)
    x1_ref[...] = x1
    x1_prev_ref[...] = x1


def _merge(x2, u_pool, mq, kv, diff, gmix, wg, wbd, pscale, pp, pd, pm, wo, gffn, wr, br, seq, n_mem, tm):
    t, d = x2.shape
    nt = t // tm
    tiles_per_seq = seq // tm
    halo_blocks = tm // POOL_HALO

    def tile(i):
        return (jnp.minimum(i, nt - 1), 0)

    def routed(i):
        return (jnp.maximum(i - 1, 0), 0)

    def const(i):
        return (0, 0)

    def wspec(a):
        return pl.BlockSpec(a.shape, const)

    return pl.pallas_call(
        functools.partial(_merge_kernel, tm=tm, seq=seq),
        grid=(nt + 1,),
        in_specs=[
            pl.BlockSpec((tm, d), tile),
            pl.BlockSpec((tm, POOL_WIDTH), tile),
            pl.BlockSpec((POOL_HALO, POOL_WIDTH), lambda i: (jnp.maximum(jnp.minimum(i, nt - 1) * halo_blocks - 1, 0), 0)),
            pl.BlockSpec((tm, MEM_WIDTH), tile),
            pl.BlockSpec((n_mem, 2 * MEM_WIDTH), lambda i: (jnp.minimum(i, nt - 1) // tiles_per_seq, 0)),
            pl.BlockSpec((tm, DA_WIDTH), tile),
            wspec(gmix), wspec(wg), wspec(wbd), wspec(pscale), wspec(pp), wspec(pd), wspec(pm), wspec(wo),
            wspec(gffn), wspec(wr), wspec(br),
        ],
        out_specs=[
            pl.BlockSpec((tm, d), tile),
            pl.BlockSpec((tm, d), routed),
            pl.BlockSpec((tm, ROUTE_LANES), routed),
            pl.BlockSpec((8, ROUTE_LANES), routed),
        ],
        out_shape=[
            jax.ShapeDtypeStruct((t, d), F32),
            jax.ShapeDtypeStruct((t, d), BF16),
            jax.ShapeDtypeStruct((t, ROUTE_LANES), F32),
            jax.ShapeDtypeStruct((nt * 8, ROUTE_LANES), F32),
        ],
        scratch_shapes=[pltpu.VMEM((tm, d), F32)],
        compiler_params=_cparams(1),
        name="merge",
    )(x2, u_pool, u_pool, mq, kv, diff, gmix, wg, wbd, pscale, pp, pd, pm, wo, gffn, wr, br)


def _dispatch_kernel(piece_ref, npiece_ref, fill_ref, hn_ref, route_ref, rows_ref, rbuf, zbuf, sem, zsem, *, tm, n_blocks):
    i = pl.program_id(0)
    nt = pl.num_programs(0)
    k = rbuf.shape[1]
    pieces = k // GROUP_ROWS
    slot = i % 2

    route = route_ref[...]
    kio = lax.broadcasted_iota(jnp.int32, (tm, k), 1).astype(F32)
    onehot = jnp.where((kio == route[:, 4:5]) | (kio == route[:, 5:6]), 1.0, 0.0).astype(BF16)
    rbuf[slot] = lax.dot_general(onehot, hn_ref[...], (((0,), (0,)), ((), ())),
                                 preferred_element_type=F32).astype(BF16)

    def group_copy(src_row, dst_row):
        return pltpu.make_async_copy(rbuf.at[slot, pl.ds(src_row, GROUP_ROWS)],
                                     rows_ref.at[pl.ds(dst_row, GROUP_ROWS)], sem)

    def drain(step):
        def body(c, carry):
            group_copy(0, 0).wait()
            return carry
        lax.fori_loop(0, npiece_ref[step], body, 0)

    @pl.when(i > 0)
    def _():
        drain(i - 1)

    def issue(f, carry):
        group_copy(pl.multiple_of(f * GROUP_ROWS, GROUP_ROWS),
                   pl.multiple_of(piece_ref[i * pieces + f], GROUP_ROWS)).start()
        return carry

    lax.fori_loop(0, npiece_ref[i], issue, 0)

    @pl.when(i == nt - 1)
    def _():
        drain(i)
        zbuf[...] = jnp.zeros(zbuf.shape, zbuf.dtype)

        def tail_copy(dst_row):
            return pltpu.make_async_copy(zbuf.at[pl.ds(0, GROUP_ROWS)], rows_ref.at[pl.ds(dst_row, GROUP_ROWS)], zsem.at[0])

        def block_copy(blk):
            return pltpu.make_async_copy(zbuf, rows_ref.at[pl.ds(pl.multiple_of(blk * MOE_BLOCK, MOE_BLOCK), MOE_BLOCK)],
                                         zsem.at[1])

        n_used = fill_ref[2 * N_EXPERTS]
        for e in range(N_EXPERTS):
            start = fill_ref[e]

            def body(c, carry, start=start):
                tail_copy(pl.multiple_of(start + c * GROUP_ROWS, GROUP_ROWS)).start()
                return carry

            lax.fori_loop(0, fill_ref[N_EXPERTS + e], body, 0)

        def blk_body(b, carry):
            block_copy(b).start()
            return carry

        lax.fori_loop(n_used, n_blocks, blk_body, 0)

        for e in range(N_EXPERTS):
            def wbody(c, carry):
                tail_copy(0).wait()
                return carry

            lax.fori_loop(0, fill_ref[N_EXPERTS + e], wbody, 0)

        def blk_wait(b, carry):
            block_copy(0).wait()
            return carry

        lax.fori_loop(n_used, n_blocks, blk_wait, 0)


def _tile_rows(tm):
    return 2 * tm + N_EXPERTS * GROUP_ROWS


def _dispatch(piece_rows, n_pieces, fill, hn, route, n_blocks, tm):
    t, d = hn.shape
    k = _tile_rows(tm)
    grid_spec = pltpu.PrefetchScalarGridSpec(
        num_scalar_prefetch=3,
        grid=(t // tm,),
        in_specs=[
            pl.BlockSpec((tm, d), lambda i, b, c, f: (i, 0)),
            pl.BlockSpec((tm, ROUTE_LANES), lambda i, b, c, f: (i, 0)),
        ],
        out_specs=pl.BlockSpec(memory_space=pl.ANY),
        scratch_shapes=[pltpu.VMEM((2, k, d), BF16), pltpu.VMEM((MOE_BLOCK, d), BF16),
                        pltpu.SemaphoreType.DMA(()), pltpu.SemaphoreType.DMA((2,))],
    )
    return pl.pallas_call(
        functools.partial(_dispatch_kernel, tm=tm, n_blocks=n_blocks),
        grid_spec=grid_spec,
        out_shape=jax.ShapeDtypeStruct((n_blocks * MOE_BLOCK, d), BF16),
        compiler_params=_cparams(1),
        name="dispatch",
    )(piece_rows, n_pieces, fill, hn, route)


def _ffn_kernel(be_ref, nused_ref, rows_ref, wg_ref, wu_ref, wd_ref, y_ref):
    in_use = pl.program_id(0) < nused_ref[0]

    @pl.when(in_use)
    def _():
        xb = rows_ref[...]
        hg = jnp.dot(xb, wg_ref[...], preferred_element_type=F32)
        hu = jnp.dot(xb, wu_ref[...], preferred_element_type=F32)
        hid = (hg * jax.nn.sigmoid(hg) * hu).astype(BF16)
        y_ref[...] = jnp.dot(hid, wd_ref[...], preferred_element_type=F32).astype(y_ref.dtype)

    @pl.when(jnp.logical_not(in_use))
    def _():
        y_ref[...] = jnp.zeros(y_ref.shape, y_ref.dtype)


def _ffn(block_expert, n_used, rows, wg, wu, wd):
    n_rows, d = rows.shape
    hidden = wg.shape[2]

    def blk(i, be, nu):
        return (jnp.minimum(i, nu[0] - 1), 0)

    def wblk(i, be, nu):
        return (be[jnp.minimum(i, nu[0] - 1)], 0, 0)

    grid_spec = pltpu.PrefetchScalarGridSpec(
        num_scalar_prefetch=2,
        grid=(n_rows // MOE_BLOCK,),
        in_specs=[
            pl.BlockSpec((MOE_BLOCK, d), blk),
            pl.BlockSpec((None, d, hidden), wblk),
            pl.BlockSpec((None, d, hidden), wblk),
            pl.BlockSpec((None, hidden, d), wblk),
        ],
        out_specs=pl.BlockSpec((MOE_BLOCK, d), lambda i, be, nu: (i, 0)),
    )
    return pl.pallas_call(
        _ffn_kernel,
        grid_spec=grid_spec,
        out_shape=jax.ShapeDtypeStruct((n_rows, d), BF16),
        compiler_params=_cparams(1),
        name="ffn",
    )(block_expert, n_used, rows, wg, wu, wd)


def _combine_kernel(piece_ref, npiece_ref, x1_ref, route_ref, g_ref, y_ref, o_ref, ybuf, sem, *, tm):
    i = pl.program_id(0)
    nt = pl.num_programs(0)
    slot = i % 2
    kbuf = ybuf.shape[1]
    pieces = kbuf // GROUP_ROWS

    def group_copy(src_row, dst_slot, dst_row):
        return pltpu.make_async_copy(y_ref.at[pl.ds(src_row, GROUP_ROWS)],
                                     ybuf.at[dst_slot, pl.ds(dst_row, GROUP_ROWS)], sem.at[dst_slot])

    def fetch(step, dst_slot):
        def body(f, carry):
            group_copy(pl.multiple_of(piece_ref[step * pieces + f], GROUP_ROWS), dst_slot,
                       pl.multiple_of(f * GROUP_ROWS, GROUP_ROWS)).start()
            return carry

        lax.fori_loop(0, npiece_ref[step], body, 0)

    @pl.when(i == 0)
    def _():
        ybuf[...] = jnp.zeros(ybuf.shape, ybuf.dtype)
        fetch(0, 0)

    @pl.when(i + 1 < nt)
    def _():
        fetch(i + 1, 1 - slot)

    def wait_body(c, carry):
        group_copy(0, slot, 0).wait()
        return carry

    lax.fori_loop(0, npiece_ref[i], wait_body, 0)

    route = route_ref[...]
    kio = lax.broadcasted_iota(jnp.int32, (tm, kbuf), 1).astype(F32)
    pw = jnp.where(kio == route[:, 4:5], route[:, 2:3], 0.0) + jnp.where(kio == route[:, 5:6], route[:, 3:4], 0.0)
    moe = jnp.dot(pw.astype(BF16), ybuf[slot], preferred_element_type=F32)
    o_ref[...] = _rms(x1_ref[...] + moe) * g_ref[...]


def _combine(piece_rows, n_pieces, x1, route, g, y_rows, tm):
    t, d = x1.shape
    kbuf = _tile_rows(tm)
    grid_spec = pltpu.PrefetchScalarGridSpec(
        num_scalar_prefetch=2,
        grid=(t // tm,),
        in_specs=[
            pl.BlockSpec((tm, d), lambda i, b, c: (i, 0)),
            pl.BlockSpec((tm, ROUTE_LANES), lambda i, b, c: (i, 0)),
            pl.BlockSpec((1, d), lambda i, b, c: (0, 0)),
            pl.BlockSpec(memory_space=pl.ANY),
        ],
        out_specs=pl.BlockSpec((tm, d), lambda i, b, c: (i, 0)),
        scratch_shapes=[pltpu.VMEM((2, kbuf, d), BF16), pltpu.SemaphoreType.DMA((2,))],
    )
    return pl.pallas_call(
        functools.partial(_combine_kernel, tm=tm),
        grid_spec=grid_spec,
        out_shape=jax.ShapeDtypeStruct((t, d), F32),
        compiler_params=_cparams(1),
        name="combine",
    )(piece_rows, n_pieces, x1, route, g, y_rows)


def _dispatch_plan(cnt_tiles, n_blocks, pieces):
    n_pc = (cnt_tiles + GROUP_ROWS - 1) // GROUP_ROWS
    group = n_pc * GROUP_ROWS
    total = jnp.sum(group, axis=0)
    cap = ((total + MOE_BLOCK - 1) // MOE_BLOCK) * MOE_BLOCK
    pend = jnp.cumsum(cap)
    pstart = pend - cap
    tile_base = pstart[None, :] + jnp.cumsum(group, axis=0) - group
    block_start = jnp.arange(n_blocks, dtype=jnp.int32) * MOE_BLOCK
    block_expert = jnp.minimum(jnp.sum(pend[None, :] <= block_start[:, None], axis=1), N_EXPERTS - 1)
    n_used = jnp.maximum(pend[-1] // MOE_BLOCK, 1).reshape(1)
    fill = jnp.concatenate([pstart + total, (cap - total) // GROUP_ROWS, n_used])

    pc_end = jnp.cumsum(n_pc, axis=1)
    pc_start = pc_end - n_pc
    f = jnp.arange(pieces, dtype=jnp.int32)
    owner = jnp.sum(pc_end[:, :, None] <= f[None, None, :], axis=1)
    is_owner = owner[:, None, :] == jnp.arange(N_EXPERTS, dtype=jnp.int32)[None, :, None]
    piece_rows = jnp.sum(jnp.where(is_owner, (tile_base - pc_start * GROUP_ROWS)[:, :, None] + f * GROUP_ROWS, 0), axis=1)
    return (piece_rows.reshape(-1).astype(jnp.int32), pc_end[:, -1].astype(jnp.int32), fill.astype(jnp.int32),
            block_expert.astype(jnp.int32), n_used.astype(jnp.int32))


def kernel(x, mem, norm_mix_g, w_in, pool_w, pool_scale, lam_q1, lam_k1, lam_q2, lam_k2, subln_g, norm_mem_g,
           w_mem_kv, p_pool, p_diff, p_mem, w_o, norm_ffn_g, w_router_group, b_router_group, w_router_expert,
           b_router_expert, w_expert_gate, w_expert_up, w_expert_down, final_g):
    b, s, d = x.shape
    n_mem = mem.shape[1]
    t = b * s
    layer = 0
    x2 = x.reshape(t, d)
    mem2 = mem.reshape(b * n_mem, d)

    w_in_l = w_in[layer]
    w_proj = w_in_l[:, :PROJ_COLS].astype(BF16)
    w_gates = w_in_l[:, PROJ_COLS:].astype(BF16)
    n_win = len(POOL_WINDOWS)
    eye = jnp.eye(n_win, dtype=F32)
    w_bd = (pool_w[layer][:, :, None, :] * eye[:, None, :, None]).reshape(POOL_WIDTH, POOL_WIDTH).astype(BF16)
    n_route = N_GROUPS + N_EXPERTS
    w_r = jnp.concatenate([w_router_group[layer], w_router_expert[layer].reshape(d, N_EXPERTS),
                           jnp.zeros((d, ROUTE_LANES - n_route), F32)], axis=1).astype(BF16)
    b_r = jnp.concatenate([b_router_group[layer], b_router_expert[layer].reshape(N_EXPERTS),
                           jnp.zeros((ROUTE_LANES - n_route,), F32)]).reshape(1, ROUTE_LANES)

    def row(v):
        return v.reshape(1, -1)

    tm = MOE_TILE
    u_pool, qkv, mq = _proj(x2, row(norm_mix_g[layer]), w_proj, tm=512)
    kv = _memkv(mem2, row(norm_mem_g[layer]), w_mem_kv[layer].astype(BF16), tm=512)
    diff = _diff_attn(qkv, row(lam_q1[layer]), row(lam_k1[layer]), row(lam_q2[layer]), row(lam_k2[layer]),
                      subln_g[layer].reshape(-1, 1), b, s, tq=256)
    x1, hn, route, cnt = _merge(
        x2, u_pool, mq, kv, diff, row(norm_mix_g[layer]), w_gates, w_bd, row(pool_scale[layer]),
        p_pool[layer].astype(BF16), p_diff[layer].astype(BF16), p_mem[layer].astype(BF16), w_o[layer].astype(BF16),
        row(norm_ffn_g[layer]), w_r, b_r, s, n_mem, tm=tm)

    cnt_tiles = cnt[::8, :N_EXPERTS].astype(jnp.int32)
    n_groups = (t // tm) * N_EXPERTS
    n_blocks = (2 * t + n_groups * (GROUP_ROWS - 1) + N_EXPERTS * (MOE_BLOCK - 1)) // MOE_BLOCK + 1
    piece_rows, n_pieces, fill, block_expert, n_used = _dispatch_plan(cnt_tiles, n_blocks, _tile_rows(tm) // GROUP_ROWS)
    rows = _dispatch(piece_rows, n_pieces, fill, hn, route, n_blocks, tm=tm)
    y_rows = _ffn(block_expert, n_used, rows, w_expert_gate[layer].astype(BF16), w_expert_up[layer].astype(BF16),
                  w_expert_down[layer].astype(BF16))
    out = _combine(piece_rows, n_pieces, x1, route, row(final_g), y_rows, tm=tm)
    return out.reshape(b, s, d)
```

```python
import functools

import jax
import jax.numpy as jnp
from jax import lax
from jax.experimental import pallas as pl
from jax.experimental.pallas import tpu as pltpu

F32 = jnp.float32
BF16 = jnp.bfloat16

RMS_EPS = 1e-6
POOL_WINDOWS = (2, 4, 8, 16)
POOL_GROUP = 64
POOL_WIDTH = 256
POOL_HALO = 16
DA_HEADS = 4
DA_HEAD_DIM = 64
DA_V_DIM = 128
SUM_ROWS = 16
DA_QK_WIDTH = 512
DA_WIDTH = 512
MEM_HEADS = 4
MEM_HEAD_DIM = 64
MEM_WIDTH = 256
N_GROUPS = 4
EXPERTS_PER_GROUP = 4
N_EXPERTS = 16
MOE_BLOCK = 1024
MOE_TILE = 256
GROUP_ROWS = 16
LAMBDA_INIT = 0.2
QK_SCALE = 0.125
LOG2_E = 1.4426950408889634
ROUTE_LANES = 128

PROJ_COLS = POOL_WIDTH + 3 * DA_QK_WIDTH + MEM_WIDTH

VMEM_LIMIT = 56 * 1024 * 1024


def _cparams(n_axes):
    return pltpu.CompilerParams(dimension_semantics=("arbitrary",) * n_axes, vmem_limit_bytes=VMEM_LIMIT)


def _rms(x):
    return x * lax.rsqrt(jnp.mean(x * x, axis=-1, keepdims=True) + RMS_EPS)


def _proj_kernel(x_ref, g_ref, w_ref, pool_ref, qkv_ref, mq_ref):
    h = (_rms(x_ref[...]) * g_ref[...]).astype(BF16)
    p = jnp.dot(h, w_ref[...], preferred_element_type=F32)
    pool_ref[...] = p[:, :POOL_WIDTH]
    q_end = POOL_WIDTH + DA_QK_WIDTH
    kv_end = q_end + 2 * DA_QK_WIDTH
    qkv_ref[:, :DA_QK_WIDTH] = (p[:, POOL_WIDTH:q_end] * (QK_SCALE * LOG2_E)).astype(BF16)
    qkv_ref[:, DA_QK_WIDTH:] = p[:, q_end:kv_end].astype(BF16)
    mq_ref[...] = (p[:, kv_end:] * QK_SCALE).astype(BF16)


def _proj(x2, g, w, tm):
    t, d = x2.shape
    return pl.pallas_call(
        _proj_kernel,
        grid=(t // tm,),
        in_specs=[
            pl.BlockSpec((tm, d), lambda i: (i, 0)),
            pl.BlockSpec((1, d), lambda i: (0, 0)),
            pl.BlockSpec((d, PROJ_COLS), lambda i: (0, 0)),
        ],
        out_specs=[
            pl.BlockSpec((tm, POOL_WIDTH), lambda i: (i, 0)),
            pl.BlockSpec((tm, 3 * DA_QK_WIDTH), lambda i: (i, 0)),
            pl.BlockSpec((tm, MEM_WIDTH), lambda i: (i, 0)),
        ],
        out_shape=[
            jax.ShapeDtypeStruct((t, POOL_WIDTH), F32),
            jax.ShapeDtypeStruct((t, 3 * DA_QK_WIDTH), BF16),
            jax.ShapeDtypeStruct((t, MEM_WIDTH), BF16),
        ],
        compiler_params=_cparams(1),
        name="proj",
    )(x2, g, w)


def _memkv_kernel(m_ref, g_ref, w_ref, kv_ref):
    h = (_rms(m_ref[...]) * g_ref[...]).astype(BF16)
    kv_ref[...] = jnp.dot(h, w_ref[...], preferred_element_type=F32).astype(BF16)


def _memkv(mem2, g, w, tm):
    t, d = mem2.shape
    n = w.shape[1]
    return pl.pallas_call(
        _memkv_kernel,
        grid=(t // tm,),
        in_specs=[
            pl.BlockSpec((tm, d), lambda i: (i, 0)),
            pl.BlockSpec((1, d), lambda i: (0, 0)),
            pl.BlockSpec((d, n), lambda i: (0, 0)),
        ],
        out_specs=pl.BlockSpec((tm, n), lambda i: (i, 0)),
        out_shape=jax.ShapeDtypeStruct((t, n), BF16),
        compiler_params=_cparams(1),
        name="mem_kv",
    )(mem2, g, w)


def _diff_attn_kernel(q_ref, k_ref, v_ref, lq1_ref, lk1_ref, lq2_ref, lk2_ref, sg_ref, o_ref,
                      vt_ref, s_ref, m_ref, acc_ref, *, tq):
    i = pl.program_id(1)
    n_chunks = vt_ref.shape[1]
    hw = 2 * DA_HEAD_DIM

    @pl.when(i == 0)
    def _():
        for h in range(DA_HEADS):
            for c in range(n_chunks):
                vt_ref[h, c, :DA_V_DIM] = v_ref[c * tq:(c + 1) * tq, h * hw:(h + 1) * hw].astype(F32).T.astype(BF16)
                vt_ref[h, c, DA_V_DIM:] = jnp.ones((SUM_ROWS, tq), BF16)

    lane = lax.broadcasted_iota(jnp.int32, (tq, hw), 1)
    q2 = []
    for h in range(DA_HEADS):
        q = q_ref[:, h * hw:(h + 1) * hw]
        zero = jnp.zeros_like(q)
        q2.append(jnp.concatenate([jnp.where(lane < DA_HEAD_DIM, q, zero), jnp.where(lane >= DA_HEAD_DIM, q, zero)],
                                  axis=0))

    m_ref[...] = jnp.full(m_ref.shape, -jnp.inf, F32)
    acc_ref[...] = jnp.zeros(acc_ref.shape, F32)

    heads = range(DA_HEADS)

    def score_chunk(j, slot):
        start = pl.multiple_of(j * tq, tq)
        for h in heads:
            k = k_ref[pl.ds(start, tq), h * hw:(h + 1) * hw]
            s_ref[slot, h] = lax.dot_general(k, q2[h], (((1,), (1,)), ((), ())), preferred_element_type=F32)

    def softmax_chunk(slot, masked):
        probs, alphas = [], []
        for h in heads:
            s = s_ref[slot, h]
            if masked:
                key = lax.broadcasted_iota(jnp.int32, s.shape, 0)
                qry = lax.broadcasted_iota(jnp.int32, s.shape, 1)
                qry = jnp.where(qry >= tq, qry - tq, qry)
                s = jnp.where(key <= qry, s, -jnp.inf)
            m = m_ref[h]
            m_new = jnp.maximum(m, jnp.max(s, axis=0, keepdims=True))
            alpha = jnp.exp2(m - m_new)
            p = jnp.exp2(s - m_new)
            m_ref[h] = m_new
            alphas.append(alpha)
            probs.append(p.astype(BF16))
        return probs, alphas

    def value_chunk(j, probs, alphas):
        for h in heads:
            acc_ref[h] = alphas[h] * acc_ref[h] + jnp.dot(vt_ref[h, j], probs[h], preferred_element_type=F32)

    def step(j, slot, masked, prefetch):
        probs, alphas = softmax_chunk(slot, masked)
        if prefetch:
            score_chunk(j + 1, 1 - slot)
        value_chunk(j, probs, alphas)

    score_chunk(0, 0)

    def body(jj, carry):
        step(2 * jj, 0, False, True)
        step(2 * jj + 1, 1, False, True)
        return carry

    lax.fori_loop(0, i // 2, body, 0)

    @pl.when(i % 2 == 1)
    def _():
        step(i - 1, 0, False, True)
        step(i, 1, True, False)

    @pl.when(i % 2 == 0)
    def _():
        step(i, 0, True, False)

    lam = (jnp.exp(jnp.sum(lq1_ref[...] * lk1_ref[...], axis=-1, keepdims=True))
           - jnp.exp(jnp.sum(lq2_ref[...] * lk2_ref[...], axis=-1, keepdims=True)) + LAMBDA_INIT)
    for h in range(DA_HEADS):
        o = acc_ref[h, :DA_V_DIM] / acc_ref[h, DA_V_DIM:DA_V_DIM + 1]
        o = o[:, :tq] - lam * o[:, tq:]
        o = o * lax.rsqrt(jnp.mean(o * o, axis=0, keepdims=True) + RMS_EPS) * sg_ref[...] * (1.0 - LAMBDA_INIT)
        o_ref[:, h * DA_V_DIM:(h + 1) * DA_V_DIM] = o.T.astype(o_ref.dtype)


def _diff_attn(qkv, lq1, lk1, lq2, lk2, subln_g_col, batch, seq, tq):
    t = qkv.shape[0]
    nq = seq // tq
    lam_spec = pl.BlockSpec((1, DA_HEAD_DIM), lambda b, i: (0, 0))
    return pl.pallas_call(
        functools.partial(_diff_attn_kernel, tq=tq),
        grid=(batch, nq),
        in_specs=[
            pl.BlockSpec((tq, DA_QK_WIDTH), lambda b, i: (b * nq + i, 0)),
            pl.BlockSpec((seq, DA_QK_WIDTH), lambda b, i: (b, 1)),
            pl.BlockSpec((seq, DA_WIDTH), lambda b, i: (b, 2)),
            lam_spec, lam_spec, lam_spec, lam_spec,
            pl.BlockSpec((DA_V_DIM, 1), lambda b, i: (0, 0)),
        ],
        out_specs=pl.BlockSpec((tq, DA_WIDTH), lambda b, i: (b * nq + i, 0)),
        out_shape=jax.ShapeDtypeStruct((t, DA_WIDTH), BF16),
        scratch_shapes=[
            pltpu.VMEM((DA_HEADS, nq, DA_V_DIM + SUM_ROWS, tq), BF16),
            pltpu.VMEM((2, DA_HEADS, tq, 2 * tq), F32),
            pltpu.VMEM((DA_HEADS, 1, 2 * tq), F32),
            pltpu.VMEM((DA_HEADS, DA_V_DIM + SUM_ROWS, 2 * tq), F32),
        ],
        compiler_params=_cparams(2),
        name="diff_attn",
    )(qkv, qkv, qkv, lq1, lk1, lq2, lk2, subln_g_col)


def _pool_mixer(u, halo, seq_pos0):
    tm = u.shape[0]
    ext = jnp.concatenate([halo, u], axis=0)
    n = ext.shape[0]
    sums = []
    acc = ext
    for shift in (1, 2, 4, 8):
        acc = acc + pltpu.roll(acc, shift, 0)
        sums.append(acc[POOL_HALO:n])
    lane = lax.broadcasted_iota(jnp.int32, (tm, POOL_WIDTH), 1)
    pos = seq_pos0 + lax.broadcasted_iota(jnp.int32, (tm, POOL_WIDTH), 0)
    win_sum = sums[3]
    win = jnp.full((tm, POOL_WIDTH), POOL_WINDOWS[3], jnp.int32)
    for gi in (2, 1, 0):
        in_group = lane < (gi + 1) * POOL_GROUP
        win_sum = jnp.where(in_group, sums[gi], win_sum)
        win = jnp.where(in_group, POOL_WINDOWS[gi], win)
    cnt = jnp.minimum(pos + 1, win).astype(F32)
    return win_sum / cnt - u


def _mem_attention(mq, kv):
    tm = mq.shape[0]
    head = lax.broadcasted_iota(jnp.int32, mq.shape, 1) // MEM_HEAD_DIM
    zero = jnp.zeros_like(mq)
    q4 = jnp.concatenate([jnp.where(head == h, mq, zero) for h in range(MEM_HEADS)], axis=0)
    mk = kv[:, :MEM_WIDTH]
    mv = kv[:, MEM_WIDTH:]
    s = lax.dot_general(q4, mk, (((1,), (1,)), ((), ())), preferred_element_type=F32)
    p = jnp.exp(s - jnp.max(s, axis=-1, keepdims=True))
    l = jnp.sum(p, axis=-1, keepdims=True)
    o4 = jnp.dot(p.astype(BF16), mv, preferred_element_type=F32) / l
    out = jnp.zeros((tm, MEM_WIDTH), F32)
    for h in range(MEM_HEADS):
        out = jnp.where(head == h, o4[h * tm:(h + 1) * tm], out)
    return out


def _router_select(logits):
    lane = lax.broadcasted_iota(jnp.int32, logits.shape, 1).astype(F32)
    neg = -jnp.inf

    def first_argmax(vals, vmax):
        return jnp.min(jnp.where(vals == vmax, lane, float(ROUTE_LANES)), axis=-1, keepdims=True)

    lg = jnp.where(lane < N_GROUPS, logits, neg)
    mg = jnp.max(lg, axis=-1, keepdims=True)
    g_p = 1.0 / jnp.sum(jnp.exp(lg - mg), axis=-1, keepdims=True)
    g_idx = first_argmax(lg, mg)
    lo = N_GROUPS + EXPERTS_PER_GROUP * g_idx
    le = jnp.where((lane >= lo) & (lane < lo + EXPERTS_PER_GROUP), logits, neg)
    m1 = jnp.max(le, axis=-1, keepdims=True)
    i1 = first_argmax(le, m1)
    le2 = jnp.where(lane == i1, neg, le)
    m2 = jnp.max(le2, axis=-1, keepdims=True)
    i2 = first_argmax(le2, m2)
    r = jnp.exp(m2 - m1)
    w1 = g_p / (1.0 + r)
    w2 = w1 * r
    return i1 - N_GROUPS, i2 - N_GROUPS, w1, w2


def _router_place(e1, e2, w1, w2):
    tm = e1.shape[0]
    lane = lax.broadcasted_iota(jnp.int32, (tm, ROUTE_LANES), 1).astype(F32)
    oh1 = lane == e1
    oh2 = lane == e2
    oh = jnp.where(oh1 | oh2, 1.0, 0.0)
    earlier = (lax.broadcasted_iota(jnp.int32, (tm, tm), 0) > lax.broadcasted_iota(jnp.int32, (tm, tm), 1))
    rank = jnp.dot(earlier.astype(BF16), oh.astype(BF16), preferred_element_type=F32)
    cnt = jnp.sum(oh, axis=0, keepdims=True)
    group = jnp.floor((cnt + (GROUP_ROWS - 1)) * (1.0 / GROUP_ROWS)) * GROUP_ROWS
    lower = (lax.broadcasted_iota(jnp.int32, (ROUTE_LANES, ROUTE_LANES), 0)
             < lax.broadcasted_iota(jnp.int32, (ROUTE_LANES, ROUTE_LANES), 1)).astype(BF16)
    offs = jnp.dot(jnp.broadcast_to(group, (8, ROUTE_LANES)).astype(BF16), lower,
                   preferred_element_type=F32)
    pos = offs[0:1] + rank

    def pick(onehot):
        return jnp.sum(jnp.where(onehot, pos, 0.0), axis=-1, keepdims=True)

    cols = (e1, e2, w1, w2, pick(oh1), pick(oh2))
    out = jnp.zeros((tm, ROUTE_LANES), F32)
    for n, col in enumerate(cols):
        out = jnp.where(lane == n, col, out)
    return out, cnt


def _merge_kernel(x_ref, u_ref, halo_ref, mq_ref, kv_ref, diff_ref,
                  gmix_ref, wg_ref, wbd_ref, pscale_ref, pp_ref, pd_ref, pm_ref, wo_ref,
                  gffn_ref, wr_ref, br_ref,
                  x1_ref, hn_ref, route_ref, cnt_ref, x1_prev_ref, *, tm, seq):
    i = pl.program_id(0)
    d = x_ref.shape[1]

    @pl.when(i == 0)
    def _():
        x1_prev_ref[...] = jnp.zeros(x1_prev_ref.shape, F32)

    hn = (_rms(x1_prev_ref[...]) * gffn_ref[...]).astype(BF16)
    hn_ref[...] = hn
    logits = jnp.dot(hn, wr_ref[...], preferred_element_type=F32) + br_ref[...]
    selection = _router_select(logits)

    x = x_ref[...]
    h = (_rms(x) * gmix_ref[...]).astype(BF16)

    seq_pos0 = (jnp.minimum(i, pl.num_programs(0) - 2) * tm) % seq
    halo = jnp.where(seq_pos0 == 0, 0.0, halo_ref[...])
    pooled = _pool_mixer(u_ref[...], halo, seq_pos0).astype(BF16)
    pool_out = jnp.dot(pooled, wbd_ref[...], preferred_element_type=F32) * pscale_ref[...]
    mem_out = _mem_attention(mq_ref[...], kv_ref[...])

    branches = ((pool_out.astype(BF16), pp_ref), (diff_ref[...], pd_ref), (mem_out.astype(BF16), pm_ref))
    merged = jnp.zeros((tm, d), F32)
    for n, (b_in, p_ref) in enumerate(branches):
        gate = jax.nn.sigmoid(jnp.dot(h, wg_ref[:, n * d:(n + 1) * d], preferred_element_type=F32))
        merged = merged + gate * jnp.dot(b_in, p_ref[...], preferred_element_type=F32)
    route, cnt = _router_place(*selection)
    route_ref[...] = route
    cnt_ref[...] = jnp.broadcast_to(cnt, cnt_ref.shape)

    x1 = x + jnp.dot(merged.astype(BF16), wo_ref[...], preferred_element_type=F32)
    x1_ref[...] = x1
    x1_prev_ref[...] = x1


def _merge(x2, u_pool, mq, kv, diff, gmix, wg, wbd, pscale, pp, pd, pm, wo, gffn, wr, br, seq, n_mem, tm):
    t, d = x2.shape
    nt = t // tm
    tiles_per_seq = seq // tm
    halo_blocks = tm // POOL_HALO

    def tile(i):
        return (jnp.minimum(i, nt - 1), 0)

    def routed(i):
        return (jnp.maximum(i - 1, 0), 0)

    def const(i):
        return (0, 0)

    def wspec(a):
        return pl.BlockSpec(a.shape, const)

    return pl.pallas_call(
        functools.partial(_merge_kernel, tm=tm, seq=seq),
        grid=(nt + 1,),
        in_specs=[
            pl.BlockSpec((tm, d), tile),
            pl.BlockSpec((tm, POOL_WIDTH), tile),
            pl.BlockSpec((POOL_HALO, POOL_WIDTH), lambda i: (jnp.maximum(jnp.minimum(i, nt - 1) * halo_blocks - 1, 0), 0)),
            pl.BlockSpec((tm, MEM_WIDTH), tile),
            pl.BlockSpec((n_mem, 2 * MEM_WIDTH), lambda i: (jnp.minimum(i, nt - 1) // tiles_per_seq, 0)),
            pl.BlockSpec((tm, DA_WIDTH), tile),
            wspec(gmix), wspec(wg), wspec(wbd), wspec(pscale), wspec(pp), wspec(pd), wspec(pm), wspec(wo),
            wspec(gffn), wspec(wr), wspec(br),
        ],
        out_specs=[
            pl.BlockSpec((tm, d), tile),
            pl.BlockSpec((tm, d), routed),
            pl.BlockSpec((tm, ROUTE_LANES), routed),
            pl.BlockSpec((8, ROUTE_LANES), routed),
        ],
        out_shape=[
            jax.ShapeDtypeStruct((t, d), F32),
            jax.ShapeDtypeStruct((t, d), BF16),
            jax.ShapeDtypeStruct((t, ROUTE_LANES), F32),
            jax.ShapeDtypeStruct((nt * 8, ROUTE_LANES), F32),
        ],
        scratch_shapes=[pltpu.VMEM((tm, d), F32)],
        compiler_params=_cparams(1),
        name="merge",
    )(x2, u_pool, u_pool, mq, kv, diff, gmix, wg, wbd, pscale, pp, pd, pm, wo, gffn, wr, br)


def _dispatch_kernel(piece_ref, npiece_ref, fill_ref, hn_ref, route_ref, rows_ref, rbuf, zbuf, sem, zsem, *, tm, n_blocks):
    i = pl.program_id(0)
    nt = pl.num_programs(0)
    k = rbuf.shape[1]
    pieces = k // GROUP_ROWS
    slot = i % 2

    route = route_ref[...]
    kio = lax.broadcasted_iota(jnp.int32, (tm, k), 1).astype(F32)
    onehot = jnp.where((kio == route[:, 4:5]) | (kio == route[:, 5:6]), 1.0, 0.0).astype(BF16)
    rbuf[slot] = lax.dot_general(onehot, hn_ref[...], (((0,), (0,)), ((), ())),
                                 preferred_element_type=F32).astype(BF16)

    def group_copy(src_row, dst_row):
        return pltpu.make_async_copy(rbuf.at[slot, pl.ds(src_row, GROUP_ROWS)],
                                     rows_ref.at[pl.ds(dst_row, GROUP_ROWS)], sem)

    def drain(step):
        def body(c, carry):
            group_copy(0, 0).wait()
            return carry
        lax.fori_loop(0, npiece_ref[step], body, 0)

    @pl.when(i > 0)
    def _():
        drain(i - 1)

    def issue(f, carry):
        group_copy(pl.multiple_of(f * GROUP_ROWS, GROUP_ROWS),
                   pl.multiple_of(piece_ref[i * pieces + f], GROUP_ROWS)).start()
        return carry

    lax.fori_loop(0, npiece_ref[i], issue, 0)

    @pl.when(i == nt - 1)
    def _():
        drain(i)
        zbuf[...] = jnp.zeros(zbuf.shape, zbuf.dtype)

        def tail_copy(dst_row):
            return pltpu.make_async_copy(zbuf.at[pl.ds(0, GROUP_ROWS)], rows_ref.at[pl.ds(dst_row, GROUP_ROWS)], zsem.at[0])

        def block_copy(blk):
            return pltpu.make_async_copy(zbuf, rows_ref.at[pl.ds(pl.multiple_of(blk * MOE_BLOCK, MOE_BLOCK), MOE_BLOCK)],
                                         zsem.at[1])

        n_used = fill_ref[2 * N_EXPERTS]
        for e in range(N_EXPERTS):
            start = fill_ref[e]

            def body(c, carry, start=start):
                tail_copy(pl.multiple_of(start + c * GROUP_ROWS, GROUP_ROWS)).start()
                return carry

            lax.fori_loop(0, fill_ref[N_EXPERTS + e], body, 0)

        def blk_body(b, carry):
            block_copy(b).start()
            return carry

        lax.fori_loop(n_used, n_blocks, blk_body, 0)

        for e in range(N_EXPERTS):
            def wbody(c, carry):
                tail_copy(0).wait()
                return carry

            lax.fori_loop(0, fill_ref[N_EXPERTS + e], wbody, 0)

        def blk_wait(b, carry):
            block_copy(0).wait()
            return carry

        lax.fori_loop(n_used, n_blocks, blk_wait, 0)


def _tile_rows(tm):
    return 2 * tm + N_EXPERTS * GROUP_ROWS


def _dispatch(piece_rows, n_pieces, fill, hn, route, n_blocks, tm):
    t, d = hn.shape
    k = _tile_rows(tm)
    grid_spec = pltpu.PrefetchScalarGridSpec(
        num_scalar_prefetch=3,
        grid=(t // tm,),
        in_specs=[
            pl.BlockSpec((tm, d), lambda i, b, c, f: (i, 0)),
            pl.BlockSpec((tm, ROUTE_LANES), lambda i, b, c, f: (i, 0)),
        ],
        out_specs=pl.BlockSpec(memory_space=pl.ANY),
        scratch_shapes=[pltpu.VMEM((2, k, d), BF16), pltpu.VMEM((MOE_BLOCK, d), BF16),
                        pltpu.SemaphoreType.DMA(()), pltpu.SemaphoreType.DMA((2,))],
    )
    return pl.pallas_call(
        functools.partial(_dispatch_kernel, tm=tm, n_blocks=n_blocks),
        grid_spec=grid_spec,
        out_shape=jax.ShapeDtypeStruct((n_blocks * MOE_BLOCK, d), BF16),
        compiler_params=_cparams(1),
        name="dispatch",
    )(piece_rows, n_pieces, fill, hn, route)


def _ffn_kernel(be_ref, nused_ref, rows_ref, wg_ref, wu_ref, wd_ref, y_ref):
    in_use = pl.program_id(0) < nused_ref[0]

    @pl.when(in_use)
    def _():
        xb = rows_ref[...]
        hg = jnp.dot(xb, wg_ref[...], preferred_element_type=F32)
        hu = jnp.dot(xb, wu_ref[...], preferred_element_type=F32)
        hid = (hg * jax.nn.sigmoid(hg) * hu).astype(BF16)
        y_ref[...] = jnp.dot(hid, wd_ref[...], preferred_element_type=F32).astype(y_ref.dtype)

    @pl.when(jnp.logical_not(in_use))
    def _():
        y_ref[...] = jnp.zeros(y_ref.shape, y_ref.dtype)


def _ffn(block_expert, n_used, rows, wg, wu, wd):
    n_rows, d = rows.shape
    hidden = wg.shape[2]

    def blk(i, be, nu):
        return (jnp.minimum(i, nu[0] - 1), 0)

    def wblk(i, be, nu):
        return (be[jnp.minimum(i, nu[0] - 1)], 0, 0)

    grid_spec = pltpu.PrefetchScalarGridSpec(
        num_scalar_prefetch=2,
        grid=(n_rows // MOE_BLOCK,),
        in_specs=[
            pl.BlockSpec((MOE_BLOCK, d), blk),
            pl.BlockSpec((None, d, hidden), wblk),
            pl.BlockSpec((None, d, hidden), wblk),
            pl.BlockSpec((None, hidden, d), wblk),
        ],
        out_specs=pl.BlockSpec((MOE_BLOCK, d), lambda i, be, nu: (i, 0)),
    )
    return pl.pallas_call(
        _ffn_kernel,
        grid_spec=grid_spec,
        out_shape=jax.ShapeDtypeStruct((n_rows, d), BF16),
        compiler_params=_cparams(1),
        name="ffn",
    )(block_expert, n_used, rows, wg, wu, wd)


def _combine_kernel(piece_ref, npiece_ref, x1_ref, route_ref, g_ref, y_ref, o_ref, ybuf, sem, *, tm):
    i = pl.program_id(0)
    nt = pl.num_programs(0)
    slot = i % 2
    kbuf = ybuf.shape[1]
    pieces = kbuf // GROUP_ROWS

    def group_copy(src_row, dst_slot, dst_row):
        return pltpu.make_async_copy(y_ref.at[pl.ds(src_row, GROUP_ROWS)],
                                     ybuf.at[dst_slot, pl.ds(dst_row, GROUP_ROWS)], sem.at[dst_slot])

    def fetch(step, dst_slot):
        def body(f, carry):
            group_copy(pl.multiple_of(piece_ref[step * pieces + f], GROUP_ROWS), dst_slot,
                       pl.multiple_of(f * GROUP_ROWS, GROUP_ROWS)).start()
            return carry

        lax.fori_loop(0, npiece_ref[step], body, 0)

    @pl.when(i == 0)
    def _():
        ybuf[...] = jnp.zeros(ybuf.shape, ybuf.dtype)
        fetch(0, 0)

    @pl.when(i + 1 < nt)
    def _():
        fetch(i + 1, 1 - slot)

    def wait_body(c, carry):
        group_copy(0, slot, 0).wait()
        return carry

    lax.fori_loop(0, npiece_ref[i], wait_body, 0)

    route = route_ref[...]
    kio = lax.broadcasted_iota(jnp.int32, (tm, kbuf), 1).astype(F32)
    pw = jnp.where(kio == route[:, 4:5], route[:, 2:3], 0.0) + jnp.where(kio == route[:, 5:6], route[:, 3:4], 0.0)
    moe = jnp.dot(pw.astype(BF16), ybuf[slot], preferred_element_type=F32)
    o_ref[...] = _rms(x1_ref[...] + moe) * g_ref[...]


def _combine(piece_rows, n_pieces, x1, route, g, y_rows, tm):
    t, d = x1.shape
    kbuf = _tile_rows(tm)
    grid_spec = pltpu.PrefetchScalarGridSpec(
        num_scalar_prefetch=2,
        grid=(t // tm,),
        in_specs=[
            pl.BlockSpec((tm, d), lambda i, b, c: (i, 0)),
            pl.BlockSpec((tm, ROUTE_LANES), lambda i, b, c: (i, 0)),
            pl.BlockSpec((1, d), lambda i, b, c: (0, 0)),
            pl.BlockSpec(memory_space=pl.ANY),
        ],
        out_specs=pl.BlockSpec((tm, d), lambda i, b, c: (i, 0)),
        scratch_shapes=[pltpu.VMEM((2, kbuf, d), BF16), pltpu.SemaphoreType.DMA((2,))],
    )
    return pl.pallas_call(
        functools.partial(_combine_kernel, tm=tm),
        grid_spec=grid_spec,
        out_shape=jax.ShapeDtypeStruct((t, d), F32),
        compiler_params=_cparams(1),
        name="combine",
    )(piece_rows, n_pieces, x1, route, g, y_rows)


def _dispatch_plan(cnt_tiles, n_blocks, pieces):
    n_pc = (cnt_tiles + GROUP_ROWS - 1) // GROUP_ROWS
    group = n_pc * GROUP_ROWS
    total = jnp.sum(group, axis=0)
    cap = ((total + MOE_BLOCK - 1) // MOE_BLOCK) * MOE_BLOCK
    pend = jnp.cumsum(cap)
    pstart = pend - cap
    tile_base = pstart[None, :] + jnp.cumsum(group, axis=0) - group
    block_start = jnp.arange(n_blocks, dtype=jnp.int32) * MOE_BLOCK
    block_expert = jnp.minimum(jnp.sum(pend[None, :] <= block_start[:, None], axis=1), N_EXPERTS - 1)
    n_used = jnp.maximum(pend[-1] // MOE_BLOCK, 1).reshape(1)
    fill = jnp.concatenate([pstart + total, (cap - total) // GROUP_ROWS, n_used])

    pc_end = jnp.cumsum(n_pc, axis=1)
    pc_start = pc_end - n_pc
    f = jnp.arange(pieces, dtype=jnp.int32)
    owner = jnp.sum(pc_end[:, :, None] <= f[None, None, :], axis=1)
    is_owner = owner[:, None, :] == jnp.arange(N_EXPERTS, dtype=jnp.int32)[None, :, None]
    piece_rows = jnp.sum(jnp.where(is_owner, (tile_base - pc_start * GROUP_ROWS)[:, :, None] + f * GROUP_ROWS, 0), axis=1)
    return (piece_rows.reshape(-1).astype(jnp.int32), pc_end[:, -1].astype(jnp.int32), fill.astype(jnp.int32),
            block_expert.astype(jnp.int32), n_used.astype(jnp.int32))


def kernel(x, mem, norm_mix_g, w_in, pool_w, pool_scale, lam_q1, lam_k1, lam_q2, lam_k2, subln_g, norm_mem_g,
           w_mem_kv, p_pool, p_diff, p_mem, w_o, norm_ffn_g, w_router_group, b_router_group, w_router_expert,
           b_router_expert, w_expert_gate, w_expert_up, w_expert_down, final_g):
    b, s, d = x.shape
    n_mem = mem.shape[1]
    t = b * s
    layer = 0
    x2 = x.reshape(t, d)
    mem2 = mem.reshape(b * n_mem, d)

    w_in_l = w_in[layer]
    w_proj = w_in_l[:, :PROJ_COLS].astype(BF16)
    w_gates = w_in_l[:, PROJ_COLS:].astype(BF16)
    n_win = len(POOL_WINDOWS)
    eye = jnp.eye(n_win, dtype=F32)
    w_bd = (pool_w[layer][:, :, None, :] * eye[:, None, :, None]).reshape(POOL_WIDTH, POOL_WIDTH).astype(BF16)
    n_route = N_GROUPS + N_EXPERTS
    w_r = jnp.concatenate([w_router_group[layer], w_router_expert[layer].reshape(d, N_EXPERTS),
                           jnp.zeros((d, ROUTE_LANES - n_route), F32)], axis=1).astype(BF16)
    b_r = jnp.concatenate([b_router_group[layer], b_router_expert[layer].reshape(N_EXPERTS),
                           jnp.zeros((ROUTE_LANES - n_route,), F32)]).reshape(1, ROUTE_LANES)

    def row(v):
        return v.reshape(1, -1)

    tm = MOE_TILE
    u_pool, qkv, mq = _proj(x2, row(norm_mix_g[layer]), w_proj, tm=1024)
    kv = _memkv(mem2, row(norm_mem_g[layer]), w_mem_kv[layer].astype(BF16), tm=512)
    diff = _diff_attn(qkv, row(lam_q1[layer]), row(lam_k1[layer]), row(lam_q2[layer]), row(lam_k2[layer]),
                      subln_g[layer].reshape(-1, 1), b, s, tq=256)
    x1, hn, route, cnt = _merge(
        x2, u_pool, mq, kv, diff, row(norm_mix_g[layer]), w_gates, w_bd, row(pool_scale[layer]),
        p_pool[layer].astype(BF16), p_diff[layer].astype(BF16), p_mem[layer].astype(BF16), w_o[layer].astype(BF16),
        row(norm_ffn_g[layer]), w_r, b_r, s, n_mem, tm=tm)

    cnt_tiles = cnt[::8, :N_EXPERTS].astype(jnp.int32)
    n_groups = (t // tm) * N_EXPERTS
    n_blocks = (2 * t + n_groups * (GROUP_ROWS - 1) + N_EXPERTS * (MOE_BLOCK - 1)) // MOE_BLOCK + 1
    piece_rows, n_pieces, fill, block_expert, n_used = _dispatch_plan(cnt_tiles, n_blocks, _tile_rows(tm) // GROUP_ROWS)
    rows = _dispatch(piece_rows, n_pieces, fill, hn, route, n_blocks, tm=tm)
    y_rows = _ffn(block_expert, n_used, rows, w_expert_gate[layer].astype(BF16), w_expert_up[layer].astype(BF16),
                  w_expert_down[layer].astype(BF16))
    out = _combine(piece_rows, n_pieces, x1, route, row(final_g), y_rows, tm=tm)
    return out.reshape(b, s, d)
```

```python
import functools

import jax
import jax.numpy as jnp
from jax import lax
from jax.experimental import pallas as pl
from jax.experimental.pallas import tpu as pltpu

F32 = jnp.float32
BF16 = jnp.bfloat16

RMS_EPS = 1e-6
POOL_WINDOWS = (2, 4, 8, 16)
POOL_GROUP = 64
POOL_WIDTH = 256
POOL_HALO = 16
DA_HEADS = 4
DA_HEAD_DIM = 64
DA_V_DIM = 128
SUM_ROWS = 16
DA_QK_WIDTH = 512
DA_WIDTH = 512
MEM_HEADS = 4
MEM_HEAD_DIM = 64
MEM_WIDTH = 256
N_GROUPS = 4
EXPERTS_PER_GROUP = 4
N_EXPERTS = 16
MOE_BLOCK = 1024
MOE_TILE = 256
COMBINE_COLS = 256
GROUP_ROWS = 16
LAMBDA_INIT = 0.2
QK_SCALE = 0.125
LOG2_E = 1.4426950408889634
ROUTE_LANES = 128

PROJ_COLS = POOL_WIDTH + 3 * DA_QK_WIDTH + MEM_WIDTH

VMEM_LIMIT = 56 * 1024 * 1024


def _cparams(n_axes):
    return pltpu.CompilerParams(dimension_semantics=("arbitrary",) * n_axes, vmem_limit_bytes=VMEM_LIMIT)


def _rms(x):
    return x * lax.rsqrt(jnp.mean(x * x, axis=-1, keepdims=True) + RMS_EPS)


def _proj_kernel(x_ref, g_ref, w_ref, pool_ref, qkv_ref, mq_ref):
    h = (_rms(x_ref[...]) * g_ref[...]).astype(BF16)
    p = jnp.dot(h, w_ref[...], preferred_element_type=F32)
    pool_ref[...] = p[:, :POOL_WIDTH]
    q_end = POOL_WIDTH + DA_QK_WIDTH
    kv_end = q_end + 2 * DA_QK_WIDTH
    qkv_ref[:, :DA_QK_WIDTH] = (p[:, POOL_WIDTH:q_end] * (QK_SCALE * LOG2_E)).astype(BF16)
    qkv_ref[:, DA_QK_WIDTH:] = p[:, q_end:kv_end].astype(BF16)
    mq_ref[...] = (p[:, kv_end:] * QK_SCALE).astype(BF16)


def _proj(x2, g, w, tm):
    t, d = x2.shape
    return pl.pallas_call(
        _proj_kernel,
        grid=(t // tm,),
        in_specs=[
            pl.BlockSpec((tm, d), lambda i: (i, 0)),
            pl.BlockSpec((1, d), lambda i: (0, 0)),
            pl.BlockSpec((d, PROJ_COLS), lambda i: (0, 0)),
        ],
        out_specs=[
            pl.BlockSpec((tm, POOL_WIDTH), lambda i: (i, 0)),
            pl.BlockSpec((tm, 3 * DA_QK_WIDTH), lambda i: (i, 0)),
            pl.BlockSpec((tm, MEM_WIDTH), lambda i: (i, 0)),
        ],
        out_shape=[
            jax.ShapeDtypeStruct((t, POOL_WIDTH), F32),
            jax.ShapeDtypeStruct((t, 3 * DA_QK_WIDTH), BF16),
            jax.ShapeDtypeStruct((t, MEM_WIDTH), BF16),
        ],
        compiler_params=_cparams(1),
        name="proj",
    )(x2, g, w)


def _memkv_kernel(m_ref, g_ref, w_ref, kv_ref):
    h = (_rms(m_ref[...]) * g_ref[...]).astype(BF16)
    kv_ref[...] = jnp.dot(h, w_ref[...], preferred_element_type=F32).astype(BF16)


def _memkv(mem2, g, w, tm):
    t, d = mem2.shape
    n = w.shape[1]
    return pl.pallas_call(
        _memkv_kernel,
        grid=(t // tm,),
        in_specs=[
            pl.BlockSpec((tm, d), lambda i: (i, 0)),
            pl.BlockSpec((1, d), lambda i: (0, 0)),
            pl.BlockSpec((d, n), lambda i: (0, 0)),
        ],
        out_specs=pl.BlockSpec((tm, n), lambda i: (i, 0)),
        out_shape=jax.ShapeDtypeStruct((t, n), BF16),
        compiler_params=_cparams(1),
        name="mem_kv",
    )(mem2, g, w)


def _diff_attn_kernel(q_ref, k_ref, v_ref, lq1_ref, lk1_ref, lq2_ref, lk2_ref, sg_ref, o_ref,
                      vt_ref, s_ref, m_ref, acc_ref, *, tq):
    i = pl.program_id(1)
    n_chunks = vt_ref.shape[1]
    hw = 2 * DA_HEAD_DIM

    @pl.when(i == 0)
    def _():
        for h in range(DA_HEADS):
            for c in range(n_chunks):
                vt_ref[h, c, :DA_V_DIM] = v_ref[c * tq:(c + 1) * tq, h * hw:(h + 1) * hw].astype(F32).T.astype(BF16)
                vt_ref[h, c, DA_V_DIM:] = jnp.ones((SUM_ROWS, tq), BF16)

    lane = lax.broadcasted_iota(jnp.int32, (tq, hw), 1)
    q2 = []
    for h in range(DA_HEADS):
        q = q_ref[:, h * hw:(h + 1) * hw]
        zero = jnp.zeros_like(q)
        q2.append(jnp.concatenate([jnp.where(lane < DA_HEAD_DIM, q, zero), jnp.where(lane >= DA_HEAD_DIM, q, zero)],
                                  axis=0))

    m_ref[...] = jnp.full(m_ref.shape, -jnp.inf, F32)
    acc_ref[...] = jnp.zeros(acc_ref.shape, F32)

    heads = range(DA_HEADS)

    def score_chunk(j, slot):
        start = pl.multiple_of(j * tq, tq)
        for h in heads:
            k = k_ref[pl.ds(start, tq), h * hw:(h + 1) * hw]
            s_ref[slot, h] = lax.dot_general(k, q2[h], (((1,), (1,)), ((), ())), preferred_element_type=F32)

    def softmax_chunk(slot, masked):
        probs, alphas = [], []
        for h in heads:
            s = s_ref[slot, h]
            if masked:
                key = lax.broadcasted_iota(jnp.int32, s.shape, 0)
                qry = lax.broadcasted_iota(jnp.int32, s.shape, 1)
                qry = jnp.where(qry >= tq, qry - tq, qry)
                s = jnp.where(key <= qry, s, -jnp.inf)
            m = m_ref[h]
            m_new = jnp.maximum(m, jnp.max(s, axis=0, keepdims=True))
            alpha = jnp.exp2(m - m_new)
            p = jnp.exp2(s - m_new)
            m_ref[h] = m_new
            alphas.append(alpha)
            probs.append(p.astype(BF16))
        return probs, alphas

    def value_chunk(j, probs, alphas):
        for h in heads:
            acc_ref[h] = alphas[h] * acc_ref[h] + jnp.dot(vt_ref[h, j], probs[h], preferred_element_type=F32)

    def step(j, slot, masked, prefetch):
        probs, alphas = softmax_chunk(slot, masked)
        if prefetch:
            score_chunk(j + 1, 1 - slot)
        value_chunk(j, probs, alphas)

    score_chunk(0, 0)

    def body(jj, carry):
        step(2 * jj, 0, False, True)
        step(2 * jj + 1, 1, False, True)
        return carry

    lax.fori_loop(0, i // 2, body, 0)

    @pl.when(i % 2 == 1)
    def _():
        step(i - 1, 0, False, True)
        step(i, 1, True, False)

    @pl.when(i % 2 == 0)
    def _():
        step(i, 0, True, False)

    lam = (jnp.exp(jnp.sum(lq1_ref[...] * lk1_ref[...], axis=-1, keepdims=True))
           - jnp.exp(jnp.sum(lq2_ref[...] * lk2_ref[...], axis=-1, keepdims=True)) + LAMBDA_INIT)
    for h in range(DA_HEADS):
        o = acc_ref[h, :DA_V_DIM] / acc_ref[h, DA_V_DIM:DA_V_DIM + 1]
        o = o[:, :tq] - lam * o[:, tq:]
        o = o * lax.rsqrt(jnp.mean(o * o, axis=0, keepdims=True) + RMS_EPS) * sg_ref[...] * (1.0 - LAMBDA_INIT)
        o_ref[:, h * DA_V_DIM:(h + 1) * DA_V_DIM] = o.T.astype(o_ref.dtype)


def _diff_attn(qkv, lq1, lk1, lq2, lk2, subln_g_col, batch, seq, tq):
    t = qkv.shape[0]
    nq = seq // tq
    lam_spec = pl.BlockSpec((1, DA_HEAD_DIM), lambda b, i: (0, 0))
    return pl.pallas_call(
        functools.partial(_diff_attn_kernel, tq=tq),
        grid=(batch, nq),
        in_specs=[
            pl.BlockSpec((tq, DA_QK_WIDTH), lambda b, i: (b * nq + i, 0)),
            pl.BlockSpec((seq, DA_QK_WIDTH), lambda b, i: (b, 1)),
            pl.BlockSpec((seq, DA_WIDTH), lambda b, i: (b, 2)),
            lam_spec, lam_spec, lam_spec, lam_spec,
            pl.BlockSpec((DA_V_DIM, 1), lambda b, i: (0, 0)),
        ],
        out_specs=pl.BlockSpec((tq, DA_WIDTH), lambda b, i: (b * nq + i, 0)),
        out_shape=jax.ShapeDtypeStruct((t, DA_WIDTH), BF16),
        scratch_shapes=[
            pltpu.VMEM((DA_HEADS, nq, DA_V_DIM + SUM_ROWS, tq), BF16),
            pltpu.VMEM((2, DA_HEADS, tq, 2 * tq), F32),
            pltpu.VMEM((DA_HEADS, 1, 2 * tq), F32),
            pltpu.VMEM((DA_HEADS, DA_V_DIM + SUM_ROWS, 2 * tq), F32),
        ],
        compiler_params=_cparams(2),
        name="diff_attn",
    )(qkv, qkv, qkv, lq1, lk1, lq2, lk2, subln_g_col)


def _pool_mixer(u, halo, seq_pos0):
    tm = u.shape[0]
    ext = jnp.concatenate([halo, u], axis=0)
    n = ext.shape[0]
    sums = []
    acc = ext
    for shift in (1, 2, 4, 8):
        acc = acc + pltpu.roll(acc, shift, 0)
        sums.append(acc[POOL_HALO:n])
    lane = lax.broadcasted_iota(jnp.int32, (tm, POOL_WIDTH), 1)
    pos = seq_pos0 + lax.broadcasted_iota(jnp.int32, (tm, POOL_WIDTH), 0)
    win_sum = sums[3]
    win = jnp.full((tm, POOL_WIDTH), POOL_WINDOWS[3], jnp.int32)
    for gi in (2, 1, 0):
        in_group = lane < (gi + 1) * POOL_GROUP
        win_sum = jnp.where(in_group, sums[gi], win_sum)
        win = jnp.where(in_group, POOL_WINDOWS[gi], win)
    cnt = jnp.minimum(pos + 1, win).astype(F32)
    return win_sum / cnt - u


def _mem_attention(mq, kv):
    tm = mq.shape[0]
    head = lax.broadcasted_iota(jnp.int32, mq.shape, 1) // MEM_HEAD_DIM
    zero = jnp.zeros_like(mq)
    q4 = jnp.concatenate([jnp.where(head == h, mq, zero) for h in range(MEM_HEADS)], axis=0)
    mk = kv[:, :MEM_WIDTH]
    mv = kv[:, MEM_WIDTH:]
    s = lax.dot_general(q4, mk, (((1,), (1,)), ((), ())), preferred_element_type=F32)
    p = jnp.exp(s - jnp.max(s, axis=-1, keepdims=True))
    l = jnp.sum(p, axis=-1, keepdims=True)
    o4 = jnp.dot(p.astype(BF16), mv, preferred_element_type=F32) / l
    out = jnp.zeros((tm, MEM_WIDTH), F32)
    for h in range(MEM_HEADS):
        out = jnp.where(head == h, o4[h * tm:(h + 1) * tm], out)
    return out


def _router_select(logits):
    lane = lax.broadcasted_iota(jnp.int32, logits.shape, 1).astype(F32)
    neg = -jnp.inf

    def first_argmax(vals, vmax):
        return jnp.min(jnp.where(vals == vmax, lane, float(ROUTE_LANES)), axis=-1, keepdims=True)

    lg = jnp.where(lane < N_GROUPS, logits, neg)
    mg = jnp.max(lg, axis=-1, keepdims=True)
    g_p = 1.0 / jnp.sum(jnp.exp(lg - mg), axis=-1, keepdims=True)
    g_idx = first_argmax(lg, mg)
    lo = N_GROUPS + EXPERTS_PER_GROUP * g_idx
    le = jnp.where((lane >= lo) & (lane < lo + EXPERTS_PER_GROUP), logits, neg)
    m1 = jnp.max(le, axis=-1, keepdims=True)
    i1 = first_argmax(le, m1)
    le2 = jnp.where(lane == i1, neg, le)
    m2 = jnp.max(le2, axis=-1, keepdims=True)
    i2 = first_argmax(le2, m2)
    r = jnp.exp(m2 - m1)
    w1 = g_p / (1.0 + r)
    w2 = w1 * r
    return i1 - N_GROUPS, i2 - N_GROUPS, w1, w2


def _router_place(e1, e2, w1, w2):
    tm = e1.shape[0]
    lane = lax.broadcasted_iota(jnp.int32, (tm, ROUTE_LANES), 1).astype(F32)
    oh1 = lane == e1
    oh2 = lane == e2
    oh = jnp.where(oh1 | oh2, 1.0, 0.0)
    earlier = (lax.broadcasted_iota(jnp.int32, (tm, tm), 0) > lax.broadcasted_iota(jnp.int32, (tm, tm), 1))
    rank = jnp.dot(earlier.astype(BF16), oh.astype(BF16), preferred_element_type=F32)
    cnt = jnp.sum(oh, axis=0, keepdims=True)
    group = jnp.floor((cnt + (GROUP_ROWS - 1)) * (1.0 / GROUP_ROWS)) * GROUP_ROWS
    lower = (lax.broadcasted_iota(jnp.int32, (ROUTE_LANES, ROUTE_LANES), 0)
             < lax.broadcasted_iota(jnp.int32, (ROUTE_LANES, ROUTE_LANES), 1)).astype(BF16)
    offs = jnp.dot(jnp.broadcast_to(group, (8, ROUTE_LANES)).astype(BF16), lower,
                   preferred_element_type=F32)
    pos = offs[0:1] + rank

    def pick(onehot):
        return jnp.sum(jnp.where(onehot, pos, 0.0), axis=-1, keepdims=True)

    cols = (e1, e2, w1, w2, pick(oh1), pick(oh2))
    out = jnp.zeros((tm, ROUTE_LANES), F32)
    for n, col in enumerate(cols):
        out = jnp.where(lane == n, col, out)
    return out, cnt


def _merge_kernel(x_ref, u_ref, halo_ref, mq_ref, kv_ref, diff_ref,
                  gmix_ref, wg_ref, wbd_ref, pscale_ref, pp_ref, pd_ref, pm_ref, wo_ref,
                  gffn_ref, wr_ref, br_ref,
                  x1_ref, hn_ref, route_ref, cnt_ref, x1_prev_ref, *, tm, seq):
    i = pl.program_id(0)
    d = x_ref.shape[1]

    @pl.when(i == 0)
    def _():
        x1_prev_ref[...] = jnp.zeros(x1_prev_ref.shape, F32)

    hn = (_rms(x1_prev_ref[...]) * gffn_ref[...]).astype(BF16)
    hn_ref[...] = hn
    logits = jnp.dot(hn, wr_ref[...], preferred_element_type=F32) + br_ref[...]
    selection = _router_select(logits)

    x = x_ref[...]
    h = (_rms(x) * gmix_ref[...]).astype(BF16)

    seq_pos0 = (jnp.minimum(i, pl.num_programs(0) - 2) * tm) % seq
    halo = jnp.where(seq_pos0 == 0, 0.0, halo_ref[...])
    pooled = _pool_mixer(u_ref[...], halo, seq_pos0).astype(BF16)
    pool_out = jnp.dot(pooled, wbd_ref[...], preferred_element_type=F32) * pscale_ref[...]
    mem_out = _mem_attention(mq_ref[...], kv_ref[...])

    branches = ((pool_out.astype(BF16), pp_ref), (diff_ref[...], pd_ref), (mem_out.astype(BF16), pm_ref))
    merged = jnp.zeros((tm, d), F32)
    for n, (b_in, p_ref) in enumerate(branches):
        gate = jax.nn.sigmoid(jnp.dot(h, wg_ref[:, n * d:(n + 1) * d], preferred_element_type=F32))
        merged = merged + gate * jnp.dot(b_in, p_ref[...], preferred_element_type=F32)
    route, cnt = _router_place(*selection)
    route_ref[...] = route
    cnt_ref[...] = jnp.broadcast_to(cnt, cnt_ref.shape)

    x1 = x + jnp.dot(merged.astype(BF16), wo_ref[...], preferred_element_type=F32)
    x1_ref[...] = x1
    x1_prev_ref[...] = x1


def _merge(x2, u_pool, mq, kv, diff, gmix, wg, wbd, pscale, pp, pd, pm, wo, gffn, wr, br, seq, n_mem, tm):
    t, d = x2.shape
    nt = t // tm
    tiles_per_seq = seq // tm
    halo_blocks = tm // POOL_HALO

    def tile(i):
        return (jnp.minimum(i, nt - 1), 0)

    def routed(i):
        return (jnp.maximum(i - 1, 0), 0)

    def const(i):
        return (0, 0)

    def wspec(a):
        return pl.BlockSpec(a.shape, const)

    return pl.pallas_call(
        functools.partial(_merge_kernel, tm=tm, seq=seq),
        grid=(nt + 1,),
        in_specs=[
            pl.BlockSpec((tm, d), tile),
            pl.BlockSpec((tm, POOL_WIDTH), tile),
            pl.BlockSpec((POOL_HALO, POOL_WIDTH), lambda i: (jnp.maximum(jnp.minimum(i, nt - 1) * halo_blocks - 1, 0), 0)),
            pl.BlockSpec((tm, MEM_WIDTH), tile),
            pl.BlockSpec((n_mem, 2 * MEM_WIDTH), lambda i: (jnp.minimum(i, nt - 1) // tiles_per_seq, 0)),
            pl.BlockSpec((tm, DA_WIDTH), tile),
            wspec(gmix), wspec(wg), wspec(wbd), wspec(pscale), wspec(pp), wspec(pd), wspec(pm), wspec(wo),
            wspec(gffn), wspec(wr), wspec(br),
        ],
        out_specs=[
            pl.BlockSpec((tm, d), tile),
            pl.BlockSpec((tm, d), routed),
            pl.BlockSpec((tm, ROUTE_LANES), routed),
            pl.BlockSpec((8, ROUTE_LANES), routed),
        ],
        out_shape=[
            jax.ShapeDtypeStruct((t, d), F32),
            jax.ShapeDtypeStruct((t, d), BF16),
            jax.ShapeDtypeStruct((t, ROUTE_LANES), F32),
            jax.ShapeDtypeStruct((nt * 8, ROUTE_LANES), F32),
        ],
        scratch_shapes=[pltpu.VMEM((tm, d), F32)],
        compiler_params=_cparams(1),
        name="merge",
    )(x2, u_pool, u_pool, mq, kv, diff, gmix, wg, wbd, pscale, pp, pd, pm, wo, gffn, wr, br)


def _dispatch_kernel(piece_ref, fill_ref, hn_ref, route_ref, rows_ref, rbuf, zbuf, sem, zsem, *, tm, n_blocks):
    i = pl.program_id(0)
    last = pl.num_programs(0) - 1
    k = rbuf.shape[1]
    pieces = k // GROUP_ROWS
    slot = i % 2
    prev = jnp.maximum(i - 1, 0)

    def wait_all():
        pltpu.make_async_copy(rbuf.at[0], rows_ref.at[pl.ds(0, k)], sem).wait()

    @pl.when(i == 0)
    def _():
        rbuf[1] = jnp.zeros(rbuf.shape[1:], rbuf.dtype)

    @pl.when(i > 0)
    def _():
        wait_all()

    route = route_ref[...]
    kio = lax.broadcasted_iota(jnp.int32, (tm, k), 1).astype(F32)
    onehot = jnp.where((kio == route[:, 4:5]) | (kio == route[:, 5:6]), 1.0, 0.0).astype(BF16)
    for f in range(pieces):
        pltpu.make_async_copy(
            rbuf.at[1 - slot, pl.ds(f * GROUP_ROWS, GROUP_ROWS)],
            rows_ref.at[pl.ds(pl.multiple_of(piece_ref[prev * pieces + f], GROUP_ROWS), GROUP_ROWS)], sem).start()
    rbuf[slot] = lax.dot_general(onehot, hn_ref[...], (((0,), (0,)), ((), ())),
                                 preferred_element_type=F32).astype(BF16)

    @pl.when(i == last)
    def _():
        wait_all()
        zbuf[...] = jnp.zeros(zbuf.shape, zbuf.dtype)

        def tail_copy(dst_row):
            return pltpu.make_async_copy(zbuf.at[pl.ds(0, GROUP_ROWS)], rows_ref.at[pl.ds(dst_row, GROUP_ROWS)], zsem.at[0])

        def block_copy(blk):
            return pltpu.make_async_copy(zbuf, rows_ref.at[pl.ds(pl.multiple_of(blk * MOE_BLOCK, MOE_BLOCK), MOE_BLOCK)],
                                         zsem.at[1])

        n_used = fill_ref[2 * N_EXPERTS]
        for e in range(N_EXPERTS):
            start = fill_ref[e]

            def body(c, carry, start=start):
                tail_copy(pl.multiple_of(start + c * GROUP_ROWS, GROUP_ROWS)).start()
                return carry

            lax.fori_loop(0, fill_ref[N_EXPERTS + e], body, 0)

        def blk_body(b, carry):
            block_copy(b).start()
            return carry

        lax.fori_loop(n_used, n_blocks, blk_body, 0)

        for e in range(N_EXPERTS):
            def wbody(c, carry):
                tail_copy(0).wait()
                return carry

            lax.fori_loop(0, fill_ref[N_EXPERTS + e], wbody, 0)

        def blk_wait(b, carry):
            block_copy(0).wait()
            return carry

        lax.fori_loop(n_used, n_blocks, blk_wait, 0)

        dump_copy = pltpu.make_async_copy(zbuf.at[pl.ds(0, k)], rows_ref.at[pl.ds(n_blocks * MOE_BLOCK, k)], zsem.at[2])
        dump_copy.start()
        dump_copy.wait()


def _tile_rows(tm):
    return 2 * tm + N_EXPERTS * GROUP_ROWS


def _dispatch(piece_rows, fill, hn, route, n_blocks, tm):
    t, d = hn.shape
    nt = t // tm
    k = _tile_rows(tm)
    assert MOE_BLOCK >= k

    def tile(i, p, f):
        return (jnp.minimum(i, nt - 1), 0)

    grid_spec = pltpu.PrefetchScalarGridSpec(
        num_scalar_prefetch=2,
        grid=(nt + 1,),
        in_specs=[pl.BlockSpec((tm, d), tile), pl.BlockSpec((tm, ROUTE_LANES), tile)],
        out_specs=pl.BlockSpec(memory_space=pl.ANY),
        scratch_shapes=[pltpu.VMEM((2, k, d), BF16), pltpu.VMEM((MOE_BLOCK, d), BF16),
                        pltpu.SemaphoreType.DMA(()), pltpu.SemaphoreType.DMA((3,))],
    )
    return pl.pallas_call(
        functools.partial(_dispatch_kernel, tm=tm, n_blocks=n_blocks),
        grid_spec=grid_spec,
        out_shape=jax.ShapeDtypeStruct((n_blocks * MOE_BLOCK + k, d), BF16),
        compiler_params=_cparams(1),
        name="dispatch",
    )(piece_rows, fill, hn, route)


def _ffn_kernel(be_ref, nused_ref, rows_ref, wg_ref, wu_ref, wd_ref, y_ref):
    in_use = pl.program_id(0) < nused_ref[0]

    @pl.when(in_use)
    def _():
        xb = rows_ref[...]
        hg = jnp.dot(xb, wg_ref[...], preferred_element_type=F32)
        hu = jnp.dot(xb, wu_ref[...], preferred_element_type=F32)
        hid = (hg * jax.nn.sigmoid(hg) * hu).astype(BF16)
        y_ref[...] = jnp.dot(hid, wd_ref[...], preferred_element_type=F32).astype(y_ref.dtype)

    @pl.when(jnp.logical_not(in_use))
    def _():
        y_ref[...] = jnp.zeros(y_ref.shape, y_ref.dtype)


def _ffn(block_expert, n_used, rows, wg, wu, wd):
    n_blocks = block_expert.shape[0]
    d = rows.shape[1]
    hidden = wg.shape[2]

    def blk(i, be, nu):
        return (jnp.minimum(i, nu[0] - 1), 0)

    def wblk(i, be, nu):
        return (be[jnp.minimum(i, nu[0] - 1)], 0, 0)

    grid_spec = pltpu.PrefetchScalarGridSpec(
        num_scalar_prefetch=2,
        grid=(n_blocks,),
        in_specs=[
            pl.BlockSpec((MOE_BLOCK, d), blk),
            pl.BlockSpec((None, d, hidden), wblk),
            pl.BlockSpec((None, d, hidden), wblk),
            pl.BlockSpec((None, hidden, d), wblk),
        ],
        out_specs=pl.BlockSpec((MOE_BLOCK, d), lambda i, be, nu: (i, 0)),
    )
    return pl.pallas_call(
        _ffn_kernel,
        grid_spec=grid_spec,
        out_shape=jax.ShapeDtypeStruct((n_blocks * MOE_BLOCK, d), BF16),
        compiler_params=_cparams(1),
        name="ffn",
    )(block_expert, n_used, rows, wg, wu, wd)


def _combine_kernel(piece_ref, x1_ref, route_ref, g_ref, y_ref, o_ref, ybuf, sem, *, tm):
    i = pl.program_id(0)
    nt = pl.num_programs(0)
    slot = i % 2
    kbuf = ybuf.shape[1]
    pieces = kbuf // GROUP_ROWS

    def fetch(step, dst_slot, first, last):
        for f in range(first, last):
            pltpu.make_async_copy(y_ref.at[pl.ds(pl.multiple_of(piece_ref[step * pieces + f], GROUP_ROWS), GROUP_ROWS)],
                                  ybuf.at[dst_slot, pl.ds(f * GROUP_ROWS, GROUP_ROWS)], sem.at[dst_slot]).start()

    def wait_all(the_slot):
        pltpu.make_async_copy(y_ref.at[pl.ds(0, kbuf)], ybuf.at[the_slot], sem.at[the_slot]).wait()

    @pl.when(i == 0)
    def _():
        fetch(0, 0, 0, pieces)

    wait_all(slot)

    route = route_ref[...]
    kio = lax.broadcasted_iota(jnp.int32, (tm, kbuf), 1).astype(F32)
    pw = (jnp.where(kio == route[:, 4:5], route[:, 2:3], 0.0)
          + jnp.where(kio == route[:, 5:6], route[:, 3:4], 0.0)).astype(BF16)
    nxt = jnp.minimum(i + 1, nt - 1)
    d = o_ref.shape[1]
    n_col = d // COMBINE_COLS
    per_chunk = -(-pieces // n_col)
    res, sq = [], jnp.zeros((tm, 1), F32)
    for c in range(n_col):
        cols = slice(c * COMBINE_COLS, (c + 1) * COMBINE_COLS)
        r = x1_ref[:, cols] + jnp.dot(pw, ybuf[slot, :, cols], preferred_element_type=F32)
        sq = sq + jnp.sum(r * r, axis=-1, keepdims=True)
        res.append(r)
        fetch(nxt, 1 - slot, c * per_chunk, min((c + 1) * per_chunk, pieces))
    scale = lax.rsqrt(sq * (1.0 / d) + RMS_EPS)
    for c in range(n_col):
        cols = slice(c * COMBINE_COLS, (c + 1) * COMBINE_COLS)
        o_ref[:, cols] = res[c] * scale * g_ref[:, cols]

    @pl.when(i == nt - 1)
    def _():
        wait_all(1 - slot)


def _combine(piece_rows, x1, route, g, y_rows, tm):
    t, d = x1.shape
    kbuf = _tile_rows(tm)
    grid_spec = pltpu.PrefetchScalarGridSpec(
        num_scalar_prefetch=1,
        grid=(t // tm,),
        in_specs=[
            pl.BlockSpec((tm, d), lambda i, p: (i, 0)),
            pl.BlockSpec((tm, ROUTE_LANES), lambda i, p: (i, 0)),
            pl.BlockSpec((1, d), lambda i, p: (0, 0)),
            pl.BlockSpec(memory_space=pl.ANY),
        ],
        out_specs=pl.BlockSpec((tm, d), lambda i, p: (i, 0)),
        scratch_shapes=[pltpu.VMEM((2, kbuf, d), BF16), pltpu.SemaphoreType.DMA((2,))],
    )
    return pl.pallas_call(
        functools.partial(_combine_kernel, tm=tm),
        grid_spec=grid_spec,
        out_shape=jax.ShapeDtypeStruct((t, d), F32),
        compiler_params=_cparams(1),
        name="combine",
    )(piece_rows, x1, route, g, y_rows)


def _dispatch_plan(cnt_tiles, n_blocks, pieces):
    n_pc = (cnt_tiles + GROUP_ROWS - 1) // GROUP_ROWS
    group = n_pc * GROUP_ROWS
    total = jnp.sum(group, axis=0)
    cap = ((total + MOE_BLOCK - 1) // MOE_BLOCK) * MOE_BLOCK
    pend = jnp.cumsum(cap)
    pstart = pend - cap
    tile_base = pstart[None, :] + jnp.cumsum(group, axis=0) - group
    block_start = jnp.arange(n_blocks, dtype=jnp.int32) * MOE_BLOCK
    block_expert = jnp.minimum(jnp.sum(pend[None, :] <= block_start[:, None], axis=1), N_EXPERTS - 1)
    n_used = jnp.maximum(pend[-1] // MOE_BLOCK, 1).reshape(1)
    fill = jnp.concatenate([pstart + total, (cap - total) // GROUP_ROWS, n_used])

    pc_end = jnp.cumsum(n_pc, axis=1)
    pc_start = pc_end - n_pc
    f = jnp.arange(pieces, dtype=jnp.int32)
    owner = jnp.sum(pc_end[:, :, None] <= f[None, None, :], axis=1)
    is_owner = owner[:, None, :] == jnp.arange(N_EXPERTS, dtype=jnp.int32)[None, :, None]
    piece_rows = jnp.sum(jnp.where(is_owner, (tile_base - pc_start * GROUP_ROWS)[:, :, None] + f * GROUP_ROWS, 0), axis=1)
    dump_rows = n_blocks * MOE_BLOCK + f * GROUP_ROWS
    to_rows = jnp.where(owner < N_EXPERTS, piece_rows, dump_rows[None, :])
    return (to_rows.reshape(-1).astype(jnp.int32), piece_rows.reshape(-1).astype(jnp.int32), fill.astype(jnp.int32),
            block_expert.astype(jnp.int32), n_used.astype(jnp.int32))


def kernel(x, mem, norm_mix_g, w_in, pool_w, pool_scale, lam_q1, lam_k1, lam_q2, lam_k2, subln_g, norm_mem_g,
           w_mem_kv, p_pool, p_diff, p_mem, w_o, norm_ffn_g, w_router_group, b_router_group, w_router_expert,
           b_router_expert, w_expert_gate, w_expert_up, w_expert_down, final_g):
    b, s, d = x.shape
    n_mem = mem.shape[1]
    t = b * s
    layer = 0
    x2 = x.reshape(t, d)
    mem2 = mem.reshape(b * n_mem, d)

    w_in_l = w_in[layer]
    w_proj = w_in_l[:, :PROJ_COLS].astype(BF16)
    w_gates = w_in_l[:, PROJ_COLS:].astype(BF16)
    n_win = len(POOL_WINDOWS)
    eye = jnp.eye(n_win, dtype=F32)
    w_bd = (pool_w[layer][:, :, None, :] * eye[:, None, :, None]).reshape(POOL_WIDTH, POOL_WIDTH).astype(BF16)
    n_route = N_GROUPS + N_EXPERTS
    w_r = jnp.concatenate([w_router_group[layer], w_router_expert[layer].reshape(d, N_EXPERTS),
                           jnp.zeros((d, ROUTE_LANES - n_route), F32)], axis=1).astype(BF16)
    b_r = jnp.concatenate([b_router_group[layer], b_router_expert[layer].reshape(N_EXPERTS),
                           jnp.zeros((ROUTE_LANES - n_route,), F32)]).reshape(1, ROUTE_LANES)

    def row(v):
        return v.reshape(1, -1)

    tm = MOE_TILE
    u_pool, qkv, mq = _proj(x2, row(norm_mix_g[layer]), w_proj, tm=1024)
    kv = _memkv(mem2, row(norm_mem_g[layer]), w_mem_kv[layer].astype(BF16), tm=512)
    diff = _diff_attn(qkv, row(lam_q1[layer]), row(lam_k1[layer]), row(lam_q2[layer]), row(lam_k2[layer]),
                      subln_g[layer].reshape(-1, 1), b, s, tq=256)
    x1, hn, route, cnt = _merge(
        x2, u_pool, mq, kv, diff, row(norm_mix_g[layer]), w_gates, w_bd, row(pool_scale[layer]),
        p_pool[layer].astype(BF16), p_diff[layer].astype(BF16), p_mem[layer].astype(BF16), w_o[layer].astype(BF16),
        row(norm_ffn_g[layer]), w_r, b_r, s, n_mem, tm=tm)

    cnt_tiles = cnt[::8, :N_EXPERTS].astype(jnp.int32)
    n_groups = (t // tm) * N_EXPERTS
    n_blocks = (2 * t + n_groups * (GROUP_ROWS - 1) + N_EXPERTS * (MOE_BLOCK - 1)) // MOE_BLOCK + 1
    to_rows, from_rows, fill, block_expert, n_used = _dispatch_plan(cnt_tiles, n_blocks, _tile_rows(tm) // GROUP_ROWS)
    rows = _dispatch(to_rows, fill, hn, route, n_blocks, tm=tm)
    y_rows = _ffn(block_expert, n_used, rows, w_expert_gate[layer].astype(BF16), w_expert_up[layer].astype(BF16),
                  w_expert_down[layer].astype(BF16))
    out = _combine(from_rows, x1, route, row(final_g), y_rows, tm=tm)
    return out.reshape(b, s, d)
```

```python
import functools

import jax
import jax.numpy as jnp
from jax import lax
from jax.experimental import pallas as pl
from jax.experimental.pallas import tpu as pltpu

F32 = jnp.float32
BF16 = jnp.bfloat16

RMS_EPS = 1e-6
POOL_WINDOWS = (2, 4, 8, 16)
POOL_GROUP = 64
POOL_WIDTH = 256
POOL_HALO = 16
DA_HEADS = 4
DA_HEAD_DIM = 64
DA_V_DIM = 128
SUM_ROWS = 16
DA_QK_WIDTH = 512
DA_WIDTH = 512
MEM_HEADS = 4
MEM_HEAD_DIM = 64
MEM_WIDTH = 256
N_GROUPS = 4
EXPERTS_PER_GROUP = 4
N_EXPERTS = 16
MOE_BLOCK = 1024
MOE_TILE = 256
GROUP_ROWS = 16
LAMBDA_INIT = 0.2
QK_SCALE = 0.125
LOG2_E = 1.4426950408889634
ROUTE_LANES = 128

PROJ_COLS = POOL_WIDTH + 3 * DA_QK_WIDTH + MEM_WIDTH

VMEM_LIMIT = 56 * 1024 * 1024


def _cparams(n_axes):
    return pltpu.CompilerParams(dimension_semantics=("arbitrary",) * n_axes, vmem_limit_bytes=VMEM_LIMIT)


def _rms(x):
    return x * lax.rsqrt(jnp.mean(x * x, axis=-1, keepdims=True) + RMS_EPS)


def _proj_kernel(x_ref, g_ref, w_ref, pool_ref, qkv_ref, mq_ref):
    h = (_rms(x_ref[...]) * g_ref[...]).astype(BF16)
    p = jnp.dot(h, w_ref[...], preferred_element_type=F32)
    pool_ref[...] = p[:, :POOL_WIDTH]
    q_end = POOL_WIDTH + DA_QK_WIDTH
    kv_end = q_end + 2 * DA_QK_WIDTH
    qkv_ref[:, :DA_QK_WIDTH] = (p[:, POOL_WIDTH:q_end] * (QK_SCALE * LOG2_E)).astype(BF16)
    qkv_ref[:, DA_QK_WIDTH:] = p[:, q_end:kv_end].astype(BF16)
    mq_ref[...] = (p[:, kv_end:] * QK_SCALE).astype(BF16)


def _proj(x2, g, w, tm):
    t, d = x2.shape
    return pl.pallas_call(
        _proj_kernel,
        grid=(t // tm,),
        in_specs=[
            pl.BlockSpec((tm, d), lambda i: (i, 0)),
            pl.BlockSpec((1, d), lambda i: (0, 0)),
            pl.BlockSpec((d, PROJ_COLS), lambda i: (0, 0)),
        ],
        out_specs=[
            pl.BlockSpec((tm, POOL_WIDTH), lambda i: (i, 0)),
            pl.BlockSpec((tm, 3 * DA_QK_WIDTH), lambda i: (i, 0)),
            pl.BlockSpec((tm, MEM_WIDTH), lambda i: (i, 0)),
        ],
        out_shape=[
            jax.ShapeDtypeStruct((t, POOL_WIDTH), F32),
            jax.ShapeDtypeStruct((t, 3 * DA_QK_WIDTH), BF16),
            jax.ShapeDtypeStruct((t, MEM_WIDTH), BF16),
        ],
        compiler_params=_cparams(1),
        name="proj",
    )(x2, g, w)


def _memkv_kernel(m_ref, g_ref, w_ref, kv_ref):
    h = (_rms(m_ref[...]) * g_ref[...]).astype(BF16)
    kv_ref[...] = jnp.dot(h, w_ref[...], preferred_element_type=F32).astype(BF16)


def _memkv(mem2, g, w, tm):
    t, d = mem2.shape
    n = w.shape[1]
    return pl.pallas_call(
        _memkv_kernel,
        grid=(t // tm,),
        in_specs=[
            pl.BlockSpec((tm, d), lambda i: (i, 0)),
            pl.BlockSpec((1, d), lambda i: (0, 0)),
            pl.BlockSpec((d, n), lambda i: (0, 0)),
        ],
        out_specs=pl.BlockSpec((tm, n), lambda i: (i, 0)),
        out_shape=jax.ShapeDtypeStruct((t, n), BF16),
        compiler_params=_cparams(1),
        name="mem_kv",
    )(mem2, g, w)


def _diff_attn_kernel(q_ref, k_ref, v_ref, lq1_ref, lk1_ref, lq2_ref, lk2_ref, sg_ref, o_ref,
                      vt_ref, s_ref, m_ref, acc_ref, *, tq):
    i = pl.program_id(1)
    n_chunks = vt_ref.shape[1]
    hw = 2 * DA_HEAD_DIM

    @pl.when(i == 0)
    def _():
        for h in range(DA_HEADS):
            for c in range(n_chunks):
                vt_ref[h, c, :DA_V_DIM] = v_ref[c * tq:(c + 1) * tq, h * hw:(h + 1) * hw].astype(F32).T.astype(BF16)
                vt_ref[h, c, DA_V_DIM:] = jnp.ones((SUM_ROWS, tq), BF16)

    lane = lax.broadcasted_iota(jnp.int32, (tq, hw), 1)
    q2 = []
    for h in range(DA_HEADS):
        q = q_ref[:, h * hw:(h + 1) * hw]
        zero = jnp.zeros_like(q)
        q2.append(jnp.concatenate([jnp.where(lane < DA_HEAD_DIM, q, zero), jnp.where(lane >= DA_HEAD_DIM, q, zero)],
                                  axis=0))

    m_ref[...] = jnp.full(m_ref.shape, -jnp.inf, F32)
    acc_ref[...] = jnp.zeros(acc_ref.shape, F32)

    heads = range(DA_HEADS)

    def score_chunk(j, slot):
        start = pl.multiple_of(j * tq, tq)
        for h in heads:
            k = k_ref[pl.ds(start, tq), h * hw:(h + 1) * hw]
            s_ref[slot, h] = lax.dot_general(k, q2[h], (((1,), (1,)), ((), ())), preferred_element_type=F32)

    def softmax_chunk(slot, masked):
        probs, alphas = [], []
        for h in heads:
            s = s_ref[slot, h]
            if masked:
                key = lax.broadcasted_iota(jnp.int32, s.shape, 0)
                qry = lax.broadcasted_iota(jnp.int32, s.shape, 1)
                qry = jnp.where(qry >= tq, qry - tq, qry)
                s = jnp.where(key <= qry, s, -jnp.inf)
            m = m_ref[h]
            m_new = jnp.maximum(m, jnp.max(s, axis=0, keepdims=True))
            alpha = jnp.exp2(m - m_new)
            p = jnp.exp2(s - m_new)
            m_ref[h] = m_new
            alphas.append(alpha)
            probs.append(p.astype(BF16))
        return probs, alphas

    def value_chunk(j, probs, alphas):
        for h in heads:
            acc_ref[h] = alphas[h] * acc_ref[h] + jnp.dot(vt_ref[h, j], probs[h], preferred_element_type=F32)

    def step(j, slot, masked, prefetch):
        probs, alphas = softmax_chunk(slot, masked)
        if prefetch:
            score_chunk(j + 1, 1 - slot)
        value_chunk(j, probs, alphas)

    score_chunk(0, 0)

    def body(jj, carry):
        step(2 * jj, 0, False, True)
        step(2 * jj + 1, 1, False, True)
        return carry

    lax.fori_loop(0, i // 2, body, 0)

    @pl.when(i % 2 == 1)
    def _():
        step(i - 1, 0, False, True)
        step(i, 1, True, False)

    @pl.when(i % 2 == 0)
    def _():
        step(i, 0, True, False)

    lam = (jnp.exp(jnp.sum(lq1_ref[...] * lk1_ref[...], axis=-1, keepdims=True))
           - jnp.exp(jnp.sum(lq2_ref[...] * lk2_ref[...], axis=-1, keepdims=True)) + LAMBDA_INIT)
    for h in range(DA_HEADS):
        o = acc_ref[h, :DA_V_DIM] / acc_ref[h, DA_V_DIM:DA_V_DIM + 1]
        o = o[:, :tq] - lam * o[:, tq:]
        o = o * lax.rsqrt(jnp.mean(o * o, axis=0, keepdims=True) + RMS_EPS) * sg_ref[...] * (1.0 - LAMBDA_INIT)
        o_ref[:, h * DA_V_DIM:(h + 1) * DA_V_DIM] = o.T.astype(o_ref.dtype)


def _diff_attn(qkv, lq1, lk1, lq2, lk2, subln_g_col, batch, seq, tq):
    t = qkv.shape[0]
    nq = seq // tq
    lam_spec = pl.BlockSpec((1, DA_HEAD_DIM), lambda b, i: (0, 0))
    return pl.pallas_call(
        functools.partial(_diff_attn_kernel, tq=tq),
        grid=(batch, nq),
        in_specs=[
            pl.BlockSpec((tq, DA_QK_WIDTH), lambda b, i: (b * nq + i, 0)),
            pl.BlockSpec((seq, DA_QK_WIDTH), lambda b, i: (b, 1)),
            pl.BlockSpec((seq, DA_WIDTH), lambda b, i: (b, 2)),
            lam_spec, lam_spec, lam_spec, lam_spec,
            pl.BlockSpec((DA_V_DIM, 1), lambda b, i: (0, 0)),
        ],
        out_specs=pl.BlockSpec((tq, DA_WIDTH), lambda b, i: (b * nq + i, 0)),
        out_shape=jax.ShapeDtypeStruct((t, DA_WIDTH), BF16),
        scratch_shapes=[
            pltpu.VMEM((DA_HEADS, nq, DA_V_DIM + SUM_ROWS, tq), BF16),
            pltpu.VMEM((2, DA_HEADS, tq, 2 * tq), F32),
            pltpu.VMEM((DA_HEADS, 1, 2 * tq), F32),
            pltpu.VMEM((DA_HEADS, DA_V_DIM + SUM_ROWS, 2 * tq), F32),
        ],
        compiler_params=_cparams(2),
        name="diff_attn",
    )(qkv, qkv, qkv, lq1, lk1, lq2, lk2, subln_g_col)


def _pool_mixer(u, halo, seq_pos0):
    tm = u.shape[0]
    ext = jnp.concatenate([halo, u], axis=0)
    n = ext.shape[0]
    sums = []
    acc = ext
    for shift in (1, 2, 4, 8):
        acc = acc + pltpu.roll(acc, shift, 0)
        sums.append(acc[POOL_HALO:n])
    lane = lax.broadcasted_iota(jnp.int32, (tm, POOL_WIDTH), 1)
    pos = seq_pos0 + lax.broadcasted_iota(jnp.int32, (tm, POOL_WIDTH), 0)
    win_sum = sums[3]
    win = jnp.full((tm, POOL_WIDTH), POOL_WINDOWS[3], jnp.int32)
    for gi in (2, 1, 0):
        in_group = lane < (gi + 1) * POOL_GROUP
        win_sum = jnp.where(in_group, sums[gi], win_sum)
        win = jnp.where(in_group, POOL_WINDOWS[gi], win)
    cnt = jnp.minimum(pos + 1, win).astype(F32)
    return win_sum / cnt - u


def _mem_attention(mq, kv):
    tm = mq.shape[0]
    head = lax.broadcasted_iota(jnp.int32, mq.shape, 1) // MEM_HEAD_DIM
    zero = jnp.zeros_like(mq)
    q4 = jnp.concatenate([jnp.where(head == h, mq, zero) for h in range(MEM_HEADS)], axis=0)
    mk = kv[:, :MEM_WIDTH]
    mv = kv[:, MEM_WIDTH:]
    s = lax.dot_general(q4, mk, (((1,), (1,)), ((), ())), preferred_element_type=F32)
    p = jnp.exp(s - jnp.max(s, axis=-1, keepdims=True))
    l = jnp.sum(p, axis=-1, keepdims=True)
    o4 = jnp.dot(p.astype(BF16), mv, preferred_element_type=F32) / l
    out = jnp.zeros((tm, MEM_WIDTH), F32)
    for h in range(MEM_HEADS):
        out = jnp.where(head == h, o4[h * tm:(h + 1) * tm], out)
    return out


def _router_select(logits):
    lane = lax.broadcasted_iota(jnp.int32, logits.shape, 1).astype(F32)
    neg = -jnp.inf

    def first_argmax(vals, vmax):
        return jnp.min(jnp.where(vals == vmax, lane, float(ROUTE_LANES)), axis=-1, keepdims=True)

    lg = jnp.where(lane < N_GROUPS, logits, neg)
    mg = jnp.max(lg, axis=-1, keepdims=True)
    g_p = 1.0 / jnp.sum(jnp.exp(lg - mg), axis=-1, keepdims=True)
    g_idx = first_argmax(lg, mg)
    lo = N_GROUPS + EXPERTS_PER_GROUP * g_idx
    le = jnp.where((lane >= lo) & (lane < lo + EXPERTS_PER_GROUP), logits, neg)
    m1 = jnp.max(le, axis=-1, keepdims=True)
    i1 = first_argmax(le, m1)
    le2 = jnp.where(lane == i1, neg, le)
    m2 = jnp.max(le2, axis=-1, keepdims=True)
    i2 = first_argmax(le2, m2)
    r = jnp.exp(m2 - m1)
    w1 = g_p / (1.0 + r)
    w2 = w1 * r
    return i1 - N_GROUPS, i2 - N_GROUPS, w1, w2


def _router_place(e1, e2, w1, w2):
    tm = e1.shape[0]
    lane = lax.broadcasted_iota(jnp.int32, (tm, ROUTE_LANES), 1).astype(F32)
    oh1 = lane == e1
    oh2 = lane == e2
    oh = jnp.where(oh1 | oh2, 1.0, 0.0)
    earlier = (lax.broadcasted_iota(jnp.int32, (tm, tm), 0) > lax.broadcasted_iota(jnp.int32, (tm, tm), 1))
    rank = jnp.dot(earlier.astype(BF16), oh.astype(BF16), preferred_element_type=F32)
    cnt = jnp.sum(oh, axis=0, keepdims=True)
    group = jnp.floor((cnt + (GROUP_ROWS - 1)) * (1.0 / GROUP_ROWS)) * GROUP_ROWS
    lower = (lax.broadcasted_iota(jnp.int32, (ROUTE_LANES, ROUTE_LANES), 0)
             < lax.broadcasted_iota(jnp.int32, (ROUTE_LANES, ROUTE_LANES), 1)).astype(BF16)
    offs = jnp.dot(jnp.broadcast_to(group, (8, ROUTE_LANES)).astype(BF16), lower,
                   preferred_element_type=F32)
    pos = offs[0:1] + rank

    def pick(onehot):
        return jnp.sum(jnp.where(onehot, pos, 0.0), axis=-1, keepdims=True)

    cols = (e1, e2, w1, w2, pick(oh1), pick(oh2))
    out = jnp.zeros((tm, ROUTE_LANES), F32)
    for n, col in enumerate(cols):
        out = jnp.where(lane == n, col, out)
    return out, cnt


def _merge_kernel(x_ref, u_ref, halo_ref, mq_ref, kv_ref, diff_ref,
                  gmix_ref, wg_ref, wbd_ref, pscale_ref, pp_ref, pd_ref, pm_ref, wo_ref,
                  gffn_ref, wr_ref, br_ref,
                  x1_ref, hn_ref, route_ref, cnt_ref, x1_prev_ref, *, tm, seq):
    i = pl.program_id(0)
    d = x_ref.shape[1]

    @pl.when(i == 0)
    def _():
        x1_prev_ref[...] = jnp.zeros(x1_prev_ref.shape, F32)

    hn = (_rms(x1_prev_ref[...]) * gffn_ref[...]).astype(BF16)
    hn_ref[...] = hn
    logits = jnp.dot(hn, wr_ref[...], preferred_element_type=F32) + br_ref[...]
    selection = _router_select(logits)

    x = x_ref[...]
    h = (_rms(x) * gmix_ref[...]).astype(BF16)

    seq_pos0 = (jnp.minimum(i, pl.num_programs(0) - 2) * tm) % seq
    halo = jnp.where(seq_pos0 == 0, 0.0, halo_ref[...])
    pooled = _pool_mixer(u_ref[...], halo, seq_pos0).astype(BF16)
    pool_out = jnp.dot(pooled, wbd_ref[...], preferred_element_type=F32) * pscale_ref[...]
    mem_out = _mem_attention(mq_ref[...], kv_ref[...])

    branches = ((pool_out.astype(BF16), pp_ref), (diff_ref[...], pd_ref), (mem_out.astype(BF16), pm_ref))
    merged = jnp.zeros((tm, d), F32)
    for n, (b_in, p_ref) in enumerate(branches):
        gate = jax.nn.sigmoid(jnp.dot(h, wg_ref[:, n * d:(n + 1) * d], preferred_element_type=F32))
        merged = merged + gate * jnp.dot(b_in, p_ref[...], preferred_element_type=F32)
    route, cnt = _router_place(*selection)
    route_ref[...] = route
    cnt_ref[...] = jnp.broadcast_to(cnt, cnt_ref.shape)

    x1 = x + jnp.dot(merged.astype(BF16), wo_ref[...], preferred_element_type=F32)
    x1_ref[...] = x1
    x1_prev_ref[...] = x1


def _merge(x2, u_pool, mq, kv, diff, gmix, wg, wbd, pscale, pp, pd, pm, wo, gffn, wr, br, seq, n_mem, tm):
    t, d = x2.shape
    nt = t // tm
    tiles_per_seq = seq // tm
    halo_blocks = tm // POOL_HALO

    def tile(i):
        return (jnp.minimum(i, nt - 1), 0)

    def routed(i):
        return (jnp.maximum(i - 1, 0), 0)

    def const(i):
        return (0, 0)

    def wspec(a):
        return pl.BlockSpec(a.shape, const)

    return pl.pallas_call(
        functools.partial(_merge_kernel, tm=tm, seq=seq),
        grid=(nt + 1,),
        in_specs=[
            pl.BlockSpec((tm, d), tile),
            pl.BlockSpec((tm, POOL_WIDTH), tile),
            pl.BlockSpec((POOL_HALO, POOL_WIDTH), lambda i: (jnp.maximum(jnp.minimum(i, nt - 1) * halo_blocks - 1, 0), 0)),
            pl.BlockSpec((tm, MEM_WIDTH), tile),
            pl.BlockSpec((n_mem, 2 * MEM_WIDTH), lambda i: (jnp.minimum(i, nt - 1) // tiles_per_seq, 0)),
            pl.BlockSpec((tm, DA_WIDTH), tile),
            wspec(gmix), wspec(wg), wspec(wbd), wspec(pscale), wspec(pp), wspec(pd), wspec(pm), wspec(wo),
            wspec(gffn), wspec(wr), wspec(br),
        ],
        out_specs=[
            pl.BlockSpec((tm, d), tile),
            pl.BlockSpec((tm, d), routed),
            pl.BlockSpec((tm, ROUTE_LANES), routed),
            pl.BlockSpec((8, ROUTE_LANES), routed),
        ],
        out_shape=[
            jax.ShapeDtypeStruct((t, d), F32),
            jax.ShapeDtypeStruct((t, d), BF16),
            jax.ShapeDtypeStruct((t, ROUTE_LANES), F32),
            jax.ShapeDtypeStruct((nt * 8, ROUTE_LANES), F32),
        ],
        scratch_shapes=[pltpu.VMEM((tm, d), F32)],
        compiler_params=_cparams(1),
        name="merge",
    )(x2, u_pool, u_pool, mq, kv, diff, gmix, wg, wbd, pscale, pp, pd, pm, wo, gffn, wr, br)


def _dispatch_kernel(piece_ref, fill_ref, hn_ref, route_ref, rows_ref, rbuf, zbuf, sem, zsem, *, tm, n_blocks):
    i = pl.program_id(0)
    last = pl.num_programs(0) - 1
    k = rbuf.shape[1]
    pieces = k // GROUP_ROWS
    slot = i % 2
    prev = jnp.maximum(i - 1, 0)

    def wait_all():
        pltpu.make_async_copy(rbuf.at[0], rows_ref.at[pl.ds(0, k)], sem).wait()

    @pl.when(i == 0)
    def _():
        rbuf[1] = jnp.zeros(rbuf.shape[1:], rbuf.dtype)

    @pl.when(i > 0)
    def _():
        wait_all()

    route = route_ref[...]
    kio = lax.broadcasted_iota(jnp.int32, (tm, k), 1).astype(F32)
    onehot = jnp.where((kio == route[:, 4:5]) | (kio == route[:, 5:6]), 1.0, 0.0).astype(BF16)
    for f in range(pieces):
        pltpu.make_async_copy(
            rbuf.at[1 - slot, pl.ds(f * GROUP_ROWS, GROUP_ROWS)],
            rows_ref.at[pl.ds(pl.multiple_of(piece_ref[prev * pieces + f], GROUP_ROWS), GROUP_ROWS)], sem).start()
    rbuf[slot] = lax.dot_general(onehot, hn_ref[...], (((0,), (0,)), ((), ())),
                                 preferred_element_type=F32).astype(BF16)

    @pl.when(i == last)
    def _():
        wait_all()
        zbuf[...] = jnp.zeros(zbuf.shape, zbuf.dtype)

        def tail_copy(dst_row):
            return pltpu.make_async_copy(zbuf.at[pl.ds(0, GROUP_ROWS)], rows_ref.at[pl.ds(dst_row, GROUP_ROWS)], zsem.at[0])

        def block_copy(blk):
            return pltpu.make_async_copy(zbuf, rows_ref.at[pl.ds(pl.multiple_of(blk * MOE_BLOCK, MOE_BLOCK), MOE_BLOCK)],
                                         zsem.at[1])

        n_used = fill_ref[2 * N_EXPERTS]
        for e in range(N_EXPERTS):
            start = fill_ref[e]

            def body(c, carry, start=start):
                tail_copy(pl.multiple_of(start + c * GROUP_ROWS, GROUP_ROWS)).start()
                return carry

            lax.fori_loop(0, fill_ref[N_EXPERTS + e], body, 0)

        def blk_body(b, carry):
            block_copy(b).start()
            return carry

        lax.fori_loop(n_used, n_blocks, blk_body, 0)

        for e in range(N_EXPERTS):
            def wbody(c, carry):
                tail_copy(0).wait()
                return carry

            lax.fori_loop(0, fill_ref[N_EXPERTS + e], wbody, 0)

        def blk_wait(b, carry):
            block_copy(0).wait()
            return carry

        lax.fori_loop(n_used, n_blocks, blk_wait, 0)

        dump_copy = pltpu.make_async_copy(zbuf.at[pl.ds(0, k)], rows_ref.at[pl.ds(n_blocks * MOE_BLOCK, k)], zsem.at[2])
        dump_copy.start()
        dump_copy.wait()


def _tile_rows(tm):
    return 2 * tm + N_EXPERTS * GROUP_ROWS


def _dispatch(piece_rows, fill, hn, route, n_blocks, tm):
    t, d = hn.shape
    nt = t // tm
    k = _tile_rows(tm)
    assert MOE_BLOCK >= k

    def tile(i, p, f):
        return (jnp.minimum(i, nt - 1), 0)

    grid_spec = pltpu.PrefetchScalarGridSpec(
        num_scalar_prefetch=2,
        grid=(nt + 1,),
        in_specs=[pl.BlockSpec((tm, d), tile), pl.BlockSpec((tm, ROUTE_LANES), tile)],
        out_specs=pl.BlockSpec(memory_space=pl.ANY),
        scratch_shapes=[pltpu.VMEM((2, k, d), BF16), pltpu.VMEM((MOE_BLOCK, d), BF16),
                        pltpu.SemaphoreType.DMA(()), pltpu.SemaphoreType.DMA((3,))],
    )
    return pl.pallas_call(
        functools.partial(_dispatch_kernel, tm=tm, n_blocks=n_blocks),
        grid_spec=grid_spec,
        out_shape=jax.ShapeDtypeStruct((n_blocks * MOE_BLOCK + k, d), BF16),
        compiler_params=_cparams(1),
        name="dispatch",
    )(piece_rows, fill, hn, route)


def _ffn_kernel(be_ref, nused_ref, rows_ref, wg_ref, wu_ref, wd_ref, y_ref):
    in_use = pl.program_id(0) < nused_ref[0]

    @pl.when(in_use)
    def _():
        xb = rows_ref[...]
        hg = jnp.dot(xb, wg_ref[...], preferred_element_type=F32)
        hu = jnp.dot(xb, wu_ref[...], preferred_element_type=F32)
        hid = (hg * jax.nn.sigmoid(hg) * hu).astype(BF16)
        y_ref[...] = jnp.dot(hid, wd_ref[...], preferred_element_type=F32).astype(y_ref.dtype)

    @pl.when(jnp.logical_not(in_use))
    def _():
        y_ref[...] = jnp.zeros(y_ref.shape, y_ref.dtype)


def _ffn(block_expert, n_used, rows, wg, wu, wd):
    n_blocks = block_expert.shape[0]
    d = rows.shape[1]
    hidden = wg.shape[2]

    def blk(i, be, nu):
        return (jnp.minimum(i, nu[0] - 1), 0)

    def wblk(i, be, nu):
        return (be[jnp.minimum(i, nu[0] - 1)], 0, 0)

    grid_spec = pltpu.PrefetchScalarGridSpec(
        num_scalar_prefetch=2,
        grid=(n_blocks,),
        in_specs=[
            pl.BlockSpec((MOE_BLOCK, d), blk),
            pl.BlockSpec((None, d, hidden), wblk),
            pl.BlockSpec((None, d, hidden), wblk),
            pl.BlockSpec((None, hidden, d), wblk),
        ],
        out_specs=pl.BlockSpec((MOE_BLOCK, d), lambda i, be, nu: (i, 0)),
    )
    return pl.pallas_call(
        _ffn_kernel,
        grid_spec=grid_spec,
        out_shape=jax.ShapeDtypeStruct((n_blocks * MOE_BLOCK, d), BF16),
        compiler_params=_cparams(1),
        name="ffn",
    )(block_expert, n_used, rows, wg, wu, wd)


def _combine_kernel(piece_ref, x1_ref, route_ref, g_ref, y_ref, o_ref, ybuf, sem, *, tm):
    i = pl.program_id(0)
    nt = pl.num_programs(0)
    slot = i % 2
    kbuf = ybuf.shape[1]
    pieces = kbuf // GROUP_ROWS

    def fetch(step, dst_slot, first, last):
        for f in range(first, last):
            pltpu.make_async_copy(y_ref.at[pl.ds(pl.multiple_of(piece_ref[step * pieces + f], GROUP_ROWS), GROUP_ROWS)],
                                  ybuf.at[dst_slot, pl.ds(f * GROUP_ROWS, GROUP_ROWS)], sem.at[dst_slot]).start()

    def wait_all(the_slot):
        pltpu.make_async_copy(y_ref.at[pl.ds(0, kbuf)], ybuf.at[the_slot], sem.at[the_slot]).wait()

    @pl.when(i == 0)
    def _():
        fetch(0, 0, 0, pieces)

    wait_all(slot)

    route = route_ref[...]
    kio = lax.broadcasted_iota(jnp.int32, (tm, kbuf), 1).astype(F32)
    pw = (jnp.where(kio == route[:, 4:5], route[:, 2:3], 0.0)
          + jnp.where(kio == route[:, 5:6], route[:, 3:4], 0.0)).astype(BF16)
    fetch(jnp.minimum(i + 1, nt - 1), 1 - slot, 0, pieces)
    moe = jnp.dot(pw, ybuf[slot], preferred_element_type=F32)
    o_ref[...] = _rms(x1_ref[...] + moe) * g_ref[...]

    @pl.when(i == nt - 1)
    def _():
        wait_all(1 - slot)


def _combine(piece_rows, x1, route, g, y_rows, tm):
    t, d = x1.shape
    kbuf = _tile_rows(tm)
    grid_spec = pltpu.PrefetchScalarGridSpec(
        num_scalar_prefetch=1,
        grid=(t // tm,),
        in_specs=[
            pl.BlockSpec((tm, d), lambda i, p: (i, 0)),
            pl.BlockSpec((tm, ROUTE_LANES), lambda i, p: (i, 0)),
            pl.BlockSpec((1, d), lambda i, p: (0, 0)),
            pl.BlockSpec(memory_space=pl.ANY),
        ],
        out_specs=pl.BlockSpec((tm, d), lambda i, p: (i, 0)),
        scratch_shapes=[pltpu.VMEM((2, kbuf, d), BF16), pltpu.SemaphoreType.DMA((2,))],
    )
    return pl.pallas_call(
        functools.partial(_combine_kernel, tm=tm),
        grid_spec=grid_spec,
        out_shape=jax.ShapeDtypeStruct((t, d), F32),
        compiler_params=_cparams(1),
        name="combine",
    )(piece_rows, x1, route, g, y_rows)


def _dispatch_plan(cnt_tiles, n_blocks, pieces):
    n_pc = (cnt_tiles + GROUP_ROWS - 1) // GROUP_ROWS
    group = n_pc * GROUP_ROWS
    total = jnp.sum(group, axis=0)
    cap = ((total + MOE_BLOCK - 1) // MOE_BLOCK) * MOE_BLOCK
    pend = jnp.cumsum(cap)
    pstart = pend - cap
    tile_base = pstart[None, :] + jnp.cumsum(group, axis=0) - group
    block_start = jnp.arange(n_blocks, dtype=jnp.int32) * MOE_BLOCK
    block_expert = jnp.minimum(jnp.sum(pend[None, :] <= block_start[:, None], axis=1), N_EXPERTS - 1)
    n_used = jnp.maximum(pend[-1] // MOE_BLOCK, 1).reshape(1)
    fill = jnp.concatenate([pstart + total, (cap - total) // GROUP_ROWS, n_used])

    pc_end = jnp.cumsum(n_pc, axis=1)
    pc_start = pc_end - n_pc
    f = jnp.arange(pieces, dtype=jnp.int32)
    owner = jnp.sum(pc_end[:, :, None] <= f[None, None, :], axis=1)
    is_owner = owner[:, None, :] == jnp.arange(N_EXPERTS, dtype=jnp.int32)[None, :, None]
    piece_rows = jnp.sum(jnp.where(is_owner, (tile_base - pc_start * GROUP_ROWS)[:, :, None] + f * GROUP_ROWS, 0), axis=1)
    dump_rows = n_blocks * MOE_BLOCK + f * GROUP_ROWS
    to_rows = jnp.where(owner < N_EXPERTS, piece_rows, dump_rows[None, :])
    return (to_rows.reshape(-1).astype(jnp.int32), piece_rows.reshape(-1).astype(jnp.int32), fill.astype(jnp.int32),
            block_expert.astype(jnp.int32), n_used.astype(jnp.int32))


def kernel(x, mem, norm_mix_g, w_in, pool_w, pool_scale, lam_q1, lam_k1, lam_q2, lam_k2, subln_g, norm_mem_g,
           w_mem_kv, p_pool, p_diff, p_mem, w_o, norm_ffn_g, w_router_group, b_router_group, w_router_expert,
           b_router_expert, w_expert_gate, w_expert_up, w_expert_down, final_g):
    b, s, d = x.shape
    n_mem = mem.shape[1]
    t = b * s
    layer = 0
    x2 = x.reshape(t, d)
    mem2 = mem.reshape(b * n_mem, d)

    w_in_l = w_in[layer]
    w_proj = w_in_l[:, :PROJ_COLS].astype(BF16)
    w_gates = w_in_l[:, PROJ_COLS:].astype(BF16)
    n_win = len(POOL_WINDOWS)
    eye = jnp.eye(n_win, dtype=F32)
    w_bd = (pool_w[layer][:, :, None, :] * eye[:, None, :, None]).reshape(POOL_WIDTH, POOL_WIDTH).astype(BF16)
    n_route = N_GROUPS + N_EXPERTS
    w_r = jnp.concatenate([w_router_group[layer], w_router_expert[layer].reshape(d, N_EXPERTS),
                           jnp.zeros((d, ROUTE_LANES - n_route), F32)], axis=1).astype(BF16)
    b_r = jnp.concatenate([b_router_group[layer], b_router_expert[layer].reshape(N_EXPERTS),
                           jnp.zeros((ROUTE_LANES - n_route,), F32)]).reshape(1, ROUTE_LANES)

    def row(v):
        return v.reshape(1, -1)

    tm = MOE_TILE
    u_pool, qkv, mq = _proj(x2, row(norm_mix_g[layer]), w_proj, tm=1024)
    kv = _memkv(mem2, row(norm_mem_g[layer]), w_mem_kv[layer].astype(BF16), tm=512)
    diff = _diff_attn(qkv, row(lam_q1[layer]), row(lam_k1[layer]), row(lam_q2[layer]), row(lam_k2[layer]),
                      subln_g[layer].reshape(-1, 1), b, s, tq=256)
    x1, hn, route, cnt = _merge(
        x2, u_pool, mq, kv, diff, row(norm_mix_g[layer]), w_gates, w_bd, row(pool_scale[layer]),
        p_pool[layer].astype(BF16), p_diff[layer].astype(BF16), p_mem[layer].astype(BF16), w_o[layer].astype(BF16),
        row(norm_ffn_g[layer]), w_r, b_r, s, n_mem, tm=tm)

    cnt_tiles = cnt[::8, :N_EXPERTS].astype(jnp.int32)
    n_groups = (t // tm) * N_EXPERTS
    n_blocks = (2 * t + n_groups * (GROUP_ROWS - 1) + N_EXPERTS * (MOE_BLOCK - 1)) // MOE_BLOCK + 1
    to_rows, from_rows, fill, block_expert, n_used = _dispatch_plan(cnt_tiles, n_blocks, _tile_rows(tm) // GROUP_ROWS)
    rows = _dispatch(to_rows, fill, hn, route, n_blocks, tm=tm)
    y_rows = _ffn(block_expert, n_used, rows, w_expert_gate[layer].astype(BF16), w_expert_up[layer].astype(BF16),
                  w_expert_down[layer].astype(BF16))
    out = _combine(from_rows, x1, route, row(final_g), y_rows, tm=tm)
    return out.reshape(b, s, d)
```

```python
import functools

import jax
import jax.numpy as jnp
from jax import lax
from jax.experimental import pallas as pl
from jax.experimental.pallas import tpu as pltpu

F32 = jnp.float32
BF16 = jnp.bfloat16

RMS_EPS = 1e-6
POOL_WINDOWS = (2, 4, 8, 16)
POOL_GROUP = 64
POOL_WIDTH = 256
POOL_HALO = 16
DA_HEADS = 4
DA_HEAD_DIM = 64
DA_V_DIM = 128
SUM_ROWS = 16
DA_QK_WIDTH = 512
DA_WIDTH = 512
MEM_HEADS = 4
MEM_HEAD_DIM = 64
MEM_WIDTH = 256
N_GROUPS = 4
EXPERTS_PER_GROUP = 4
N_EXPERTS = 16
MOE_BLOCK = 1024
MOE_TILE = 256
GROUP_ROWS = 16
LAMBDA_INIT = 0.2
QK_SCALE = 0.125
LOG2_E = 1.4426950408889634
ROUTE_LANES = 128

PROJ_COLS = POOL_WIDTH + 3 * DA_QK_WIDTH + MEM_WIDTH

VMEM_LIMIT = 56 * 1024 * 1024


def _cparams(n_axes):
    return pltpu.CompilerParams(dimension_semantics=("arbitrary",) * n_axes, vmem_limit_bytes=VMEM_LIMIT)


def _rms(x):
    return x * lax.rsqrt(jnp.mean(x * x, axis=-1, keepdims=True) + RMS_EPS)


def _proj_kernel(x_ref, g_ref, w_ref, pool_ref, qkv_ref, mq_ref):
    h = (_rms(x_ref[...]) * g_ref[...]).astype(BF16)
    p = jnp.dot(h, w_ref[...], preferred_element_type=F32)
    pool_ref[...] = p[:, :POOL_WIDTH]
    q_end = POOL_WIDTH + DA_QK_WIDTH
    kv_end = q_end + 2 * DA_QK_WIDTH
    qkv_ref[:, :DA_QK_WIDTH] = (p[:, POOL_WIDTH:q_end] * (QK_SCALE * LOG2_E)).astype(BF16)
    qkv_ref[:, DA_QK_WIDTH:] = p[:, q_end:kv_end].astype(BF16)
    mq_ref[...] = (p[:, kv_end:] * QK_SCALE).astype(BF16)


def _proj(x2, g, w, tm):
    t, d = x2.shape
    return pl.pallas_call(
        _proj_kernel,
        grid=(t // tm,),
        in_specs=[
            pl.BlockSpec((tm, d), lambda i: (i, 0)),
            pl.BlockSpec((1, d), lambda i: (0, 0)),
            pl.BlockSpec((d, PROJ_COLS), lambda i: (0, 0)),
        ],
        out_specs=[
            pl.BlockSpec((tm, POOL_WIDTH), lambda i: (i, 0)),
            pl.BlockSpec((tm, 3 * DA_QK_WIDTH), lambda i: (i, 0)),
            pl.BlockSpec((tm, MEM_WIDTH), lambda i: (i, 0)),
        ],
        out_shape=[
            jax.ShapeDtypeStruct((t, POOL_WIDTH), F32),
            jax.ShapeDtypeStruct((t, 3 * DA_QK_WIDTH), BF16),
            jax.ShapeDtypeStruct((t, MEM_WIDTH), BF16),
        ],
        compiler_params=_cparams(1),
        name="proj",
    )(x2, g, w)


def _memkv_kernel(m_ref, g_ref, w_ref, kv_ref):
    h = (_rms(m_ref[...]) * g_ref[...]).astype(BF16)
    kv_ref[...] = jnp.dot(h, w_ref[...], preferred_element_type=F32).astype(BF16)


def _memkv(mem2, g, w, tm):
    t, d = mem2.shape
    n = w.shape[1]
    return pl.pallas_call(
        _memkv_kernel,
        grid=(t // tm,),
        in_specs=[
            pl.BlockSpec((tm, d), lambda i: (i, 0)),
            pl.BlockSpec((1, d), lambda i: (0, 0)),
            pl.BlockSpec((d, n), lambda i: (0, 0)),
        ],
        out_specs=pl.BlockSpec((tm, n), lambda i: (i, 0)),
        out_shape=jax.ShapeDtypeStruct((t, n), BF16),
        compiler_params=_cparams(1),
        name="mem_kv",
    )(mem2, g, w)


def _diff_attn_kernel(q_ref, k_ref, v_ref, lq1_ref, lk1_ref, lq2_ref, lk2_ref, sg_ref, o_ref,
                      vt_ref, s_ref, m_ref, acc_ref, *, tq):
    i = pl.program_id(1)
    n_chunks = vt_ref.shape[1]
    hw = 2 * DA_HEAD_DIM

    @pl.when(i == 0)
    def _():
        for h in range(DA_HEADS):
            for c in range(n_chunks):
                vt_ref[h, c, :DA_V_DIM] = v_ref[c * tq:(c + 1) * tq, h * hw:(h + 1) * hw].astype(F32).T.astype(BF16)
                vt_ref[h, c, DA_V_DIM:] = jnp.ones((SUM_ROWS, tq), BF16)

    lane = lax.broadcasted_iota(jnp.int32, (tq, hw), 1)
    q2 = []
    for h in range(DA_HEADS):
        q = q_ref[:, h * hw:(h + 1) * hw]
        zero = jnp.zeros_like(q)
        q2.append(jnp.concatenate([jnp.where(lane < DA_HEAD_DIM, q, zero), jnp.where(lane >= DA_HEAD_DIM, q, zero)],
                                  axis=0))

    m_ref[...] = jnp.full(m_ref.shape, -jnp.inf, F32)
    acc_ref[...] = jnp.zeros(acc_ref.shape, F32)

    heads = range(DA_HEADS)

    def score_chunk(j, slot):
        start = pl.multiple_of(j * tq, tq)
        for h in heads:
            k = k_ref[pl.ds(start, tq), h * hw:(h + 1) * hw]
            s_ref[slot, h] = lax.dot_general(k, q2[h], (((1,), (1,)), ((), ())), preferred_element_type=F32)

    def softmax_chunk(slot, masked):
        probs, alphas = [], []
        for h in heads:
            s = s_ref[slot, h]
            if masked:
                key = lax.broadcasted_iota(jnp.int32, s.shape, 0)
                qry = lax.broadcasted_iota(jnp.int32, s.shape, 1)
                qry = jnp.where(qry >= tq, qry - tq, qry)
                s = jnp.where(key <= qry, s, -jnp.inf)
            m = m_ref[h]
            m_new = jnp.maximum(m, jnp.max(s, axis=0, keepdims=True))
            alpha = jnp.exp2(m - m_new)
            p = jnp.exp2(s - m_new)
            m_ref[h] = m_new
            alphas.append(alpha)
            probs.append(p.astype(BF16))
        return probs, alphas

    def value_chunk(j, probs, alphas):
        for h in heads:
            acc_ref[h] = alphas[h] * acc_ref[h] + jnp.dot(vt_ref[h, j], probs[h], preferred_element_type=F32)

    def step(j, slot, masked, prefetch):
        probs, alphas = softmax_chunk(slot, masked)
        if prefetch:
            score_chunk(j + 1, 1 - slot)
        value_chunk(j, probs, alphas)

    score_chunk(0, 0)

    def body(jj, carry):
        step(2 * jj, 0, False, True)
        step(2 * jj + 1, 1, False, True)
        return carry

    lax.fori_loop(0, i // 2, body, 0)

    @pl.when(i % 2 == 1)
    def _():
        step(i - 1, 0, False, True)
        step(i, 1, True, False)

    @pl.when(i % 2 == 0)
    def _():
        step(i, 0, True, False)

    lam = (jnp.exp(jnp.sum(lq1_ref[...] * lk1_ref[...], axis=-1, keepdims=True))
           - jnp.exp(jnp.sum(lq2_ref[...] * lk2_ref[...], axis=-1, keepdims=True)) + LAMBDA_INIT)
    for h in range(DA_HEADS):
        o = acc_ref[h, :DA_V_DIM] / acc_ref[h, DA_V_DIM:DA_V_DIM + 1]
        o = o[:, :tq] - lam * o[:, tq:]
        o = o * lax.rsqrt(jnp.mean(o * o, axis=0, keepdims=True) + RMS_EPS) * sg_ref[...] * (1.0 - LAMBDA_INIT)
        o_ref[:, h * DA_V_DIM:(h + 1) * DA_V_DIM] = o.T.astype(o_ref.dtype)


def _diff_attn(qkv, lq1, lk1, lq2, lk2, subln_g_col, batch, seq, tq):
    t = qkv.shape[0]
    nq = seq // tq
    lam_spec = pl.BlockSpec((1, DA_HEAD_DIM), lambda b, i: (0, 0))
    return pl.pallas_call(
        functools.partial(_diff_attn_kernel, tq=tq),
        grid=(batch, nq),
        in_specs=[
            pl.BlockSpec((tq, DA_QK_WIDTH), lambda b, i: (b * nq + i, 0)),
            pl.BlockSpec((seq, DA_QK_WIDTH), lambda b, i: (b, 1)),
            pl.BlockSpec((seq, DA_WIDTH), lambda b, i: (b, 2)),
            lam_spec, lam_spec, lam_spec, lam_spec,
            pl.BlockSpec((DA_V_DIM, 1), lambda b, i: (0, 0)),
        ],
        out_specs=pl.BlockSpec((tq, DA_WIDTH), lambda b, i: (b * nq + i, 0)),
        out_shape=jax.ShapeDtypeStruct((t, DA_WIDTH), BF16),
        scratch_shapes=[
            pltpu.VMEM((DA_HEADS, nq, DA_V_DIM + SUM_ROWS, tq), BF16),
            pltpu.VMEM((2, DA_HEADS, tq, 2 * tq), F32),
            pltpu.VMEM((DA_HEADS, 1, 2 * tq), F32),
            pltpu.VMEM((DA_HEADS, DA_V_DIM + SUM_ROWS, 2 * tq), F32),
        ],
        compiler_params=_cparams(2),
        name="diff_attn",
    )(qkv, qkv, qkv, lq1, lk1, lq2, lk2, subln_g_col)


def _pool_mixer(u, halo, seq_pos0):
    tm = u.shape[0]
    ext = jnp.concatenate([halo, u], axis=0)
    n = ext.shape[0]
    sums = []
    acc = ext
    for shift in (1, 2, 4, 8):
        acc = acc + pltpu.roll(acc, shift, 0)
        sums.append(acc[POOL_HALO:n])
    lane = lax.broadcasted_iota(jnp.int32, (tm, POOL_WIDTH), 1)
    pos = seq_pos0 + lax.broadcasted_iota(jnp.int32, (tm, POOL_WIDTH), 0)
    win_sum = sums[3]
    win = jnp.full((tm, POOL_WIDTH), POOL_WINDOWS[3], jnp.int32)
    for gi in (2, 1, 0):
        in_group = lane < (gi + 1) * POOL_GROUP
        win_sum = jnp.where(in_group, sums[gi], win_sum)
        win = jnp.where(in_group, POOL_WINDOWS[gi], win)
    cnt = jnp.minimum(pos + 1, win).astype(F32)
    return win_sum / cnt - u


def _mem_attention(mq, kv):
    tm = mq.shape[0]
    head = lax.broadcasted_iota(jnp.int32, mq.shape, 1) // MEM_HEAD_DIM
    zero = jnp.zeros_like(mq)
    q4 = jnp.concatenate([jnp.where(head == h, mq, zero) for h in range(MEM_HEADS)], axis=0)
    mk = kv[:, :MEM_WIDTH]
    mv = kv[:, MEM_WIDTH:]
    s = lax.dot_general(q4, mk, (((1,), (1,)), ((), ())), preferred_element_type=F32)
    p = jnp.exp(s - jnp.max(s, axis=-1, keepdims=True))
    l = jnp.sum(p, axis=-1, keepdims=True)
    o4 = jnp.dot(p.astype(BF16), mv, preferred_element_type=F32) / l
    out = jnp.zeros((tm, MEM_WIDTH), F32)
    for h in range(MEM_HEADS):
        out = jnp.where(head == h, o4[h * tm:(h + 1) * tm], out)
    return out


def _router_select(logits):
    lane = lax.broadcasted_iota(jnp.int32, logits.shape, 1).astype(F32)
    neg = -jnp.inf

    def first_argmax(vals, vmax):
        return jnp.min(jnp.where(vals == vmax, lane, float(ROUTE_LANES)), axis=-1, keepdims=True)

    lg = jnp.where(lane < N_GROUPS, logits, neg)
    mg = jnp.max(lg, axis=-1, keepdims=True)
    g_p = 1.0 / jnp.sum(jnp.exp(lg - mg), axis=-1, keepdims=True)
    g_idx = first_argmax(lg, mg)
    lo = N_GROUPS + EXPERTS_PER_GROUP * g_idx
    le = jnp.where((lane >= lo) & (lane < lo + EXPERTS_PER_GROUP), logits, neg)
    m1 = jnp.max(le, axis=-1, keepdims=True)
    i1 = first_argmax(le, m1)
    le2 = jnp.where(lane == i1, neg, le)
    m2 = jnp.max(le2, axis=-1, keepdims=True)
    i2 = first_argmax(le2, m2)
    r = jnp.exp(m2 - m1)
    w1 = g_p / (1.0 + r)
    w2 = w1 * r
    return i1 - N_GROUPS, i2 - N_GROUPS, w1, w2


def _router_place(e1, e2, w1, w2):
    tm = e1.shape[0]
    lane = lax.broadcasted_iota(jnp.int32, (tm, ROUTE_LANES), 1).astype(F32)
    oh1 = lane == e1
    oh2 = lane == e2
    oh = jnp.where(oh1 | oh2, 1.0, 0.0)
    earlier = (lax.broadcasted_iota(jnp.int32, (tm, tm), 0) > lax.broadcasted_iota(jnp.int32, (tm, tm), 1))
    rank = jnp.dot(earlier.astype(BF16), oh.astype(BF16), preferred_element_type=F32)
    cnt = jnp.sum(oh, axis=0, keepdims=True)
    group = jnp.floor((cnt + (GROUP_ROWS - 1)) * (1.0 / GROUP_ROWS)) * GROUP_ROWS
    lower = (lax.broadcasted_iota(jnp.int32, (ROUTE_LANES, ROUTE_LANES), 0)
             < lax.broadcasted_iota(jnp.int32, (ROUTE_LANES, ROUTE_LANES), 1)).astype(BF16)
    offs = jnp.dot(jnp.broadcast_to(group, (8, ROUTE_LANES)).astype(BF16), lower,
                   preferred_element_type=F32)
    pos = offs[0:1] + rank

    def pick(onehot):
        return jnp.sum(jnp.where(onehot, pos, 0.0), axis=-1, keepdims=True)

    cols = (e1, e2, w1, w2, pick(oh1), pick(oh2))
    out = jnp.zeros((tm, ROUTE_LANES), F32)
    for n, col in enumerate(cols):
        out = jnp.where(lane == n, col, out)
    return out, cnt


def _merge_kernel(x_ref, u_ref, halo_ref, mq_ref, kv_ref, diff_ref,
                  gmix_ref, wg_ref, wbd_ref, pscale_ref, pp_ref, pd_ref, pm_ref, wo_ref,
                  gffn_ref, wr_ref, br_ref,
                  x1_ref, hn_ref, route_ref, cnt_ref, x1_prev_ref, *, tm, seq):
    i = pl.program_id(0)
    d = x_ref.shape[1]

    @pl.when(i == 0)
    def _():
        x1_prev_ref[...] = jnp.zeros(x1_prev_ref.shape, F32)

    hn = (_rms(x1_prev_ref[...]) * gffn_ref[...]).astype(BF16)
    hn_ref[...] = hn
    logits = jnp.dot(hn, wr_ref[...], preferred_element_type=F32) + br_ref[...]
    selection = _router_select(logits)

    x = x_ref[...]
    h = (_rms(x) * gmix_ref[...]).astype(BF16)

    seq_pos0 = (jnp.minimum(i, pl.num_programs(0) - 2) * tm) % seq
    halo = jnp.where(seq_pos0 == 0, 0.0, halo_ref[...])
    pooled = _pool_mixer(u_ref[...], halo, seq_pos0).astype(BF16)
    pool_out = jnp.dot(pooled, wbd_ref[...], preferred_element_type=F32) * pscale_ref[...]
    mem_out = _mem_attention(mq_ref[...], kv_ref[...])

    branches = ((pool_out.astype(BF16), pp_ref), (diff_ref[...], pd_ref), (mem_out.astype(BF16), pm_ref))
    merged = jnp.zeros((tm, d), F32)
    for n, (b_in, p_ref) in enumerate(branches):
        gate = jax.nn.sigmoid(jnp.dot(h, wg_ref[:, n * d:(n + 1) * d], preferred_element_type=F32))
        merged = merged + gate * jnp.dot(b_in, p_ref[...], preferred_element_type=F32)
    route, cnt = _router_place(*selection)
    route_ref[...] = route
    cnt_ref[...] = jnp.broadcast_to(cnt, cnt_ref.shape)

    x1 = x + jnp.dot(merged.astype(BF16), wo_ref[...], preferred_element_type=F32)
    x1_ref[...] = x1
    x1_prev_ref[...] = x1


def _merge(x2, u_pool, mq, kv, diff, gmix, wg, wbd, pscale, pp, pd, pm, wo, gffn, wr, br, seq, n_mem, tm):
    t, d = x2.shape
    nt = t // tm
    tiles_per_seq = seq // tm
    halo_blocks = tm // POOL_HALO

    def tile(i):
        return (jnp.minimum(i, nt - 1), 0)

    def routed(i):
        return (jnp.maximum(i - 1, 0), 0)

    def const(i):
        return (0, 0)

    def wspec(a):
        return pl.BlockSpec(a.shape, const)

    return pl.pallas_call(
        functools.partial(_merge_kernel, tm=tm, seq=seq),
        grid=(nt + 1,),
        in_specs=[
            pl.BlockSpec((tm, d), tile),
            pl.BlockSpec((tm, POOL_WIDTH), tile),
            pl.BlockSpec((POOL_HALO, POOL_WIDTH), lambda i: (jnp.maximum(jnp.minimum(i, nt - 1) * halo_blocks - 1, 0), 0)),
            pl.BlockSpec((tm, MEM_WIDTH), tile),
            pl.BlockSpec((n_mem, 2 * MEM_WIDTH), lambda i: (jnp.minimum(i, nt - 1) // tiles_per_seq, 0)),
            pl.BlockSpec((tm, DA_WIDTH), tile),
            wspec(gmix), wspec(wg), wspec(wbd), wspec(pscale), wspec(pp), wspec(pd), wspec(pm), wspec(wo),
            wspec(gffn), wspec(wr), wspec(br),
        ],
        out_specs=[
            pl.BlockSpec((tm, d), tile),
            pl.BlockSpec((tm, d), routed),
            pl.BlockSpec((tm, ROUTE_LANES), routed),
            pl.BlockSpec((8, ROUTE_LANES), routed),
        ],
        out_shape=[
            jax.ShapeDtypeStruct((t, d), F32),
            jax.ShapeDtypeStruct((t, d), BF16),
            jax.ShapeDtypeStruct((t, ROUTE_LANES), F32),
            jax.ShapeDtypeStruct((nt * 8, ROUTE_LANES), F32),
        ],
        scratch_shapes=[pltpu.VMEM((tm, d), F32)],
        compiler_params=_cparams(1),
        name="merge",
    )(x2, u_pool, u_pool, mq, kv, diff, gmix, wg, wbd, pscale, pp, pd, pm, wo, gffn, wr, br)


def _dispatch_kernel(piece_ref, fill_ref, hn_ref, route_ref, rows_ref, rbuf, zbuf, sem, zsem, *, tm, n_blocks):
    i = pl.program_id(0)
    last = pl.num_programs(0) - 1
    k = rbuf.shape[1]
    pieces = k // GROUP_ROWS
    slot = i % 2
    prev = jnp.maximum(i - 1, 0)

    def wait_all():
        pltpu.make_async_copy(rbuf.at[0], rows_ref.at[pl.ds(0, k)], sem).wait()

    @pl.when(i == 0)
    def _():
        rbuf[1] = jnp.zeros(rbuf.shape[1:], rbuf.dtype)

    @pl.when(i > 0)
    def _():
        wait_all()

    route = route_ref[...]
    kio = lax.broadcasted_iota(jnp.int32, (tm, k), 1).astype(F32)
    onehot = jnp.where((kio == route[:, 4:5]) | (kio == route[:, 5:6]), 1.0, 0.0).astype(BF16)
    for f in range(pieces):
        pltpu.make_async_copy(
            rbuf.at[1 - slot, pl.ds(f * GROUP_ROWS, GROUP_ROWS)],
            rows_ref.at[pl.ds(pl.multiple_of(piece_ref[prev * pieces + f], GROUP_ROWS), GROUP_ROWS)], sem).start()
    rbuf[slot] = lax.dot_general(onehot, hn_ref[...], (((0,), (0,)), ((), ())),
                                 preferred_element_type=F32).astype(BF16)

    @pl.when(i == last)
    def _():
        wait_all()
        zbuf[...] = jnp.zeros(zbuf.shape, zbuf.dtype)

        def tail_copy(dst_row):
            return pltpu.make_async_copy(zbuf.at[pl.ds(0, GROUP_ROWS)], rows_ref.at[pl.ds(dst_row, GROUP_ROWS)], zsem.at[0])

        def block_copy(blk):
            return pltpu.make_async_copy(zbuf, rows_ref.at[pl.ds(pl.multiple_of(blk * MOE_BLOCK, MOE_BLOCK), MOE_BLOCK)],
                                         zsem.at[1])

        n_used = fill_ref[2 * N_EXPERTS]
        for e in range(N_EXPERTS):
            start = fill_ref[e]

            def body(c, carry, start=start):
                tail_copy(pl.multiple_of(start + c * GROUP_ROWS, GROUP_ROWS)).start()
                return carry

            lax.fori_loop(0, fill_ref[N_EXPERTS + e], body, 0)

        def blk_body(b, carry):
            block_copy(b).start()
            return carry

        lax.fori_loop(n_used, n_blocks, blk_body, 0)

        for e in range(N_EXPERTS):
            def wbody(c, carry):
                tail_copy(0).wait()
                return carry

            lax.fori_loop(0, fill_ref[N_EXPERTS + e], wbody, 0)

        def blk_wait(b, carry):
            block_copy(0).wait()
            return carry

        lax.fori_loop(n_used, n_blocks, blk_wait, 0)

        dump_copy = pltpu.make_async_copy(zbuf.at[pl.ds(0, k)], rows_ref.at[pl.ds(n_blocks * MOE_BLOCK, k)], zsem.at[2])
        dump_copy.start()
        dump_copy.wait()


def _tile_rows(tm):
    return 2 * tm + N_EXPERTS * GROUP_ROWS


def _dispatch(piece_rows, fill, hn, route, n_blocks, tm):
    t, d = hn.shape
    nt = t // tm
    k = _tile_rows(tm)
    assert MOE_BLOCK >= k

    def tile(i, p, f):
        return (jnp.minimum(i, nt - 1), 0)

    grid_spec = pltpu.PrefetchScalarGridSpec(
        num_scalar_prefetch=2,
        grid=(nt + 1,),
        in_specs=[pl.BlockSpec((tm, d), tile), pl.BlockSpec((tm, ROUTE_LANES), tile)],
        out_specs=pl.BlockSpec(memory_space=pl.ANY),
        scratch_shapes=[pltpu.VMEM((2, k, d), BF16), pltpu.VMEM((MOE_BLOCK, d), BF16),
                        pltpu.SemaphoreType.DMA(()), pltpu.SemaphoreType.DMA((3,))],
    )
    return pl.pallas_call(
        functools.partial(_dispatch_kernel, tm=tm, n_blocks=n_blocks),
        grid_spec=grid_spec,
        out_shape=jax.ShapeDtypeStruct((n_blocks * MOE_BLOCK + k, d), BF16),
        compiler_params=_cparams(1),
        name="dispatch",
    )(piece_rows, fill, hn, route)


def _ffn_kernel(be_ref, nused_ref, rows_ref, wg_ref, wu_ref, wd_ref, y_ref):
    in_use = pl.program_id(0) < nused_ref[0]

    @pl.when(in_use)
    def _():
        xb = rows_ref[...]
        hg = jnp.dot(xb, wg_ref[...], preferred_element_type=F32)
        hu = jnp.dot(xb, wu_ref[...], preferred_element_type=F32)
        hid = (hg * jax.nn.sigmoid(hg) * hu).astype(BF16)
        y_ref[...] = jnp.dot(hid, wd_ref[...], preferred_element_type=F32).astype(y_ref.dtype)


def _ffn(block_expert, n_used, rows, wg, wu, wd):
    n_blocks = block_expert.shape[0]
    d = rows.shape[1]
    hidden = wg.shape[2]

    def blk(i, be, nu):
        return (jnp.minimum(i, nu[0] - 1), 0)

    def wblk(i, be, nu):
        return (be[jnp.minimum(i, nu[0] - 1)], 0, 0)

    grid_spec = pltpu.PrefetchScalarGridSpec(
        num_scalar_prefetch=2,
        grid=(n_blocks,),
        in_specs=[
            pl.BlockSpec((MOE_BLOCK, d), blk),
            pl.BlockSpec((None, d, hidden), wblk),
            pl.BlockSpec((None, d, hidden), wblk),
            pl.BlockSpec((None, hidden, d), wblk),
        ],
        out_specs=pl.BlockSpec((MOE_BLOCK, d), blk),
    )
    return pl.pallas_call(
        _ffn_kernel,
        grid_spec=grid_spec,
        out_shape=jax.ShapeDtypeStruct(rows.shape, rows.dtype),
        input_output_aliases={2: 0},
        compiler_params=_cparams(1),
        name="ffn",
    )(block_expert, n_used, rows, wg, wu, wd)


def _combine_kernel(piece_ref, x1_ref, route_ref, g_ref, y_ref, o_ref, ybuf, sem, *, tm):
    i = pl.program_id(0)
    nt = pl.num_programs(0)
    slot = i % 2
    kbuf = ybuf.shape[1]
    pieces = kbuf // GROUP_ROWS

    def fetch(step, dst_slot, first, last):
        for f in range(first, last):
            pltpu.make_async_copy(y_ref.at[pl.ds(pl.multiple_of(piece_ref[step * pieces + f], GROUP_ROWS), GROUP_ROWS)],
                                  ybuf.at[dst_slot, pl.ds(f * GROUP_ROWS, GROUP_ROWS)], sem.at[dst_slot]).start()

    def wait_all(the_slot):
        pltpu.make_async_copy(y_ref.at[pl.ds(0, kbuf)], ybuf.at[the_slot], sem.at[the_slot]).wait()

    @pl.when(i == 0)
    def _():
        fetch(0, 0, 0, pieces)

    wait_all(slot)

    route = route_ref[...]
    kio = lax.broadcasted_iota(jnp.int32, (tm, kbuf), 1).astype(F32)
    pw = (jnp.where(kio == route[:, 4:5], route[:, 2:3], 0.0)
          + jnp.where(kio == route[:, 5:6], route[:, 3:4], 0.0)).astype(BF16)
    fetch(jnp.minimum(i + 1, nt - 1), 1 - slot, 0, pieces)
    moe = jnp.dot(pw, ybuf[slot], preferred_element_type=F32)
    o_ref[...] = _rms(x1_ref[...] + moe) * g_ref[...]

    @pl.when(i == nt - 1)
    def _():
        wait_all(1 - slot)


def _combine(piece_rows, x1, route, g, y_rows, tm):
    t, d = x1.shape
    kbuf = _tile_rows(tm)
    grid_spec = pltpu.PrefetchScalarGridSpec(
        num_scalar_prefetch=1,
        grid=(t // tm,),
        in_specs=[
            pl.BlockSpec((tm, d), lambda i, p: (i, 0)),
            pl.BlockSpec((tm, ROUTE_LANES), lambda i, p: (i, 0)),
            pl.BlockSpec((1, d), lambda i, p: (0, 0)),
            pl.BlockSpec(memory_space=pl.ANY),
        ],
        out_specs=pl.BlockSpec((tm, d), lambda i, p: (i, 0)),
        scratch_shapes=[pltpu.VMEM((2, kbuf, d), BF16), pltpu.SemaphoreType.DMA((2,))],
    )
    return pl.pallas_call(
        functools.partial(_combine_kernel, tm=tm),
        grid_spec=grid_spec,
        out_shape=jax.ShapeDtypeStruct((t, d), F32),
        compiler_params=_cparams(1),
        name="combine",
    )(piece_rows, x1, route, g, y_rows)


def _dispatch_plan(cnt_tiles, n_blocks, pieces):
    n_pc = (cnt_tiles + GROUP_ROWS - 1) // GROUP_ROWS
    group = n_pc * GROUP_ROWS
    total = jnp.sum(group, axis=0)
    cap = ((total + MOE_BLOCK - 1) // MOE_BLOCK) * MOE_BLOCK
    pend = jnp.cumsum(cap)
    pstart = pend - cap
    tile_base = pstart[None, :] + jnp.cumsum(group, axis=0) - group
    block_start = jnp.arange(n_blocks, dtype=jnp.int32) * MOE_BLOCK
    block_expert = jnp.minimum(jnp.sum(pend[None, :] <= block_start[:, None], axis=1), N_EXPERTS - 1)
    n_used = jnp.maximum(pend[-1] // MOE_BLOCK, 1).reshape(1)
    fill = jnp.concatenate([pstart + total, (cap - total) // GROUP_ROWS, n_used])

    pc_end = jnp.cumsum(n_pc, axis=1)
    pc_start = pc_end - n_pc
    f = jnp.arange(pieces, dtype=jnp.int32)
    owner = jnp.sum(pc_end[:, :, None] <= f[None, None, :], axis=1)
    is_owner = owner[:, None, :] == jnp.arange(N_EXPERTS, dtype=jnp.int32)[None, :, None]
    piece_rows = jnp.sum(jnp.where(is_owner, (tile_base - pc_start * GROUP_ROWS)[:, :, None] + f * GROUP_ROWS, 0), axis=1)
    dump_rows = n_blocks * MOE_BLOCK + f * GROUP_ROWS
    to_rows = jnp.where(owner < N_EXPERTS, piece_rows, dump_rows[None, :])
    from_rows = jnp.where(owner < N_EXPERTS, piece_rows, (f * GROUP_ROWS)[None, :])
    return (to_rows.reshape(-1).astype(jnp.int32), from_rows.reshape(-1).astype(jnp.int32), fill.astype(jnp.int32),
            block_expert.astype(jnp.int32), n_used.astype(jnp.int32))


def kernel(x, mem, norm_mix_g, w_in, pool_w, pool_scale, lam_q1, lam_k1, lam_q2, lam_k2, subln_g, norm_mem_g,
           w_mem_kv, p_pool, p_diff, p_mem, w_o, norm_ffn_g, w_router_group, b_router_group, w_router_expert,
           b_router_expert, w_expert_gate, w_expert_up, w_expert_down, final_g):
    b, s, d = x.shape
    n_mem = mem.shape[1]
    t = b * s
    layer = 0
    x2 = x.reshape(t, d)
    mem2 = mem.reshape(b * n_mem, d)

    w_in_l = w_in[layer]
    w_proj = w_in_l[:, :PROJ_COLS].astype(BF16)
    w_gates = w_in_l[:, PROJ_COLS:].astype(BF16)
    n_win = len(POOL_WINDOWS)
    eye = jnp.eye(n_win, dtype=F32)
    w_bd = (pool_w[layer][:, :, None, :] * eye[:, None, :, None]).reshape(POOL_WIDTH, POOL_WIDTH).astype(BF16)
    n_route = N_GROUPS + N_EXPERTS
    w_r = jnp.concatenate([w_router_group[layer], w_router_expert[layer].reshape(d, N_EXPERTS),
                           jnp.zeros((d, ROUTE_LANES - n_route), F32)], axis=1).astype(BF16)
    b_r = jnp.concatenate([b_router_group[layer], b_router_expert[layer].reshape(N_EXPERTS),
                           jnp.zeros((ROUTE_LANES - n_route,), F32)]).reshape(1, ROUTE_LANES)

    def row(v):
        return v.reshape(1, -1)

    tm = MOE_TILE
    u_pool, qkv, mq = _proj(x2, row(norm_mix_g[layer]), w_proj, tm=1024)
    kv = _memkv(mem2, row(norm_mem_g[layer]), w_mem_kv[layer].astype(BF16), tm=512)
    diff = _diff_attn(qkv, row(lam_q1[layer]), row(lam_k1[layer]), row(lam_q2[layer]), row(lam_k2[layer]),
                      subln_g[layer].reshape(-1, 1), b, s, tq=256)
    x1, hn, route, cnt = _merge(
        x2, u_pool, mq, kv, diff, row(norm_mix_g[layer]), w_gates, w_bd, row(pool_scale[layer]),
        p_pool[layer].astype(BF16), p_diff[layer].astype(BF16), p_mem[layer].astype(BF16), w_o[layer].astype(BF16),
        row(norm_ffn_g[layer]), w_r, b_r, s, n_mem, tm=tm)

    cnt_tiles = cnt[::8, :N_EXPERTS].astype(jnp.int32)
    n_groups = (t // tm) * N_EXPERTS
    n_blocks = (2 * t + n_groups * (GROUP_ROWS - 1) + N_EXPERTS * (MOE_BLOCK - 1)) // MOE_BLOCK + 1
    to_rows, from_rows, fill, block_expert, n_used = _dispatch_plan(cnt_tiles, n_blocks, _tile_rows(tm) // GROUP_ROWS)
    rows = _dispatch(to_rows, fill, hn, route, n_blocks, tm=tm)
    y_rows = _ffn(block_expert, n_used, rows, w_expert_gate[layer].astype(BF16), w_expert_up[layer].astype(BF16),
                  w_expert_down[layer].astype(BF16))
    out = _combine(from_rows, x1, route, row(final_g), y_rows, tm=tm)
    return out.reshape(b, s, d)
```

```python
import functools

import jax
import jax.numpy as jnp
from jax import lax
from jax.experimental import pallas as pl
from jax.experimental.pallas import tpu as pltpu

F32 = jnp.float32
BF16 = jnp.bfloat16

RMS_EPS = 1e-6
POOL_WINDOWS = (2, 4, 8, 16)
POOL_GROUP = 64
POOL_WIDTH = 256
POOL_HALO = 16
DA_HEADS = 4
DA_HEAD_DIM = 64
DA_V_DIM = 128
SUM_ROWS = 16
DA_QK_WIDTH = 512
DA_WIDTH = 512
MEM_HEADS = 4
MEM_HEAD_DIM = 64
MEM_WIDTH = 256
N_GROUPS = 4
EXPERTS_PER_GROUP = 4
N_EXPERTS = 16
MOE_BLOCK = 1024
MOE_TILE = 256
GROUP_ROWS = 16
LAMBDA_INIT = 0.2
QK_SCALE = 0.125
LOG2_E = 1.4426950408889634
ROUTE_LANES = 128

PROJ_COLS = POOL_WIDTH + 3 * DA_QK_WIDTH + MEM_WIDTH

VMEM_LIMIT = 56 * 1024 * 1024


def _cparams(n_axes):
    return pltpu.CompilerParams(dimension_semantics=("arbitrary",) * n_axes, vmem_limit_bytes=VMEM_LIMIT)


def _rms(x):
    return x * lax.rsqrt(jnp.mean(x * x, axis=-1, keepdims=True) + RMS_EPS)


def _proj_kernel(x_ref, g_ref, w_ref, pool_ref, qkv_ref, mq_ref):
    h = (_rms(x_ref[...]) * g_ref[...]).astype(BF16)
    p = jnp.dot(h, w_ref[...], preferred_element_type=F32)
    pool_ref[...] = p[:, :POOL_WIDTH]
    q_end = POOL_WIDTH + DA_QK_WIDTH
    kv_end = q_end + 2 * DA_QK_WIDTH
    qkv_ref[:, :DA_QK_WIDTH] = (p[:, POOL_WIDTH:q_end] * (QK_SCALE * LOG2_E)).astype(BF16)
    qkv_ref[:, DA_QK_WIDTH:] = p[:, q_end:kv_end].astype(BF16)
    mq_ref[...] = (p[:, kv_end:] * QK_SCALE).astype(BF16)


def _proj(x2, g, w, tm):
    t, d = x2.shape
    return pl.pallas_call(
        _proj_kernel,
        grid=(t // tm,),
        in_specs=[
            pl.BlockSpec((tm, d), lambda i: (i, 0)),
            pl.BlockSpec((1, d), lambda i: (0, 0)),
            pl.BlockSpec((d, PROJ_COLS), lambda i: (0, 0)),
        ],
        out_specs=[
            pl.BlockSpec((tm, POOL_WIDTH), lambda i: (i, 0)),
            pl.BlockSpec((tm, 3 * DA_QK_WIDTH), lambda i: (i, 0)),
            pl.BlockSpec((tm, MEM_WIDTH), lambda i: (i, 0)),
        ],
        out_shape=[
            jax.ShapeDtypeStruct((t, POOL_WIDTH), F32),
            jax.ShapeDtypeStruct((t, 3 * DA_QK_WIDTH), BF16),
            jax.ShapeDtypeStruct((t, MEM_WIDTH), BF16),
        ],
        compiler_params=_cparams(1),
        name="proj",
    )(x2, g, w)


def _memkv_kernel(m_ref, g_ref, w_ref, kv_ref):
    h = (_rms(m_ref[...]) * g_ref[...]).astype(BF16)
    kv_ref[...] = jnp.dot(h, w_ref[...], preferred_element_type=F32).astype(BF16)


def _memkv(mem2, g, w, tm):
    t, d = mem2.shape
    n = w.shape[1]
    return pl.pallas_call(
        _memkv_kernel,
        grid=(t // tm,),
        in_specs=[
            pl.BlockSpec((tm, d), lambda i: (i, 0)),
            pl.BlockSpec((1, d), lambda i: (0, 0)),
            pl.BlockSpec((d, n), lambda i: (0, 0)),
        ],
        out_specs=pl.BlockSpec((tm, n), lambda i: (i, 0)),
        out_shape=jax.ShapeDtypeStruct((t, n), BF16),
        compiler_params=_cparams(1),
        name="mem_kv",
    )(mem2, g, w)


def _diff_attn_kernel(q_ref, k_ref, v_ref, lq1_ref, lk1_ref, lq2_ref, lk2_ref, sg_ref, o_ref,
                      vt_ref, s_ref, m_ref, acc_ref, *, tq):
    i = pl.program_id(1)
    n_chunks = vt_ref.shape[1]
    hw = 2 * DA_HEAD_DIM

    @pl.when(i == 0)
    def _():
        for h in range(DA_HEADS):
            for c in range(n_chunks):
                vt_ref[h, c, :DA_V_DIM] = v_ref[c * tq:(c + 1) * tq, h * hw:(h + 1) * hw].astype(F32).T.astype(BF16)
                vt_ref[h, c, DA_V_DIM:] = jnp.ones((SUM_ROWS, tq), BF16)

    lane = lax.broadcasted_iota(jnp.int32, (tq, hw), 1)
    q2 = []
    for h in range(DA_HEADS):
        q = q_ref[:, h * hw:(h + 1) * hw]
        zero = jnp.zeros_like(q)
        q2.append(jnp.concatenate([jnp.where(lane < DA_HEAD_DIM, q, zero), jnp.where(lane >= DA_HEAD_DIM, q, zero)],
                                  axis=0))

    m_ref[...] = jnp.full(m_ref.shape, -jnp.inf, F32)
    acc_ref[...] = jnp.zeros(acc_ref.shape, F32)

    heads = range(DA_HEADS)

    def score_chunk(j, slot):
        start = pl.multiple_of(j * tq, tq)
        for h in heads:
            k = k_ref[pl.ds(start, tq), h * hw:(h + 1) * hw]
            s_ref[slot, h] = lax.dot_general(k, q2[h], (((1,), (1,)), ((), ())), preferred_element_type=F32)

    def softmax_chunk(slot, masked):
        probs, alphas = [], []
        for h in heads:
            s = s_ref[slot, h]
            if masked:
                key = lax.broadcasted_iota(jnp.int32, s.shape, 0)
                qry = lax.broadcasted_iota(jnp.int32, s.shape, 1)
                qry = jnp.where(qry >= tq, qry - tq, qry)
                s = jnp.where(key <= qry, s, -jnp.inf)
            m = m_ref[h]
            m_new = jnp.maximum(m, jnp.max(s, axis=0, keepdims=True))
            alpha = jnp.exp2(m - m_new)
            p = jnp.exp2(s - m_new)
            m_ref[h] = m_new
            alphas.append(alpha)
            probs.append(p.astype(BF16))
        return probs, alphas

    def value_chunk(j, probs, alphas):
        for h in heads:
            acc_ref[h] = alphas[h] * acc_ref[h] + jnp.dot(vt_ref[h, j], probs[h], preferred_element_type=F32)

    def step(j, slot, masked, prefetch):
        probs, alphas = softmax_chunk(slot, masked)
        if prefetch:
            score_chunk(j + 1, 1 - slot)
        value_chunk(j, probs, alphas)

    score_chunk(0, 0)

    def body(jj, carry):
        step(2 * jj, 0, False, True)
        step(2 * jj + 1, 1, False, True)
        return carry

    lax.fori_loop(0, i // 2, body, 0)

    @pl.when(i % 2 == 1)
    def _():
        step(i - 1, 0, False, True)
        step(i, 1, True, False)

    @pl.when(i % 2 == 0)
    def _():
        step(i, 0, True, False)

    lam = (jnp.exp(jnp.sum(lq1_ref[...] * lk1_ref[...], axis=-1, keepdims=True))
           - jnp.exp(jnp.sum(lq2_ref[...] * lk2_ref[...], axis=-1, keepdims=True)) + LAMBDA_INIT)
    for h in range(DA_HEADS):
        o = acc_ref[h, :DA_V_DIM] / acc_ref[h, DA_V_DIM:DA_V_DIM + 1]
        o = o[:, :tq] - lam * o[:, tq:]
        o = o * lax.rsqrt(jnp.mean(o * o, axis=0, keepdims=True) + RMS_EPS) * sg_ref[...] * (1.0 - LAMBDA_INIT)
        o_ref[:, h * DA_V_DIM:(h + 1) * DA_V_DIM] = o.T.astype(o_ref.dtype)


def _diff_attn(qkv, lq1, lk1, lq2, lk2, subln_g_col, batch, seq, tq):
    t = qkv.shape[0]
    nq = seq // tq
    lam_spec = pl.BlockSpec((1, DA_HEAD_DIM), lambda b, i: (0, 0))
    return pl.pallas_call(
        functools.partial(_diff_attn_kernel, tq=tq),
        grid=(batch, nq),
        in_specs=[
            pl.BlockSpec((tq, DA_QK_WIDTH), lambda b, i: (b * nq + i, 0)),
            pl.BlockSpec((seq, DA_QK_WIDTH), lambda b, i: (b, 1)),
            pl.BlockSpec((seq, DA_WIDTH), lambda b, i: (b, 2)),
            lam_spec, lam_spec, lam_spec, lam_spec,
            pl.BlockSpec((DA_V_DIM, 1), lambda b, i: (0, 0)),
        ],
        out_specs=pl.BlockSpec((tq, DA_WIDTH), lambda b, i: (b * nq + i, 0)),
        out_shape=jax.ShapeDtypeStruct((t, DA_WIDTH), BF16),
        scratch_shapes=[
            pltpu.VMEM((DA_HEADS, nq, DA_V_DIM + SUM_ROWS, tq), BF16),
            pltpu.VMEM((2, DA_HEADS, tq, 2 * tq), F32),
            pltpu.VMEM((DA_HEADS, 1, 2 * tq), F32),
            pltpu.VMEM((DA_HEADS, DA_V_DIM + SUM_ROWS, 2 * tq), F32),
        ],
        compiler_params=_cparams(2),
        name="diff_attn",
    )(qkv, qkv, qkv, lq1, lk1, lq2, lk2, subln_g_col)


def _pool_mixer(u, halo, seq_pos0):
    tm = u.shape[0]
    ext = jnp.concatenate([halo, u], axis=0)
    n = ext.shape[0]
    sums = []
    acc = ext
    for shift in (1, 2, 4, 8):
        acc = acc + pltpu.roll(acc, shift, 0)
        sums.append(acc[POOL_HALO:n])
    lane = lax.broadcasted_iota(jnp.int32, (tm, POOL_WIDTH), 1)
    pos = seq_pos0 + lax.broadcasted_iota(jnp.int32, (tm, POOL_WIDTH), 0)
    win_sum = sums[3]
    win = jnp.full((tm, POOL_WIDTH), POOL_WINDOWS[3], jnp.int32)
    for gi in (2, 1, 0):
        in_group = lane < (gi + 1) * POOL_GROUP
        win_sum = jnp.where(in_group, sums[gi], win_sum)
        win = jnp.where(in_group, POOL_WINDOWS[gi], win)
    cnt = jnp.minimum(pos + 1, win).astype(F32)
    return win_sum / cnt - u


def _mem_probs(mq, mk):
    head = lax.broadcasted_iota(jnp.int32, mq.shape, 1) // MEM_HEAD_DIM
    zero = jnp.zeros_like(mq)
    q4 = jnp.concatenate([jnp.where(head == h, mq, zero) for h in range(MEM_HEADS)], axis=0)
    s = lax.dot_general(q4, mk, (((1,), (1,)), ((), ())), preferred_element_type=F32)
    p = jnp.exp(s - jnp.max(s, axis=-1, keepdims=True))
    return p.astype(BF16), jnp.sum(p, axis=-1, keepdims=True)


def _mem_output(p, l, mv):
    tm = p.shape[0] // MEM_HEADS
    head = lax.broadcasted_iota(jnp.int32, (tm, MEM_WIDTH), 1) // MEM_HEAD_DIM
    o4 = jnp.dot(p, mv, preferred_element_type=F32) / l
    out = jnp.zeros((tm, MEM_WIDTH), F32)
    for h in range(MEM_HEADS):
        out = jnp.where(head == h, o4[h * tm:(h + 1) * tm], out)
    return out


def _router_select(logits):
    lane = lax.broadcasted_iota(jnp.int32, logits.shape, 1).astype(F32)
    neg = -jnp.inf

    def first_argmax(vals, vmax):
        return jnp.min(jnp.where(vals == vmax, lane, float(ROUTE_LANES)), axis=-1, keepdims=True)

    lg = jnp.where(lane < N_GROUPS, logits, neg)
    mg = jnp.max(lg, axis=-1, keepdims=True)
    g_p = 1.0 / jnp.sum(jnp.exp(lg - mg), axis=-1, keepdims=True)
    g_idx = first_argmax(lg, mg)
    lo = N_GROUPS + EXPERTS_PER_GROUP * g_idx
    le = jnp.where((lane >= lo) & (lane < lo + EXPERTS_PER_GROUP), logits, neg)
    m1 = jnp.max(le, axis=-1, keepdims=True)
    i1 = first_argmax(le, m1)
    le2 = jnp.where(lane == i1, neg, le)
    m2 = jnp.max(le2, axis=-1, keepdims=True)
    i2 = first_argmax(le2, m2)
    r = jnp.exp(m2 - m1)
    w1 = g_p / (1.0 + r)
    w2 = w1 * r
    return i1 - N_GROUPS, i2 - N_GROUPS, w1, w2


def _router_place(e1, e2, w1, w2):
    tm = e1.shape[0]
    lane = lax.broadcasted_iota(jnp.int32, (tm, ROUTE_LANES), 1).astype(F32)
    oh1 = lane == e1
    oh2 = lane == e2
    oh = jnp.where(oh1 | oh2, 1.0, 0.0)
    earlier = (lax.broadcasted_iota(jnp.int32, (tm, tm), 0) > lax.broadcasted_iota(jnp.int32, (tm, tm), 1))
    rank = jnp.dot(earlier.astype(BF16), oh.astype(BF16), preferred_element_type=F32)
    cnt = jnp.sum(oh, axis=0, keepdims=True)
    group = jnp.floor((cnt + (GROUP_ROWS - 1)) * (1.0 / GROUP_ROWS)) * GROUP_ROWS
    lower = (lax.broadcasted_iota(jnp.int32, (ROUTE_LANES, ROUTE_LANES), 0)
             < lax.broadcasted_iota(jnp.int32, (ROUTE_LANES, ROUTE_LANES), 1)).astype(BF16)
    offs = jnp.dot(jnp.broadcast_to(group, (8, ROUTE_LANES)).astype(BF16), lower,
                   preferred_element_type=F32)
    pos = offs[0:1] + rank

    def pick(onehot):
        return jnp.sum(jnp.where(onehot, pos, 0.0), axis=-1, keepdims=True)

    cols = (e1, e2, w1, w2, pick(oh1), pick(oh2))
    out = jnp.zeros((tm, ROUTE_LANES), F32)
    for n, col in enumerate(cols):
        out = jnp.where(lane == n, col, out)
    return out, cnt


def _merge_kernel(x_ref, u_ref, halo_ref, mq_ref, kv_ref, diff_ref,
                  gmix_ref, wg_ref, wbd_ref, pscale_ref, pp_ref, pd_ref, pm_ref, wo_ref,
                  gffn_ref, wr_ref, br_ref,
                  x1_ref, hn_ref, route_ref, cnt_ref, x1_prev_ref, *, tm, seq):
    i = pl.program_id(0)
    d = x_ref.shape[1]

    @pl.when(i == 0)
    def _():
        x1_prev_ref[...] = jnp.zeros(x1_prev_ref.shape, F32)

    hn = (_rms(x1_prev_ref[...]) * gffn_ref[...]).astype(BF16)
    hn_ref[...] = hn
    logits = jnp.dot(hn, wr_ref[...], preferred_element_type=F32) + br_ref[...]
    selection = _router_select(logits)

    x = x_ref[...]
    h = (_rms(x) * gmix_ref[...]).astype(BF16)

    seq_pos0 = (jnp.minimum(i, pl.num_programs(0) - 2) * tm) % seq
    halo = jnp.where(seq_pos0 == 0, 0.0, halo_ref[...])
    pooled = _pool_mixer(u_ref[...], halo, seq_pos0).astype(BF16)
    pool_out = jnp.dot(pooled, wbd_ref[...], preferred_element_type=F32) * pscale_ref[...]
    mem_p, mem_l = _mem_probs(mq_ref[...], kv_ref[:, :MEM_WIDTH])

    def branch(n, b_in, p_ref):
        gate = jax.nn.sigmoid(jnp.dot(h, wg_ref[:, n * d:(n + 1) * d], preferred_element_type=F32))
        return gate * jnp.dot(b_in, p_ref[...], preferred_element_type=F32)

    merged = branch(0, pool_out.astype(BF16), pp_ref) + branch(1, diff_ref[...], pd_ref)
    mem_out = _mem_output(mem_p, mem_l, kv_ref[:, MEM_WIDTH:])
    merged = merged + branch(2, mem_out.astype(BF16), pm_ref)
    route, cnt = _router_place(*selection)
    route_ref[...] = route
    cnt_ref[...] = jnp.broadcast_to(cnt, cnt_ref.shape)

    x1 = x + jnp.dot(merged.astype(BF16), wo_ref[...], preferred_element_type=F32)
    x1_ref[...] = x1
    x1_prev_ref[...] = x1


def _merge(x2, u_pool, mq, kv, diff, gmix, wg, wbd, pscale, pp, pd, pm, wo, gffn, wr, br, seq, n_mem, tm):
    t, d = x2.shape
    nt = t // tm
    tiles_per_seq = seq // tm
    halo_blocks = tm // POOL_HALO

    def tile(i):
        return (jnp.minimum(i, nt - 1), 0)

    def routed(i):
        return (jnp.maximum(i - 1, 0), 0)

    def const(i):
        return (0, 0)

    def wspec(a):
        return pl.BlockSpec(a.shape, const)

    return pl.pallas_call(
        functools.partial(_merge_kernel, tm=tm, seq=seq),
        grid=(nt + 1,),
        in_specs=[
            pl.BlockSpec((tm, d), tile),
            pl.BlockSpec((tm, POOL_WIDTH), tile),
            pl.BlockSpec((POOL_HALO, POOL_WIDTH), lambda i: (jnp.maximum(jnp.minimum(i, nt - 1) * halo_blocks - 1, 0), 0)),
            pl.BlockSpec((tm, MEM_WIDTH), tile),
            pl.BlockSpec((n_mem, 2 * MEM_WIDTH), lambda i: (jnp.minimum(i, nt - 1) // tiles_per_seq, 0)),
            pl.BlockSpec((tm, DA_WIDTH), tile),
            wspec(gmix), wspec(wg), wspec(wbd), wspec(pscale), wspec(pp), wspec(pd), wspec(pm), wspec(wo),
            wspec(gffn), wspec(wr), wspec(br),
        ],
        out_specs=[
            pl.BlockSpec((tm, d), tile),
            pl.BlockSpec((tm, d), routed),
            pl.BlockSpec((tm, ROUTE_LANES), routed),
            pl.BlockSpec((8, ROUTE_LANES), routed),
        ],
        out_shape=[
            jax.ShapeDtypeStruct((t, d), F32),
            jax.ShapeDtypeStruct((t, d), BF16),
            jax.ShapeDtypeStruct((t, ROUTE_LANES), F32),
            jax.ShapeDtypeStruct((nt * 8, ROUTE_LANES), F32),
        ],
        scratch_shapes=[pltpu.VMEM((tm, d), F32)],
        compiler_params=_cparams(1),
        name="merge",
    )(x2, u_pool, u_pool, mq, kv, diff, gmix, wg, wbd, pscale, pp, pd, pm, wo, gffn, wr, br)


def _dispatch_kernel(piece_ref, fill_ref, hn_ref, route_ref, rows_ref, rbuf, zbuf, sem, zsem, *, tm, n_blocks):
    i = pl.program_id(0)
    last = pl.num_programs(0) - 1
    k = rbuf.shape[1]
    pieces = k // GROUP_ROWS
    slot = i % 2
    prev = jnp.maximum(i - 1, 0)

    def wait_all():
        pltpu.make_async_copy(rbuf.at[0], rows_ref.at[pl.ds(0, k)], sem).wait()

    @pl.when(i == 0)
    def _():
        rbuf[1] = jnp.zeros(rbuf.shape[1:], rbuf.dtype)

    @pl.when(i > 0)
    def _():
        wait_all()

    route = route_ref[...]
    kio = lax.broadcasted_iota(jnp.int32, (tm, k), 1).astype(F32)
    onehot = jnp.where((kio == route[:, 4:5]) | (kio == route[:, 5:6]), 1.0, 0.0).astype(BF16)
    for f in range(pieces):
        pltpu.make_async_copy(
            rbuf.at[1 - slot, pl.ds(f * GROUP_ROWS, GROUP_ROWS)],
            rows_ref.at[pl.ds(pl.multiple_of(piece_ref[prev * pieces + f], GROUP_ROWS), GROUP_ROWS)], sem).start()
    rbuf[slot] = lax.dot_general(onehot, hn_ref[...], (((0,), (0,)), ((), ())),
                                 preferred_element_type=F32).astype(BF16)

    @pl.when(i == last)
    def _():
        wait_all()
        zbuf[...] = jnp.zeros(zbuf.shape, zbuf.dtype)

        def tail_copy(dst_row):
            return pltpu.make_async_copy(zbuf.at[pl.ds(0, GROUP_ROWS)], rows_ref.at[pl.ds(dst_row, GROUP_ROWS)], zsem.at[0])

        def block_copy(blk):
            return pltpu.make_async_copy(zbuf, rows_ref.at[pl.ds(pl.multiple_of(blk * MOE_BLOCK, MOE_BLOCK), MOE_BLOCK)],
                                         zsem.at[1])

        n_used = fill_ref[2 * N_EXPERTS]
        for e in range(N_EXPERTS):
            start = fill_ref[e]

            def body(c, carry, start=start):
                tail_copy(pl.multiple_of(start + c * GROUP_ROWS, GROUP_ROWS)).start()
                return carry

            lax.fori_loop(0, fill_ref[N_EXPERTS + e], body, 0)

        def blk_body(b, carry):
            block_copy(b).start()
            return carry

        lax.fori_loop(n_used, n_blocks, blk_body, 0)

        for e in range(N_EXPERTS):
            def wbody(c, carry):
                tail_copy(0).wait()
                return carry

            lax.fori_loop(0, fill_ref[N_EXPERTS + e], wbody, 0)

        def blk_wait(b, carry):
            block_copy(0).wait()
            return carry

        lax.fori_loop(n_used, n_blocks, blk_wait, 0)

        dump_copy = pltpu.make_async_copy(zbuf.at[pl.ds(0, k)], rows_ref.at[pl.ds(n_blocks * MOE_BLOCK, k)], zsem.at[2])
        dump_copy.start()
        dump_copy.wait()


def _tile_rows(tm):
    return 2 * tm + N_EXPERTS * GROUP_ROWS


def _dispatch(piece_rows, fill, hn, route, n_blocks, tm):
    t, d = hn.shape
    nt = t // tm
    k = _tile_rows(tm)
    assert MOE_BLOCK >= k

    def tile(i, p, f):
        return (jnp.minimum(i, nt - 1), 0)

    grid_spec = pltpu.PrefetchScalarGridSpec(
        num_scalar_prefetch=2,
        grid=(nt + 1,),
        in_specs=[pl.BlockSpec((tm, d), tile), pl.BlockSpec((tm, ROUTE_LANES), tile)],
        out_specs=pl.BlockSpec(memory_space=pl.ANY),
        scratch_shapes=[pltpu.VMEM((2, k, d), BF16), pltpu.VMEM((MOE_BLOCK, d), BF16),
                        pltpu.SemaphoreType.DMA(()), pltpu.SemaphoreType.DMA((3,))],
    )
    return pl.pallas_call(
        functools.partial(_dispatch_kernel, tm=tm, n_blocks=n_blocks),
        grid_spec=grid_spec,
        out_shape=jax.ShapeDtypeStruct((n_blocks * MOE_BLOCK + k, d), BF16),
        compiler_params=_cparams(1),
        name="dispatch",
    )(piece_rows, fill, hn, route)


def _ffn_kernel(be_ref, nused_ref, rows_ref, wg_ref, wu_ref, wd_ref, y_ref):
    in_use = pl.program_id(0) < nused_ref[0]

    @pl.when(in_use)
    def _():
        xb = rows_ref[...]
        hg = jnp.dot(xb, wg_ref[...].astype(BF16), preferred_element_type=F32)
        hu = jnp.dot(xb, wu_ref[...].astype(BF16), preferred_element_type=F32)
        hid = (hg * jax.nn.sigmoid(hg) * hu).astype(BF16)
        y_ref[...] = jnp.dot(hid, wd_ref[...].astype(BF16), preferred_element_type=F32).astype(y_ref.dtype)


def _ffn(block_expert, n_used, rows, wg, wu, wd):
    n_blocks = block_expert.shape[0]
    d = rows.shape[1]
    hidden = wg.shape[2]

    def blk(i, be, nu):
        return (jnp.minimum(i, nu[0] - 1), 0)

    def wblk(i, be, nu):
        return (be[jnp.minimum(i, nu[0] - 1)], 0, 0)

    grid_spec = pltpu.PrefetchScalarGridSpec(
        num_scalar_prefetch=2,
        grid=(n_blocks,),
        in_specs=[
            pl.BlockSpec((MOE_BLOCK, d), blk),
            pl.BlockSpec((None, d, hidden), wblk),
            pl.BlockSpec((None, d, hidden), wblk),
            pl.BlockSpec((None, hidden, d), wblk),
        ],
        out_specs=pl.BlockSpec((MOE_BLOCK, d), blk),
    )
    return pl.pallas_call(
        _ffn_kernel,
        grid_spec=grid_spec,
        out_shape=jax.ShapeDtypeStruct(rows.shape, rows.dtype),
        input_output_aliases={2: 0},
        compiler_params=_cparams(1),
        name="ffn",
    )(block_expert, n_used, rows, wg, wu, wd)


def _combine_kernel(piece_ref, x1_ref, route_ref, g_ref, y_ref, o_ref, ybuf, sem, *, tm):
    i = pl.program_id(0)
    nt = pl.num_programs(0)
    slot = i % 2
    kbuf = ybuf.shape[1]
    pieces = kbuf // GROUP_ROWS

    def fetch(step, dst_slot, first, last):
        for f in range(first, last):
            pltpu.make_async_copy(y_ref.at[pl.ds(pl.multiple_of(piece_ref[step * pieces + f], GROUP_ROWS), GROUP_ROWS)],
                                  ybuf.at[dst_slot, pl.ds(f * GROUP_ROWS, GROUP_ROWS)], sem.at[dst_slot]).start()

    def wait_all(the_slot):
        pltpu.make_async_copy(y_ref.at[pl.ds(0, kbuf)], ybuf.at[the_slot], sem.at[the_slot]).wait()

    @pl.when(i == 0)
    def _():
        fetch(0, 0, 0, pieces)

    wait_all(slot)

    route = route_ref[...]
    kio = lax.broadcasted_iota(jnp.int32, (tm, kbuf), 1).astype(F32)
    pw = (jnp.where(kio == route[:, 4:5], route[:, 2:3], 0.0)
          + jnp.where(kio == route[:, 5:6], route[:, 3:4], 0.0)).astype(BF16)
    fetch(jnp.minimum(i + 1, nt - 1), 1 - slot, 0, pieces)
    moe = jnp.dot(pw, ybuf[slot], preferred_element_type=F32)
    o_ref[...] = _rms(x1_ref[...] + moe) * g_ref[...]

    @pl.when(i == nt - 1)
    def _():
        wait_all(1 - slot)


def _combine(piece_rows, x1, route, g, y_rows, tm):
    t, d = x1.shape
    kbuf = _tile_rows(tm)
    grid_spec = pltpu.PrefetchScalarGridSpec(
        num_scalar_prefetch=1,
        grid=(t // tm,),
        in_specs=[
            pl.BlockSpec((tm, d), lambda i, p: (i, 0)),
            pl.BlockSpec((tm, ROUTE_LANES), lambda i, p: (i, 0)),
            pl.BlockSpec((1, d), lambda i, p: (0, 0)),
            pl.BlockSpec(memory_space=pl.ANY),
        ],
        out_specs=pl.BlockSpec((tm, d), lambda i, p: (i, 0)),
        scratch_shapes=[pltpu.VMEM((2, kbuf, d), BF16), pltpu.SemaphoreType.DMA((2,))],
    )
    return pl.pallas_call(
        functools.partial(_combine_kernel, tm=tm),
        grid_spec=grid_spec,
        out_shape=jax.ShapeDtypeStruct((t, d), F32),
        compiler_params=_cparams(1),
        name="combine",
    )(piece_rows, x1, route, g, y_rows)


def _dispatch_plan(cnt_tiles, n_blocks, pieces):
    n_pc = (cnt_tiles + GROUP_ROWS - 1) // GROUP_ROWS
    group = n_pc * GROUP_ROWS
    total = jnp.sum(group, axis=0)
    cap = ((total + MOE_BLOCK - 1) // MOE_BLOCK) * MOE_BLOCK
    pend = jnp.cumsum(cap)
    pstart = pend - cap
    tile_base = pstart[None, :] + jnp.cumsum(group, axis=0) - group
    block_start = jnp.arange(n_blocks, dtype=jnp.int32) * MOE_BLOCK
    block_expert = jnp.minimum(jnp.sum(pend[None, :] <= block_start[:, None], axis=1), N_EXPERTS - 1)
    n_used = jnp.maximum(pend[-1] // MOE_BLOCK, 1).reshape(1)
    fill = jnp.concatenate([pstart + total, (cap - total) // GROUP_ROWS, n_used])

    pc_end = jnp.cumsum(n_pc, axis=1)
    pc_start = pc_end - n_pc
    f = jnp.arange(pieces, dtype=jnp.int32)
    owner = jnp.sum(pc_end[:, :, None] <= f[None, None, :], axis=1)
    is_owner = owner[:, None, :] == jnp.arange(N_EXPERTS, dtype=jnp.int32)[None, :, None]
    piece_rows = jnp.sum(jnp.where(is_owner, (tile_base - pc_start * GROUP_ROWS)[:, :, None] + f * GROUP_ROWS, 0), axis=1)
    dump_rows = n_blocks * MOE_BLOCK + f * GROUP_ROWS
    to_rows = jnp.where(owner < N_EXPERTS, piece_rows, dump_rows[None, :])
    from_rows = jnp.where(owner < N_EXPERTS, piece_rows, (f * GROUP_ROWS)[None, :])
    return (to_rows.reshape(-1).astype(jnp.int32), from_rows.reshape(-1).astype(jnp.int32), fill.astype(jnp.int32),
            block_expert.astype(jnp.int32), n_used.astype(jnp.int32))


def kernel(x, mem, norm_mix_g, w_in, pool_w, pool_scale, lam_q1, lam_k1, lam_q2, lam_k2, subln_g, norm_mem_g,
           w_mem_kv, p_pool, p_diff, p_mem, w_o, norm_ffn_g, w_router_group, b_router_group, w_router_expert,
           b_router_expert, w_expert_gate, w_expert_up, w_expert_down, final_g):
    b, s, d = x.shape
    n_mem = mem.shape[1]
    t = b * s
    layer = 0
    x2 = x.reshape(t, d)
    mem2 = mem.reshape(b * n_mem, d)

    w_in_l = w_in[layer]
    w_proj = w_in_l[:, :PROJ_COLS].astype(BF16)
    w_gates = w_in_l[:, PROJ_COLS:].astype(BF16)
    n_win = len(POOL_WINDOWS)
    eye = jnp.eye(n_win, dtype=F32)
    w_bd = (pool_w[layer][:, :, None, :] * eye[:, None, :, None]).reshape(POOL_WIDTH, POOL_WIDTH).astype(BF16)
    n_route = N_GROUPS + N_EXPERTS
    w_r = jnp.concatenate([w_router_group[layer], w_router_expert[layer].reshape(d, N_EXPERTS),
                           jnp.zeros((d, ROUTE_LANES - n_route), F32)], axis=1).astype(BF16)
    b_r = jnp.concatenate([b_router_group[layer], b_router_expert[layer].reshape(N_EXPERTS),
                           jnp.zeros((ROUTE_LANES - n_route,), F32)]).reshape(1, ROUTE_LANES)

    def row(v):
        return v.reshape(1, -1)

    tm = MOE_TILE
    u_pool, qkv, mq = _proj(x2, row(norm_mix_g[layer]), w_proj, tm=1024)
    kv = _memkv(mem2, row(norm_mem_g[layer]), w_mem_kv[layer].astype(BF16), tm=512)
    diff = _diff_attn(qkv, row(lam_q1[layer]), row(lam_k1[layer]), row(lam_q2[layer]), row(lam_k2[layer]),
                      subln_g[layer].reshape(-1, 1), b, s, tq=256)
    x1, hn, route, cnt = _merge(
        x2, u_pool, mq, kv, diff, row(norm_mix_g[layer]), w_gates, w_bd, row(pool_scale[layer]),
        p_pool[layer].astype(BF16), p_diff[layer].astype(BF16), p_mem[layer].astype(BF16), w_o[layer].astype(BF16),
        row(norm_ffn_g[layer]), w_r, b_r, s, n_mem, tm=tm)

    cnt_tiles = cnt[::8, :N_EXPERTS].astype(jnp.int32)
    n_groups = (t // tm) * N_EXPERTS
    n_blocks = (2 * t + n_groups * (GROUP_ROWS - 1) + N_EXPERTS * (MOE_BLOCK - 1)) // MOE_BLOCK + 1
    to_rows, from_rows, fill, block_expert, n_used = _dispatch_plan(cnt_tiles, n_blocks, _tile_rows(tm) // GROUP_ROWS)
    rows = _dispatch(to_rows, fill, hn, route, n_blocks, tm=tm)
    y_rows = _ffn(block_expert, n_used, rows, w_expert_gate[layer], w_expert_up[layer], w_expert_down[layer])
    out = _combine(from_rows, x1, route, row(final_g), y_rows, tm=tm)
    return out.reshape(b, s, d)
```

```python
import functools

import jax
import jax.numpy as jnp
from jax import lax
from jax.experimental import pallas as pl
from jax.experimental.pallas import tpu as pltpu

F32 = jnp.float32
BF16 = jnp.bfloat16

RMS_EPS = 1e-6
POOL_WINDOWS = (2, 4, 8, 16)
POOL_GROUP = 64
POOL_WIDTH = 256
POOL_HALO = 16
DA_HEADS = 4
DA_HEAD_DIM = 64
DA_V_DIM = 128
SUM_ROWS = 16
DA_QK_WIDTH = 512
DA_WIDTH = 512
MEM_HEADS = 4
MEM_HEAD_DIM = 64
MEM_WIDTH = 256
N_GROUPS = 4
EXPERTS_PER_GROUP = 4
N_EXPERTS = 16
MOE_BLOCK = 1024
MOE_TILE = 256
GROUP_ROWS = 16
LAMBDA_INIT = 0.2
QK_SCALE = 0.125
LOG2_E = 1.4426950408889634
ROUTE_LANES = 128

PROJ_COLS = POOL_WIDTH + 3 * DA_QK_WIDTH + MEM_WIDTH

VMEM_LIMIT = 56 * 1024 * 1024


def _cparams(n_axes):
    return pltpu.CompilerParams(dimension_semantics=("arbitrary",) * n_axes, vmem_limit_bytes=VMEM_LIMIT)


def _rms(x):
    return x * lax.rsqrt(jnp.mean(x * x, axis=-1, keepdims=True) + RMS_EPS)


def _proj_kernel(x_ref, g_ref, w_ref, pool_ref, qkv_ref, mq_ref):
    h = (_rms(x_ref[...]) * g_ref[...]).astype(BF16)
    p = jnp.dot(h, w_ref[...], preferred_element_type=F32)
    pool_ref[...] = p[:, :POOL_WIDTH]
    q_end = POOL_WIDTH + DA_QK_WIDTH
    kv_end = q_end + 2 * DA_QK_WIDTH
    qkv_ref[:, :DA_QK_WIDTH] = (p[:, POOL_WIDTH:q_end] * (QK_SCALE * LOG2_E)).astype(BF16)
    qkv_ref[:, DA_QK_WIDTH:] = p[:, q_end:kv_end].astype(BF16)
    mq_ref[...] = (p[:, kv_end:] * QK_SCALE).astype(BF16)


def _proj(x2, g, w, tm):
    t, d = x2.shape
    return pl.pallas_call(
        _proj_kernel,
        grid=(t // tm,),
        in_specs=[
            pl.BlockSpec((tm, d), lambda i: (i, 0)),
            pl.BlockSpec((1, d), lambda i: (0, 0)),
            pl.BlockSpec((d, PROJ_COLS), lambda i: (0, 0)),
        ],
        out_specs=[
            pl.BlockSpec((tm, POOL_WIDTH), lambda i: (i, 0)),
            pl.BlockSpec((tm, 3 * DA_QK_WIDTH), lambda i: (i, 0)),
            pl.BlockSpec((tm, MEM_WIDTH), lambda i: (i, 0)),
        ],
        out_shape=[
            jax.ShapeDtypeStruct((t, POOL_WIDTH), F32),
            jax.ShapeDtypeStruct((t, 3 * DA_QK_WIDTH), BF16),
            jax.ShapeDtypeStruct((t, MEM_WIDTH), BF16),
        ],
        compiler_params=_cparams(1),
        name="proj",
    )(x2, g, w)


def _memkv_kernel(m_ref, g_ref, w_ref, kv_ref):
    h = (_rms(m_ref[...]) * g_ref[...]).astype(BF16)
    kv_ref[...] = jnp.dot(h, w_ref[...], preferred_element_type=F32).astype(BF16)


def _memkv(mem2, g, w, tm):
    t, d = mem2.shape
    n = w.shape[1]
    return pl.pallas_call(
        _memkv_kernel,
        grid=(t // tm,),
        in_specs=[
            pl.BlockSpec((tm, d), lambda i: (i, 0)),
            pl.BlockSpec((1, d), lambda i: (0, 0)),
            pl.BlockSpec((d, n), lambda i: (0, 0)),
        ],
        out_specs=pl.BlockSpec((tm, n), lambda i: (i, 0)),
        out_shape=jax.ShapeDtypeStruct((t, n), BF16),
        compiler_params=_cparams(1),
        name="mem_kv",
    )(mem2, g, w)


def _diff_attn_kernel(q_ref, k_ref, v_ref, lq1_ref, lk1_ref, lq2_ref, lk2_ref, sg_ref, o_ref,
                      vt_ref, s_ref, m_ref, acc_ref, *, tq):
    i = pl.program_id(1)
    n_chunks = vt_ref.shape[1]
    hw = 2 * DA_HEAD_DIM

    @pl.when(i == 0)
    def _():
        for h in range(DA_HEADS):
            for c in range(n_chunks):
                vt_ref[h, c, :DA_V_DIM] = v_ref[c * tq:(c + 1) * tq, h * hw:(h + 1) * hw].astype(F32).T.astype(BF16)
                vt_ref[h, c, DA_V_DIM:] = jnp.ones((SUM_ROWS, tq), BF16)

    lane = lax.broadcasted_iota(jnp.int32, (tq, hw), 1)

    def stacked_queries(tile):
        row0 = pl.multiple_of(tile * tq, tq)
        out = []
        for h in range(DA_HEADS):
            q = q_ref[pl.ds(row0, tq), h * hw:(h + 1) * hw]
            zero = jnp.zeros_like(q)
            out.append(jnp.concatenate([jnp.where(lane < DA_HEAD_DIM, q, zero), jnp.where(lane >= DA_HEAD_DIM, q, zero)],
                                       axis=0))
        return out

    q2 = stacked_queries(i)

    m_ref[...] = jnp.full(m_ref.shape, -jnp.inf, F32)
    acc_ref[...] = jnp.zeros(acc_ref.shape, F32)

    heads = range(DA_HEADS)

    def score_chunk(j, slot, queries=q2):
        start = pl.multiple_of(j * tq, tq)
        for h in heads:
            k = k_ref[pl.ds(start, tq), h * hw:(h + 1) * hw]
            s_ref[slot, h] = lax.dot_general(k, queries[h], (((1,), (1,)), ((), ())), preferred_element_type=F32)

    def softmax_chunk(slot, masked):
        probs, alphas = [], []
        for h in heads:
            s = s_ref[slot, h]
            if masked:
                key = lax.broadcasted_iota(jnp.int32, s.shape, 0)
                qry = lax.broadcasted_iota(jnp.int32, s.shape, 1)
                qry = jnp.where(qry >= tq, qry - tq, qry)
                s = jnp.where(key <= qry, s, -jnp.inf)
            m = m_ref[h]
            m_new = jnp.maximum(m, jnp.max(s, axis=0, keepdims=True))
            alpha = jnp.exp2(m - m_new)
            p = jnp.exp2(s - m_new)
            m_ref[h] = m_new
            alphas.append(alpha)
            probs.append(p.astype(BF16))
        return probs, alphas

    def value_chunk(j, probs, alphas):
        for h in heads:
            acc_ref[h] = alphas[h] * acc_ref[h] + jnp.dot(vt_ref[h, j], probs[h], preferred_element_type=F32)

    def step(j, slot, masked, prefetch):
        probs, alphas = softmax_chunk(slot, masked)
        if prefetch:
            score_chunk(j + 1, 1 - slot)
        value_chunk(j, probs, alphas)

    @pl.when(i == 0)
    def _():
        score_chunk(0, 0)

    def body(jj, carry):
        step(2 * jj, 0, False, True)
        step(2 * jj + 1, 1, False, True)
        return carry

    lax.fori_loop(0, i // 2, body, 0)

    @pl.when(i % 2 == 1)
    def _():
        step(i - 1, 0, False, True)
        step(i, 1, True, False)

    @pl.when(i % 2 == 0)
    def _():
        step(i, 0, True, False)

    score_chunk(0, 0, stacked_queries(jnp.minimum(i + 1, n_chunks - 1)))

    lam = (jnp.exp(jnp.sum(lq1_ref[...] * lk1_ref[...], axis=-1, keepdims=True))
           - jnp.exp(jnp.sum(lq2_ref[...] * lk2_ref[...], axis=-1, keepdims=True)) + LAMBDA_INIT)
    for h in range(DA_HEADS):
        o = acc_ref[h, :DA_V_DIM] / acc_ref[h, DA_V_DIM:DA_V_DIM + 1]
        o = o[:, :tq] - lam * o[:, tq:]
        o = o * lax.rsqrt(jnp.mean(o * o, axis=0, keepdims=True) + RMS_EPS) * sg_ref[...] * (1.0 - LAMBDA_INIT)
        o_ref[:, h * DA_V_DIM:(h + 1) * DA_V_DIM] = o.T.astype(o_ref.dtype)


def _diff_attn(qkv, lq1, lk1, lq2, lk2, subln_g_col, batch, seq, tq):
    t = qkv.shape[0]
    nq = seq // tq
    lam_spec = pl.BlockSpec((1, DA_HEAD_DIM), lambda b, i: (0, 0))
    return pl.pallas_call(
        functools.partial(_diff_attn_kernel, tq=tq),
        grid=(batch, nq),
        in_specs=[
            pl.BlockSpec((seq, DA_QK_WIDTH), lambda b, i: (b, 0)),
            pl.BlockSpec((seq, DA_QK_WIDTH), lambda b, i: (b, 1)),
            pl.BlockSpec((seq, DA_WIDTH), lambda b, i: (b, 2)),
            lam_spec, lam_spec, lam_spec, lam_spec,
            pl.BlockSpec((DA_V_DIM, 1), lambda b, i: (0, 0)),
        ],
        out_specs=pl.BlockSpec((tq, DA_WIDTH), lambda b, i: (b * nq + i, 0)),
        out_shape=jax.ShapeDtypeStruct((t, DA_WIDTH), BF16),
        scratch_shapes=[
            pltpu.VMEM((DA_HEADS, nq, DA_V_DIM + SUM_ROWS, tq), BF16),
            pltpu.VMEM((2, DA_HEADS, tq, 2 * tq), F32),
            pltpu.VMEM((DA_HEADS, 1, 2 * tq), F32),
            pltpu.VMEM((DA_HEADS, DA_V_DIM + SUM_ROWS, 2 * tq), F32),
        ],
        compiler_params=_cparams(2),
        name="diff_attn",
    )(qkv, qkv, qkv, lq1, lk1, lq2, lk2, subln_g_col)


def _pool_mixer(u, halo, seq_pos0):
    tm = u.shape[0]
    ext = jnp.concatenate([halo, u], axis=0)
    n = ext.shape[0]
    sums = []
    acc = ext
    for shift in (1, 2, 4, 8):
        acc = acc + pltpu.roll(acc, shift, 0)
        sums.append(acc[POOL_HALO:n])
    lane = lax.broadcasted_iota(jnp.int32, (tm, POOL_WIDTH), 1)
    pos = seq_pos0 + lax.broadcasted_iota(jnp.int32, (tm, POOL_WIDTH), 0)
    win_sum = sums[3]
    win = jnp.full((tm, POOL_WIDTH), POOL_WINDOWS[3], jnp.int32)
    for gi in (2, 1, 0):
        in_group = lane < (gi + 1) * POOL_GROUP
        win_sum = jnp.where(in_group, sums[gi], win_sum)
        win = jnp.where(in_group, POOL_WINDOWS[gi], win)
    cnt = jnp.minimum(pos + 1, win).astype(F32)
    return win_sum / cnt - u


def _mem_probs(mq, mk):
    head = lax.broadcasted_iota(jnp.int32, mq.shape, 1) // MEM_HEAD_DIM
    zero = jnp.zeros_like(mq)
    q4 = jnp.concatenate([jnp.where(head == h, mq, zero) for h in range(MEM_HEADS)], axis=0)
    s = lax.dot_general(q4, mk, (((1,), (1,)), ((), ())), preferred_element_type=F32)
    p = jnp.exp(s - jnp.max(s, axis=-1, keepdims=True))
    return p.astype(BF16), jnp.sum(p, axis=-1, keepdims=True)


def _mem_output(p, l, mv):
    tm = p.shape[0] // MEM_HEADS
    head = lax.broadcasted_iota(jnp.int32, (tm, MEM_WIDTH), 1) // MEM_HEAD_DIM
    o4 = jnp.dot(p, mv, preferred_element_type=F32) / l
    out = jnp.zeros((tm, MEM_WIDTH), F32)
    for h in range(MEM_HEADS):
        out = jnp.where(head == h, o4[h * tm:(h + 1) * tm], out)
    return out


def _router_select(logits):
    lane = lax.broadcasted_iota(jnp.int32, logits.shape, 1).astype(F32)
    neg = -jnp.inf

    def first_argmax(vals, vmax):
        return jnp.min(jnp.where(vals == vmax, lane, float(ROUTE_LANES)), axis=-1, keepdims=True)

    lg = jnp.where(lane < N_GROUPS, logits, neg)
    mg = jnp.max(lg, axis=-1, keepdims=True)
    g_p = 1.0 / jnp.sum(jnp.exp(lg - mg), axis=-1, keepdims=True)
    g_idx = first_argmax(lg, mg)
    lo = N_GROUPS + EXPERTS_PER_GROUP * g_idx
    le = jnp.where((lane >= lo) & (lane < lo + EXPERTS_PER_GROUP), logits, neg)
    m1 = jnp.max(le, axis=-1, keepdims=True)
    i1 = first_argmax(le, m1)
    le2 = jnp.where(lane == i1, neg, le)
    m2 = jnp.max(le2, axis=-1, keepdims=True)
    i2 = first_argmax(le2, m2)
    r = jnp.exp(m2 - m1)
    w1 = g_p / (1.0 + r)
    w2 = w1 * r
    return i1 - N_GROUPS, i2 - N_GROUPS, w1, w2


def _router_place(e1, e2, w1, w2):
    tm = e1.shape[0]
    lane = lax.broadcasted_iota(jnp.int32, (tm, ROUTE_LANES), 1).astype(F32)
    oh1 = lane == e1
    oh2 = lane == e2
    oh = jnp.where(oh1 | oh2, 1.0, 0.0)
    earlier = (lax.broadcasted_iota(jnp.int32, (tm, tm), 0) > lax.broadcasted_iota(jnp.int32, (tm, tm), 1))
    rank = jnp.dot(earlier.astype(BF16), oh.astype(BF16), preferred_element_type=F32)
    cnt = jnp.sum(oh, axis=0, keepdims=True)
    group = jnp.floor((cnt + (GROUP_ROWS - 1)) * (1.0 / GROUP_ROWS)) * GROUP_ROWS
    lower = (lax.broadcasted_iota(jnp.int32, (ROUTE_LANES, ROUTE_LANES), 0)
             < lax.broadcasted_iota(jnp.int32, (ROUTE_LANES, ROUTE_LANES), 1)).astype(BF16)
    offs = jnp.dot(jnp.broadcast_to(group, (8, ROUTE_LANES)).astype(BF16), lower,
                   preferred_element_type=F32)
    pos = offs[0:1] + rank

    def pick(onehot):
        return jnp.sum(jnp.where(onehot, pos, 0.0), axis=-1, keepdims=True)

    cols = (e1, e2, w1, w2, pick(oh1), pick(oh2))
    out = jnp.zeros((tm, ROUTE_LANES), F32)
    for n, col in enumerate(cols):
        out = jnp.where(lane == n, col, out)
    return out, cnt


def _merge_kernel(x_ref, u_ref, halo_ref, mq_ref, kv_ref, diff_ref,
                  gmix_ref, wg_ref, wbd_ref, pscale_ref, pp_ref, pd_ref, pm_ref, wo_ref,
                  gffn_ref, wr_ref, br_ref,
                  x1_ref, hn_ref, route_ref, cnt_ref, x1_prev_ref, *, tm, seq):
    i = pl.program_id(0)
    d = x_ref.shape[1]

    @pl.when(i == 0)
    def _():
        x1_prev_ref[...] = jnp.zeros(x1_prev_ref.shape, F32)

    hn = (_rms(x1_prev_ref[...]) * gffn_ref[...]).astype(BF16)
    hn_ref[...] = hn
    logits = jnp.dot(hn, wr_ref[...], preferred_element_type=F32) + br_ref[...]
    selection = _router_select(logits)

    x = x_ref[...]
    h = (_rms(x) * gmix_ref[...]).astype(BF16)

    seq_pos0 = (jnp.minimum(i, pl.num_programs(0) - 2) * tm) % seq
    halo = jnp.where(seq_pos0 == 0, 0.0, halo_ref[...])
    pooled = _pool_mixer(u_ref[...], halo, seq_pos0).astype(BF16)
    pool_out = jnp.dot(pooled, wbd_ref[...], preferred_element_type=F32) * pscale_ref[...]
    mem_p, mem_l = _mem_probs(mq_ref[...], kv_ref[:, :MEM_WIDTH])

    def branch(n, b_in, p_ref):
        gate = jax.nn.sigmoid(jnp.dot(h, wg_ref[:, n * d:(n + 1) * d], preferred_element_type=F32))
        return gate * jnp.dot(b_in, p_ref[...], preferred_element_type=F32)

    merged = branch(0, pool_out.astype(BF16), pp_ref) + branch(1, diff_ref[...], pd_ref)
    mem_out = _mem_output(mem_p, mem_l, kv_ref[:, MEM_WIDTH:])
    merged = merged + branch(2, mem_out.astype(BF16), pm_ref)
    route, cnt = _router_place(*selection)
    route_ref[...] = route
    cnt_ref[...] = jnp.broadcast_to(cnt, cnt_ref.shape)

    x1 = x + jnp.dot(merged.astype(BF16), wo_ref[...], preferred_element_type=F32)
    x1_ref[...] = x1
    x1_prev_ref[...] = x1


def _merge(x2, u_pool, mq, kv, diff, gmix, wg, wbd, pscale, pp, pd, pm, wo, gffn, wr, br, seq, n_mem, tm):
    t, d = x2.shape
    nt = t // tm
    tiles_per_seq = seq // tm
    halo_blocks = tm // POOL_HALO

    def tile(i):
        return (jnp.minimum(i, nt - 1), 0)

    def routed(i):
        return (jnp.maximum(i - 1, 0), 0)

    def const(i):
        return (0, 0)

    def wspec(a):
        return pl.BlockSpec(a.shape, const)

    return pl.pallas_call(
        functools.partial(_merge_kernel, tm=tm, seq=seq),
        grid=(nt + 1,),
        in_specs=[
            pl.BlockSpec((tm, d), tile),
            pl.BlockSpec((tm, POOL_WIDTH), tile),
            pl.BlockSpec((POOL_HALO, POOL_WIDTH), lambda i: (jnp.maximum(jnp.minimum(i, nt - 1) * halo_blocks - 1, 0), 0)),
            pl.BlockSpec((tm, MEM_WIDTH), tile),
            pl.BlockSpec((n_mem, 2 * MEM_WIDTH), lambda i: (jnp.minimum(i, nt - 1) // tiles_per_seq, 0)),
            pl.BlockSpec((tm, DA_WIDTH), tile),
            wspec(gmix), wspec(wg), wspec(wbd), wspec(pscale), wspec(pp), wspec(pd), wspec(pm), wspec(wo),
            wspec(gffn), wspec(wr), wspec(br),
        ],
        out_specs=[
            pl.BlockSpec((tm, d), tile),
            pl.BlockSpec((tm, d), routed),
            pl.BlockSpec((tm, ROUTE_LANES), routed),
            pl.BlockSpec((8, ROUTE_LANES), routed),
        ],
        out_shape=[
            jax.ShapeDtypeStruct((t, d), F32),
            jax.ShapeDtypeStruct((t, d), BF16),
            jax.ShapeDtypeStruct((t, ROUTE_LANES), F32),
            jax.ShapeDtypeStruct((nt * 8, ROUTE_LANES), F32),
        ],
        scratch_shapes=[pltpu.VMEM((tm, d), F32)],
        compiler_params=_cparams(1),
        name="merge",
    )(x2, u_pool, u_pool, mq, kv, diff, gmix, wg, wbd, pscale, pp, pd, pm, wo, gffn, wr, br)


def _dispatch_kernel(piece_ref, fill_ref, hn_ref, route_ref, rows_ref, rbuf, zbuf, sem, zsem, *, tm, n_blocks):
    i = pl.program_id(0)
    last = pl.num_programs(0) - 1
    k = rbuf.shape[1]
    pieces = k // GROUP_ROWS
    slot = i % 2
    prev = jnp.maximum(i - 1, 0)

    def wait_all():
        pltpu.make_async_copy(rbuf.at[0], rows_ref.at[pl.ds(0, k)], sem).wait()

    @pl.when(i == 0)
    def _():
        rbuf[1] = jnp.zeros(rbuf.shape[1:], rbuf.dtype)

    @pl.when(i > 0)
    def _():
        wait_all()

    route = route_ref[...]
    kio = lax.broadcasted_iota(jnp.int32, (tm, k), 1).astype(F32)
    onehot = jnp.where((kio == route[:, 4:5]) | (kio == route[:, 5:6]), 1.0, 0.0).astype(BF16)
    for f in range(pieces):
        pltpu.make_async_copy(
            rbuf.at[1 - slot, pl.ds(f * GROUP_ROWS, GROUP_ROWS)],
            rows_ref.at[pl.ds(pl.multiple_of(piece_ref[prev * pieces + f], GROUP_ROWS), GROUP_ROWS)], sem).start()
    rbuf[slot] = lax.dot_general(onehot, hn_ref[...], (((0,), (0,)), ((), ())),
                                 preferred_element_type=F32).astype(BF16)

    @pl.when(i == last)
    def _():
        wait_all()
        zbuf[...] = jnp.zeros(zbuf.shape, zbuf.dtype)

        def tail_copy(dst_row):
            return pltpu.make_async_copy(zbuf.at[pl.ds(0, GROUP_ROWS)], rows_ref.at[pl.ds(dst_row, GROUP_ROWS)], zsem.at[0])

        def block_copy(blk):
            return pltpu.make_async_copy(zbuf, rows_ref.at[pl.ds(pl.multiple_of(blk * MOE_BLOCK, MOE_BLOCK), MOE_BLOCK)],
                                         zsem.at[1])

        n_used = fill_ref[2 * N_EXPERTS]
        for e in range(N_EXPERTS):
            start = fill_ref[e]

            def body(c, carry, start=start):
                tail_copy(pl.multiple_of(start + c * GROUP_ROWS, GROUP_ROWS)).start()
                return carry

            lax.fori_loop(0, fill_ref[N_EXPERTS + e], body, 0)

        def blk_body(b, carry):
            block_copy(b).start()
            return carry

        lax.fori_loop(n_used, n_blocks, blk_body, 0)

        for e in range(N_EXPERTS):
            def wbody(c, carry):
                tail_copy(0).wait()
                return carry

            lax.fori_loop(0, fill_ref[N_EXPERTS + e], wbody, 0)

        def blk_wait(b, carry):
            block_copy(0).wait()
            return carry

        lax.fori_loop(n_used, n_blocks, blk_wait, 0)

        dump_copy = pltpu.make_async_copy(zbuf.at[pl.ds(0, k)], rows_ref.at[pl.ds(n_blocks * MOE_BLOCK, k)], zsem.at[2])
        dump_copy.start()
        dump_copy.wait()


def _tile_rows(tm):
    return 2 * tm + N_EXPERTS * GROUP_ROWS


def _dispatch(piece_rows, fill, hn, route, n_blocks, tm):
    t, d = hn.shape
    nt = t // tm
    k = _tile_rows(tm)
    assert MOE_BLOCK >= k

    def tile(i, p, f):
        return (jnp.minimum(i, nt - 1), 0)

    grid_spec = pltpu.PrefetchScalarGridSpec(
        num_scalar_prefetch=2,
        grid=(nt + 1,),
        in_specs=[pl.BlockSpec((tm, d), tile), pl.BlockSpec((tm, ROUTE_LANES), tile)],
        out_specs=pl.BlockSpec(memory_space=pl.ANY),
        scratch_shapes=[pltpu.VMEM((2, k, d), BF16), pltpu.VMEM((MOE_BLOCK, d), BF16),
                        pltpu.SemaphoreType.DMA(()), pltpu.SemaphoreType.DMA((3,))],
    )
    return pl.pallas_call(
        functools.partial(_dispatch_kernel, tm=tm, n_blocks=n_blocks),
        grid_spec=grid_spec,
        out_shape=jax.ShapeDtypeStruct((n_blocks * MOE_BLOCK + k, d), BF16),
        compiler_params=_cparams(1),
        name="dispatch",
    )(piece_rows, fill, hn, route)


def _ffn_kernel(be_ref, nused_ref, rows_ref, wg_ref, wu_ref, wd_ref, y_ref):
    in_use = pl.program_id(0) < nused_ref[0]

    @pl.when(in_use)
    def _():
        xb = rows_ref[...]
        hg = jnp.dot(xb, wg_ref[...].astype(BF16), preferred_element_type=F32)
        hu = jnp.dot(xb, wu_ref[...].astype(BF16), preferred_element_type=F32)
        hid = (hg * jax.nn.sigmoid(hg) * hu).astype(BF16)
        y_ref[...] = jnp.dot(hid, wd_ref[...].astype(BF16), preferred_element_type=F32).astype(y_ref.dtype)


def _ffn(block_expert, n_used, rows, wg, wu, wd):
    n_blocks = block_expert.shape[0]
    d = rows.shape[1]
    hidden = wg.shape[2]

    def blk(i, be, nu):
        return (jnp.minimum(i, nu[0] - 1), 0)

    def wblk(i, be, nu):
        return (be[jnp.minimum(i, nu[0] - 1)], 0, 0)

    grid_spec = pltpu.PrefetchScalarGridSpec(
        num_scalar_prefetch=2,
        grid=(n_blocks,),
        in_specs=[
            pl.BlockSpec((MOE_BLOCK, d), blk),
            pl.BlockSpec((None, d, hidden), wblk),
            pl.BlockSpec((None, d, hidden), wblk),
            pl.BlockSpec((None, hidden, d), wblk),
        ],
        out_specs=pl.BlockSpec((MOE_BLOCK, d), blk),
    )
    return pl.pallas_call(
        _ffn_kernel,
        grid_spec=grid_spec,
        out_shape=jax.ShapeDtypeStruct(rows.shape, rows.dtype),
        input_output_aliases={2: 0},
        compiler_params=_cparams(1),
        name="ffn",
    )(block_expert, n_used, rows, wg, wu, wd)


def _combine_kernel(piece_ref, x1_ref, route_ref, g_ref, y_ref, o_ref, ybuf, sem, *, tm):
    i = pl.program_id(0)
    nt = pl.num_programs(0)
    slot = i % 2
    kbuf = ybuf.shape[1]
    pieces = kbuf // GROUP_ROWS

    def fetch(step, dst_slot, first, last):
        for f in range(first, last):
            pltpu.make_async_copy(y_ref.at[pl.ds(pl.multiple_of(piece_ref[step * pieces + f], GROUP_ROWS), GROUP_ROWS)],
                                  ybuf.at[dst_slot, pl.ds(f * GROUP_ROWS, GROUP_ROWS)], sem.at[dst_slot]).start()

    def wait_all(the_slot):
        pltpu.make_async_copy(y_ref.at[pl.ds(0, kbuf)], ybuf.at[the_slot], sem.at[the_slot]).wait()

    @pl.when(i == 0)
    def _():
        fetch(0, 0, 0, pieces)

    wait_all(slot)

    route = route_ref[...]
    kio = lax.broadcasted_iota(jnp.int32, (tm, kbuf), 1).astype(F32)
    pw = (jnp.where(kio == route[:, 4:5], route[:, 2:3], 0.0)
          + jnp.where(kio == route[:, 5:6], route[:, 3:4], 0.0)).astype(BF16)
    fetch(jnp.minimum(i + 1, nt - 1), 1 - slot, 0, pieces)
    moe = jnp.dot(pw, ybuf[slot], preferred_element_type=F32)
    o_ref[...] = _rms(x1_ref[...] + moe) * g_ref[...]

    @pl.when(i == nt - 1)
    def _():
        wait_all(1 - slot)


def _combine(piece_rows, x1, route, g, y_rows, tm):
    t, d = x1.shape
    kbuf = _tile_rows(tm)
    grid_spec = pltpu.PrefetchScalarGridSpec(
        num_scalar_prefetch=1,
        grid=(t // tm,),
        in_specs=[
            pl.BlockSpec((tm, d), lambda i, p: (i, 0)),
            pl.BlockSpec((tm, ROUTE_LANES), lambda i, p: (i, 0)),
            pl.BlockSpec((1, d), lambda i, p: (0, 0)),
            pl.BlockSpec(memory_space=pl.ANY),
        ],
        out_specs=pl.BlockSpec((tm, d), lambda i, p: (i, 0)),
        scratch_shapes=[pltpu.VMEM((2, kbuf, d), BF16), pltpu.SemaphoreType.DMA((2,))],
    )
    return pl.pallas_call(
        functools.partial(_combine_kernel, tm=tm),
        grid_spec=grid_spec,
        out_shape=jax.ShapeDtypeStruct((t, d), F32),
        compiler_params=_cparams(1),
        name="combine",
    )(piece_rows, x1, route, g, y_rows)


def _dispatch_plan(cnt_tiles, n_blocks, pieces):
    n_pc = (cnt_tiles + GROUP_ROWS - 1) // GROUP_ROWS
    group = n_pc * GROUP_ROWS
    total = jnp.sum(group, axis=0)
    cap = ((total + MOE_BLOCK - 1) // MOE_BLOCK) * MOE_BLOCK
    pend = jnp.cumsum(cap)
    pstart = pend - cap
    tile_base = pstart[None, :] + jnp.cumsum(group, axis=0) - group
    block_start = jnp.arange(n_blocks, dtype=jnp.int32) * MOE_BLOCK
    block_expert = jnp.minimum(jnp.sum(pend[None, :] <= block_start[:, None], axis=1), N_EXPERTS - 1)
    n_used = jnp.maximum(pend[-1] // MOE_BLOCK, 1).reshape(1)
    fill = jnp.concatenate([pstart + total, (cap - total) // GROUP_ROWS, n_used])

    pc_end = jnp.cumsum(n_pc, axis=1)
    pc_start = pc_end - n_pc
    f = jnp.arange(pieces, dtype=jnp.int32)
    owner = jnp.sum(pc_end[:, :, None] <= f[None, None, :], axis=1)
    is_owner = owner[:, None, :] == jnp.arange(N_EXPERTS, dtype=jnp.int32)[None, :, None]
    piece_rows = jnp.sum(jnp.where(is_owner, (tile_base - pc_start * GROUP_ROWS)[:, :, None] + f * GROUP_ROWS, 0), axis=1)
    dump_rows = n_blocks * MOE_BLOCK + f * GROUP_ROWS
    to_rows = jnp.where(owner < N_EXPERTS, piece_rows, dump_rows[None, :])
    from_rows = jnp.where(owner < N_EXPERTS, piece_rows, (f * GROUP_ROWS)[None, :])
    return (to_rows.reshape(-1).astype(jnp.int32), from_rows.reshape(-1).astype(jnp.int32), fill.astype(jnp.int32),
            block_expert.astype(jnp.int32), n_used.astype(jnp.int32))


def kernel(x, mem, norm_mix_g, w_in, pool_w, pool_scale, lam_q1, lam_k1, lam_q2, lam_k2, subln_g, norm_mem_g,
           w_mem_kv, p_pool, p_diff, p_mem, w_o, norm_ffn_g, w_router_group, b_router_group, w_router_expert,
           b_router_expert, w_expert_gate, w_expert_up, w_expert_down, final_g):
    b, s, d = x.shape
    n_mem = mem.shape[1]
    t = b * s
    layer = 0
    x2 = x.reshape(t, d)
    mem2 = mem.reshape(b * n_mem, d)

    w_in_l = w_in[layer]
    w_proj = w_in_l[:, :PROJ_COLS].astype(BF16)
    w_gates = w_in_l[:, PROJ_COLS:].astype(BF16)
    n_win = len(POOL_WINDOWS)
    eye = jnp.eye(n_win, dtype=F32)
    w_bd = (pool_w[layer][:, :, None, :] * eye[:, None, :, None]).reshape(POOL_WIDTH, POOL_WIDTH).astype(BF16)
    n_route = N_GROUPS + N_EXPERTS
    w_r = jnp.concatenate([w_router_group[layer], w_router_expert[layer].reshape(d, N_EXPERTS),
                           jnp.zeros((d, ROUTE_LANES - n_route), F32)], axis=1).astype(BF16)
    b_r = jnp.concatenate([b_router_group[layer], b_router_expert[layer].reshape(N_EXPERTS),
                           jnp.zeros((ROUTE_LANES - n_route,), F32)]).reshape(1, ROUTE_LANES)

    def row(v):
        return v.reshape(1, -1)

    tm = MOE_TILE
    u_pool, qkv, mq = _proj(x2, row(norm_mix_g[layer]), w_proj, tm=1024)
    kv = _memkv(mem2, row(norm_mem_g[layer]), w_mem_kv[layer].astype(BF16), tm=512)
    diff = _diff_attn(qkv, row(lam_q1[layer]), row(lam_k1[layer]), row(lam_q2[layer]), row(lam_k2[layer]),
                      subln_g[layer].reshape(-1, 1), b, s, tq=256)
    x1, hn, route, cnt = _merge(
        x2, u_pool, mq, kv, diff, row(norm_mix_g[layer]), w_gates, w_bd, row(pool_scale[layer]),
        p_pool[layer].astype(BF16), p_diff[layer].astype(BF16), p_mem[layer].astype(BF16), w_o[layer].astype(BF16),
        row(norm_ffn_g[layer]), w_r, b_r, s, n_mem, tm=tm)

    cnt_tiles = cnt[::8, :N_EXPERTS].astype(jnp.int32)
    n_groups = (t // tm) * N_EXPERTS
    n_blocks = (2 * t + n_groups * (GROUP_ROWS - 1) + N_EXPERTS * (MOE_BLOCK - 1)) // MOE_BLOCK + 1
    to_rows, from_rows, fill, block_expert, n_used = _dispatch_plan(cnt_tiles, n_blocks, _tile_rows(tm) // GROUP_ROWS)
    rows = _dispatch(to_rows, fill, hn, route, n_blocks, tm=tm)
    y_rows = _ffn(block_expert, n_used, rows, w_expert_gate[layer], w_expert_up[layer], w_expert_down[layer])
    out = _combine(from_rows, x1, route, row(final_g), y_rows, tm=tm)
    return out.reshape(b, s, d)
```

```python
import functools

import jax
import jax.numpy as jnp
from jax import lax
from jax.experimental import pallas as pl
from jax.experimental.pallas import tpu as pltpu

F32 = jnp.float32
BF16 = jnp.bfloat16

RMS_EPS = 1e-6
POOL_WINDOWS = (2, 4, 8, 16)
POOL_GROUP = 64
POOL_WIDTH = 256
POOL_HALO = 16
DA_HEADS = 4
DA_HEAD_DIM = 64
DA_V_DIM = 128
SUM_ROWS = 16
DA_QK_WIDTH = 512
DA_WIDTH = 512
MEM_HEADS = 4
MEM_HEAD_DIM = 64
MEM_WIDTH = 256
N_GROUPS = 4
EXPERTS_PER_GROUP = 4
N_EXPERTS = 16
MOE_BLOCK = 1024
MOE_TILE = 256
PROJ_TILE = 1024
MEMKV_TILE = 512
ATTN_TILE = 256
GROUP_ROWS = 16
LAMBDA_INIT = 0.2
QK_SCALE = 0.125
LOG2_E = 1.4426950408889634
ROUTE_LANES = 128

PROJ_COLS = POOL_WIDTH + 3 * DA_QK_WIDTH + MEM_WIDTH

VMEM_LIMIT = 56 * 1024 * 1024


def _cparams(n_axes):
    return pltpu.CompilerParams(dimension_semantics=("arbitrary",) * n_axes, vmem_limit_bytes=VMEM_LIMIT)


def _rms(x):
    return x * lax.rsqrt(jnp.mean(x * x, axis=-1, keepdims=True) + RMS_EPS)


def _proj_kernel(x_ref, g_ref, w_ref, pool_ref, qkv_ref, mq_ref):
    h = (_rms(x_ref[...]) * g_ref[...]).astype(BF16)
    p = jnp.dot(h, w_ref[...], preferred_element_type=F32)
    pool_ref[...] = p[:, :POOL_WIDTH]
    q_end = POOL_WIDTH + DA_QK_WIDTH
    kv_end = q_end + 2 * DA_QK_WIDTH
    qkv_ref[:, :DA_QK_WIDTH] = (p[:, POOL_WIDTH:q_end] * (QK_SCALE * LOG2_E)).astype(BF16)
    qkv_ref[:, DA_QK_WIDTH:] = p[:, q_end:kv_end].astype(BF16)
    mq_ref[...] = (p[:, kv_end:] * QK_SCALE).astype(BF16)


def _proj(x2, g, w, tm):
    t, d = x2.shape
    return pl.pallas_call(
        _proj_kernel,
        grid=(t // tm,),
        in_specs=[
            pl.BlockSpec((tm, d), lambda i: (i, 0)),
            pl.BlockSpec((1, d), lambda i: (0, 0)),
            pl.BlockSpec((d, PROJ_COLS), lambda i: (0, 0)),
        ],
        out_specs=[
            pl.BlockSpec((tm, POOL_WIDTH), lambda i: (i, 0)),
            pl.BlockSpec((tm, 3 * DA_QK_WIDTH), lambda i: (i, 0)),
            pl.BlockSpec((tm, MEM_WIDTH), lambda i: (i, 0)),
        ],
        out_shape=[
            jax.ShapeDtypeStruct((t, POOL_WIDTH), F32),
            jax.ShapeDtypeStruct((t, 3 * DA_QK_WIDTH), BF16),
            jax.ShapeDtypeStruct((t, MEM_WIDTH), BF16),
        ],
        compiler_params=_cparams(1),
        name="proj",
    )(x2, g, w)


def _memkv_kernel(m_ref, g_ref, w_ref, kv_ref):
    h = (_rms(m_ref[...]) * g_ref[...]).astype(BF16)
    kv_ref[...] = jnp.dot(h, w_ref[...], preferred_element_type=F32).astype(BF16)


def _memkv(mem2, g, w, tm):
    t, d = mem2.shape
    n = w.shape[1]
    return pl.pallas_call(
        _memkv_kernel,
        grid=(t // tm,),
        in_specs=[
            pl.BlockSpec((tm, d), lambda i: (i, 0)),
            pl.BlockSpec((1, d), lambda i: (0, 0)),
            pl.BlockSpec((d, n), lambda i: (0, 0)),
        ],
        out_specs=pl.BlockSpec((tm, n), lambda i: (i, 0)),
        out_shape=jax.ShapeDtypeStruct((t, n), BF16),
        compiler_params=_cparams(1),
        name="mem_kv",
    )(mem2, g, w)


def _diff_attn_kernel(q_ref, k_ref, v_ref, lq1_ref, lk1_ref, lq2_ref, lk2_ref, sg_ref, o_ref,
                      vt_ref, s_ref, m_ref, acc_ref, *, tq):
    i = pl.program_id(1)
    n_chunks = vt_ref.shape[1]
    hw = 2 * DA_HEAD_DIM

    @pl.when(i == 0)
    def _():
        for h in range(DA_HEADS):
            for c in range(n_chunks):
                vt_ref[h, c, :DA_V_DIM] = v_ref[c * tq:(c + 1) * tq, h * hw:(h + 1) * hw].astype(F32).T.astype(BF16)
                vt_ref[h, c, DA_V_DIM:] = jnp.ones((SUM_ROWS, tq), BF16)

    lane = lax.broadcasted_iota(jnp.int32, (tq, hw), 1)

    def stacked_queries(tile):
        row0 = pl.multiple_of(tile * tq, tq)
        out = []
        for h in range(DA_HEADS):
            q = q_ref[pl.ds(row0, tq), h * hw:(h + 1) * hw]
            zero = jnp.zeros_like(q)
            out.append(jnp.concatenate([jnp.where(lane < DA_HEAD_DIM, q, zero), jnp.where(lane >= DA_HEAD_DIM, q, zero)],
                                       axis=0))
        return out

    q2 = stacked_queries(i)

    m_ref[...] = jnp.full(m_ref.shape, -jnp.inf, F32)
    acc_ref[...] = jnp.zeros(acc_ref.shape, F32)

    heads = range(DA_HEADS)

    def score_chunk(j, slot, queries=q2):
        start = pl.multiple_of(j * tq, tq)
        for h in heads:
            k = k_ref[pl.ds(start, tq), h * hw:(h + 1) * hw]
            s_ref[slot, h] = lax.dot_general(k, queries[h], (((1,), (1,)), ((), ())), preferred_element_type=F32)

    def softmax_chunk(slot, masked):
        probs, alphas = [], []
        for h in heads:
            s = s_ref[slot, h]
            if masked:
                key = lax.broadcasted_iota(jnp.int32, s.shape, 0)
                qry = lax.broadcasted_iota(jnp.int32, s.shape, 1)
                qry = jnp.where(qry >= tq, qry - tq, qry)
                s = jnp.where(key <= qry, s, -jnp.inf)
            m = m_ref[h]
            m_new = jnp.maximum(m, jnp.max(s, axis=0, keepdims=True))
            alpha = jnp.exp2(m - m_new)
            p = jnp.exp2(s - m_new)
            m_ref[h] = m_new
            alphas.append(alpha)
            probs.append(p.astype(BF16))
        return probs, alphas

    def value_chunk(j, probs, alphas):
        for h in heads:
            acc_ref[h] = alphas[h] * acc_ref[h] + jnp.dot(vt_ref[h, j], probs[h], preferred_element_type=F32)

    def step(j, slot, masked, prefetch):
        probs, alphas = softmax_chunk(slot, masked)
        if prefetch:
            score_chunk(j + 1, 1 - slot)
        value_chunk(j, probs, alphas)

    @pl.when(i == 0)
    def _():
        score_chunk(0, 0)

    def body(jj, carry):
        step(2 * jj, 0, False, True)
        step(2 * jj + 1, 1, False, True)
        return carry

    lax.fori_loop(0, i // 2, body, 0)

    @pl.when(i % 2 == 1)
    def _():
        step(i - 1, 0, False, True)
        step(i, 1, True, False)

    @pl.when(i % 2 == 0)
    def _():
        step(i, 0, True, False)

    score_chunk(0, 0, stacked_queries(jnp.minimum(i + 1, n_chunks - 1)))

    lam = (jnp.exp(jnp.sum(lq1_ref[...] * lk1_ref[...], axis=-1, keepdims=True))
           - jnp.exp(jnp.sum(lq2_ref[...] * lk2_ref[...], axis=-1, keepdims=True)) + LAMBDA_INIT)
    for h in range(DA_HEADS):
        o = acc_ref[h, :DA_V_DIM] / acc_ref[h, DA_V_DIM:DA_V_DIM + 1]
        o = o[:, :tq] - lam * o[:, tq:]
        o = o * lax.rsqrt(jnp.mean(o * o, axis=0, keepdims=True) + RMS_EPS) * sg_ref[...] * (1.0 - LAMBDA_INIT)
        o_ref[:, h * DA_V_DIM:(h + 1) * DA_V_DIM] = o.T.astype(o_ref.dtype)


def _diff_attn(qkv, lq1, lk1, lq2, lk2, subln_g_col, batch, seq, tq):
    t = qkv.shape[0]
    nq = seq // tq
    lam_spec = pl.BlockSpec((1, DA_HEAD_DIM), lambda b, i: (0, 0))
    return pl.pallas_call(
        functools.partial(_diff_attn_kernel, tq=tq),
        grid=(batch, nq),
        in_specs=[
            pl.BlockSpec((seq, DA_QK_WIDTH), lambda b, i: (b, 0)),
            pl.BlockSpec((seq, DA_QK_WIDTH), lambda b, i: (b, 1)),
            pl.BlockSpec((seq, DA_WIDTH), lambda b, i: (b, 2)),
            lam_spec, lam_spec, lam_spec, lam_spec,
            pl.BlockSpec((DA_V_DIM, 1), lambda b, i: (0, 0)),
        ],
        out_specs=pl.BlockSpec((tq, DA_WIDTH), lambda b, i: (b * nq + i, 0)),
        out_shape=jax.ShapeDtypeStruct((t, DA_WIDTH), BF16),
        scratch_shapes=[
            pltpu.VMEM((DA_HEADS, nq, DA_V_DIM + SUM_ROWS, tq), BF16),
            pltpu.VMEM((2, DA_HEADS, tq, 2 * tq), F32),
            pltpu.VMEM((DA_HEADS, 1, 2 * tq), F32),
            pltpu.VMEM((DA_HEADS, DA_V_DIM + SUM_ROWS, 2 * tq), F32),
        ],
        compiler_params=_cparams(2),
        name="diff_attn",
    )(qkv, qkv, qkv, lq1, lk1, lq2, lk2, subln_g_col)


def _pool_mixer(u, halo, seq_pos0):
    tm = u.shape[0]
    ext = jnp.concatenate([halo, u], axis=0)
    n = ext.shape[0]
    sums = []
    acc = ext
    for shift in (1, 2, 4, 8):
        acc = acc + pltpu.roll(acc, shift, 0)
        sums.append(acc[POOL_HALO:n])
    lane = lax.broadcasted_iota(jnp.int32, (tm, POOL_WIDTH), 1)
    pos = seq_pos0 + lax.broadcasted_iota(jnp.int32, (tm, POOL_WIDTH), 0)
    win_sum = sums[3]
    win = jnp.full((tm, POOL_WIDTH), POOL_WINDOWS[3], jnp.int32)
    for gi in (2, 1, 0):
        in_group = lane < (gi + 1) * POOL_GROUP
        win_sum = jnp.where(in_group, sums[gi], win_sum)
        win = jnp.where(in_group, POOL_WINDOWS[gi], win)
    cnt = jnp.minimum(pos + 1, win).astype(F32)
    return win_sum / cnt - u


def _mem_probs(mq, mk):
    head = lax.broadcasted_iota(jnp.int32, mq.shape, 1) // MEM_HEAD_DIM
    zero = jnp.zeros_like(mq)
    q4 = jnp.concatenate([jnp.where(head == h, mq, zero) for h in range(MEM_HEADS)], axis=0)
    s = lax.dot_general(q4, mk, (((1,), (1,)), ((), ())), preferred_element_type=F32)
    p = jnp.exp(s - jnp.max(s, axis=-1, keepdims=True))
    return p.astype(BF16), jnp.sum(p, axis=-1, keepdims=True)


def _mem_output(p, l, mv):
    tm = p.shape[0] // MEM_HEADS
    head = lax.broadcasted_iota(jnp.int32, (tm, MEM_WIDTH), 1) // MEM_HEAD_DIM
    o4 = jnp.dot(p, mv, preferred_element_type=F32) / l
    out = jnp.zeros((tm, MEM_WIDTH), F32)
    for h in range(MEM_HEADS):
        out = jnp.where(head == h, o4[h * tm:(h + 1) * tm], out)
    return out


def _router_select(logits):
    lane = lax.broadcasted_iota(jnp.int32, logits.shape, 1).astype(F32)
    neg = -jnp.inf

    def first_argmax(vals, vmax):
        return jnp.min(jnp.where(vals == vmax, lane, float(ROUTE_LANES)), axis=-1, keepdims=True)

    lg = jnp.where(lane < N_GROUPS, logits, neg)
    mg = jnp.max(lg, axis=-1, keepdims=True)
    g_p = 1.0 / jnp.sum(jnp.exp(lg - mg), axis=-1, keepdims=True)
    g_idx = first_argmax(lg, mg)
    lo = N_GROUPS + EXPERTS_PER_GROUP * g_idx
    le = jnp.where((lane >= lo) & (lane < lo + EXPERTS_PER_GROUP), logits, neg)
    m1 = jnp.max(le, axis=-1, keepdims=True)
    i1 = first_argmax(le, m1)
    le2 = jnp.where(lane == i1, neg, le)
    m2 = jnp.max(le2, axis=-1, keepdims=True)
    i2 = first_argmax(le2, m2)
    r = jnp.exp(m2 - m1)
    w1 = g_p / (1.0 + r)
    w2 = w1 * r
    return i1 - N_GROUPS, i2 - N_GROUPS, w1, w2


def _router_place(e1, e2, w1, w2):
    tm = e1.shape[0]
    lane = lax.broadcasted_iota(jnp.int32, (tm, ROUTE_LANES), 1).astype(F32)
    oh1 = lane == e1
    oh2 = lane == e2
    oh = jnp.where(oh1 | oh2, 1.0, 0.0)
    earlier = (lax.broadcasted_iota(jnp.int32, (tm, tm), 0) > lax.broadcasted_iota(jnp.int32, (tm, tm), 1))
    rank = jnp.dot(earlier.astype(BF16), oh.astype(BF16), preferred_element_type=F32)
    cnt = jnp.sum(oh, axis=0, keepdims=True)
    group = jnp.floor((cnt + (GROUP_ROWS - 1)) * (1.0 / GROUP_ROWS)) * GROUP_ROWS
    lower = (lax.broadcasted_iota(jnp.int32, (ROUTE_LANES, ROUTE_LANES), 0)
             < lax.broadcasted_iota(jnp.int32, (ROUTE_LANES, ROUTE_LANES), 1)).astype(BF16)
    offs = jnp.dot(jnp.broadcast_to(group, (8, ROUTE_LANES)).astype(BF16), lower,
                   preferred_element_type=F32)
    pos = offs[0:1] + rank

    def pick(onehot):
        return jnp.sum(jnp.where(onehot, pos, 0.0), axis=-1, keepdims=True)

    cols = (e1, e2, w1, w2, pick(oh1), pick(oh2))
    out = jnp.zeros((tm, ROUTE_LANES), F32)
    for n, col in enumerate(cols):
        out = jnp.where(lane == n, col, out)
    return out, cnt


def _merge_kernel(x_ref, u_ref, halo_ref, mq_ref, kv_ref, diff_ref,
                  gmix_ref, wg_ref, wbd_ref, pscale_ref, pp_ref, pd_ref, pm_ref, wo_ref,
                  gffn_ref, wr_ref, br_ref,
                  x1_ref, hn_ref, route_ref, cnt_ref, x1_prev_ref, *, tm, seq):
    i = pl.program_id(0)
    d = x_ref.shape[1]

    @pl.when(i == 0)
    def _():
        x1_prev_ref[...] = jnp.zeros(x1_prev_ref.shape, F32)

    hn = (_rms(x1_prev_ref[...]) * gffn_ref[...]).astype(BF16)
    hn_ref[...] = hn
    logits = jnp.dot(hn, wr_ref[...], preferred_element_type=F32) + br_ref[...]
    selection = _router_select(logits)

    x = x_ref[...]
    h = (_rms(x) * gmix_ref[...]).astype(BF16)

    seq_pos0 = (jnp.minimum(i, pl.num_programs(0) - 2) * tm) % seq
    halo = jnp.where(seq_pos0 == 0, 0.0, halo_ref[...])
    pooled = _pool_mixer(u_ref[...], halo, seq_pos0).astype(BF16)
    pool_out = jnp.dot(pooled, wbd_ref[...], preferred_element_type=F32) * pscale_ref[...]
    mem_p, mem_l = _mem_probs(mq_ref[...], kv_ref[:, :MEM_WIDTH])

    def branch(n, b_in, p_ref):
        gate = jax.nn.sigmoid(jnp.dot(h, wg_ref[:, n * d:(n + 1) * d], preferred_element_type=F32))
        return gate * jnp.dot(b_in, p_ref[...], preferred_element_type=F32)

    merged = branch(0, pool_out.astype(BF16), pp_ref) + branch(1, diff_ref[...], pd_ref)
    mem_out = _mem_output(mem_p, mem_l, kv_ref[:, MEM_WIDTH:])
    merged = merged + branch(2, mem_out.astype(BF16), pm_ref)
    route, cnt = _router_place(*selection)
    route_ref[...] = route
    cnt_ref[...] = jnp.broadcast_to(cnt, cnt_ref.shape)

    x1 = x + jnp.dot(merged.astype(BF16), wo_ref[...], preferred_element_type=F32)
    x1_ref[...] = x1
    x1_prev_ref[...] = x1


def _merge(x2, u_pool, mq, kv, diff, gmix, wg, wbd, pscale, pp, pd, pm, wo, gffn, wr, br, seq, n_mem, tm):
    t, d = x2.shape
    nt = t // tm
    tiles_per_seq = seq // tm
    halo_blocks = tm // POOL_HALO

    def tile(i):
        return (jnp.minimum(i, nt - 1), 0)

    def routed(i):
        return (jnp.maximum(i - 1, 0), 0)

    def const(i):
        return (0, 0)

    def wspec(a):
        return pl.BlockSpec(a.shape, const)

    return pl.pallas_call(
        functools.partial(_merge_kernel, tm=tm, seq=seq),
        grid=(nt + 1,),
        in_specs=[
            pl.BlockSpec((tm, d), tile),
            pl.BlockSpec((tm, POOL_WIDTH), tile),
            pl.BlockSpec((POOL_HALO, POOL_WIDTH), lambda i: (jnp.maximum(jnp.minimum(i, nt - 1) * halo_blocks - 1, 0), 0)),
            pl.BlockSpec((tm, MEM_WIDTH), tile),
            pl.BlockSpec((n_mem, 2 * MEM_WIDTH), lambda i: (jnp.minimum(i, nt - 1) // tiles_per_seq, 0)),
            pl.BlockSpec((tm, DA_WIDTH), tile),
            wspec(gmix), wspec(wg), wspec(wbd), wspec(pscale), wspec(pp), wspec(pd), wspec(pm), wspec(wo),
            wspec(gffn), wspec(wr), wspec(br),
        ],
        out_specs=[
            pl.BlockSpec((tm, d), tile),
            pl.BlockSpec((tm, d), routed),
            pl.BlockSpec((tm, ROUTE_LANES), routed),
            pl.BlockSpec((8, ROUTE_LANES), routed),
        ],
        out_shape=[
            jax.ShapeDtypeStruct((t, d), F32),
            jax.ShapeDtypeStruct((t, d), BF16),
            jax.ShapeDtypeStruct((t, ROUTE_LANES), F32),
            jax.ShapeDtypeStruct((nt * 8, ROUTE_LANES), F32),
        ],
        scratch_shapes=[pltpu.VMEM((tm, d), F32)],
        compiler_params=_cparams(1),
        name="merge",
    )(x2, u_pool, u_pool, mq, kv, diff, gmix, wg, wbd, pscale, pp, pd, pm, wo, gffn, wr, br)


def _dispatch_kernel(piece_ref, fill_ref, hn_ref, route_ref, rows_ref, rbuf, zbuf, sem, zsem, *, tm, n_blocks):
    i = pl.program_id(0)
    last = pl.num_programs(0) - 1
    k = rbuf.shape[1]
    pieces = k // GROUP_ROWS
    slot = i % 2
    prev = jnp.maximum(i - 1, 0)

    def wait_all():
        pltpu.make_async_copy(rbuf.at[0], rows_ref.at[pl.ds(0, k)], sem).wait()

    @pl.when(i == 0)
    def _():
        rbuf[1] = jnp.zeros(rbuf.shape[1:], rbuf.dtype)

    @pl.when(i > 0)
    def _():
        wait_all()

    route = route_ref[...]
    kio = lax.broadcasted_iota(jnp.int32, (tm, k), 1).astype(F32)
    onehot = jnp.where((kio == route[:, 4:5]) | (kio == route[:, 5:6]), 1.0, 0.0).astype(BF16)
    for f in range(pieces):
        pltpu.make_async_copy(
            rbuf.at[1 - slot, pl.ds(f * GROUP_ROWS, GROUP_ROWS)],
            rows_ref.at[pl.ds(pl.multiple_of(piece_ref[prev * pieces + f], GROUP_ROWS), GROUP_ROWS)], sem).start()
    rbuf[slot] = lax.dot_general(onehot, hn_ref[...], (((0,), (0,)), ((), ())),
                                 preferred_element_type=F32).astype(BF16)

    @pl.when(i == last)
    def _():
        wait_all()
        zbuf[...] = jnp.zeros(zbuf.shape, zbuf.dtype)

        def tail_copy(dst_row):
            return pltpu.make_async_copy(zbuf.at[pl.ds(0, GROUP_ROWS)], rows_ref.at[pl.ds(dst_row, GROUP_ROWS)], zsem.at[0])

        def block_copy(blk):
            return pltpu.make_async_copy(zbuf, rows_ref.at[pl.ds(pl.multiple_of(blk * MOE_BLOCK, MOE_BLOCK), MOE_BLOCK)],
                                         zsem.at[1])

        n_used = fill_ref[2 * N_EXPERTS]
        for e in range(N_EXPERTS):
            start = fill_ref[e]

            def body(c, carry, start=start):
                tail_copy(pl.multiple_of(start + c * GROUP_ROWS, GROUP_ROWS)).start()
                return carry

            lax.fori_loop(0, fill_ref[N_EXPERTS + e], body, 0)

        def blk_body(b, carry):
            block_copy(b).start()
            return carry

        lax.fori_loop(n_used, n_blocks, blk_body, 0)

        for e in range(N_EXPERTS):
            def wbody(c, carry):
                tail_copy(0).wait()
                return carry

            lax.fori_loop(0, fill_ref[N_EXPERTS + e], wbody, 0)

        def blk_wait(b, carry):
            block_copy(0).wait()
            return carry

        lax.fori_loop(n_used, n_blocks, blk_wait, 0)

        dump_copy = pltpu.make_async_copy(zbuf.at[pl.ds(0, k)], rows_ref.at[pl.ds(n_blocks * MOE_BLOCK, k)], zsem.at[2])
        dump_copy.start()
        dump_copy.wait()


def _tile_rows(tm):
    return 2 * tm + N_EXPERTS * GROUP_ROWS


def _dispatch(piece_rows, fill, hn, route, n_blocks, tm):
    t, d = hn.shape
    nt = t // tm
    k = _tile_rows(tm)
    assert MOE_BLOCK >= k

    def tile(i, p, f):
        return (jnp.minimum(i, nt - 1), 0)

    grid_spec = pltpu.PrefetchScalarGridSpec(
        num_scalar_prefetch=2,
        grid=(nt + 1,),
        in_specs=[pl.BlockSpec((tm, d), tile), pl.BlockSpec((tm, ROUTE_LANES), tile)],
        out_specs=pl.BlockSpec(memory_space=pl.ANY),
        scratch_shapes=[pltpu.VMEM((2, k, d), BF16), pltpu.VMEM((MOE_BLOCK, d), BF16),
                        pltpu.SemaphoreType.DMA(()), pltpu.SemaphoreType.DMA((3,))],
    )
    return pl.pallas_call(
        functools.partial(_dispatch_kernel, tm=tm, n_blocks=n_blocks),
        grid_spec=grid_spec,
        out_shape=jax.ShapeDtypeStruct((n_blocks * MOE_BLOCK + k, d), BF16),
        compiler_params=_cparams(1),
        name="dispatch",
    )(piece_rows, fill, hn, route)


def _ffn_kernel(be_ref, nused_ref, rows_ref, wg_ref, wu_ref, wd_ref, y_ref):
    in_use = pl.program_id(0) < nused_ref[0]

    @pl.when(in_use)
    def _():
        xb = rows_ref[...]
        hg = jnp.dot(xb, wg_ref[...].astype(BF16), preferred_element_type=F32)
        hu = jnp.dot(xb, wu_ref[...].astype(BF16), preferred_element_type=F32)
        hid = (hg * jax.nn.sigmoid(hg) * hu).astype(BF16)
        y_ref[...] = jnp.dot(hid, wd_ref[...].astype(BF16), preferred_element_type=F32).astype(y_ref.dtype)


def _ffn(block_expert, n_used, rows, wg, wu, wd):
    n_blocks = block_expert.shape[0]
    d = rows.shape[1]
    hidden = wg.shape[2]

    def blk(i, be, nu):
        return (jnp.minimum(i, nu[0] - 1), 0)

    def wblk(i, be, nu):
        return (be[jnp.minimum(i, nu[0] - 1)], 0, 0)

    grid_spec = pltpu.PrefetchScalarGridSpec(
        num_scalar_prefetch=2,
        grid=(n_blocks,),
        in_specs=[
            pl.BlockSpec((MOE_BLOCK, d), blk),
            pl.BlockSpec((None, d, hidden), wblk),
            pl.BlockSpec((None, d, hidden), wblk),
            pl.BlockSpec((None, hidden, d), wblk),
        ],
        out_specs=pl.BlockSpec((MOE_BLOCK, d), blk),
    )
    return pl.pallas_call(
        _ffn_kernel,
        grid_spec=grid_spec,
        out_shape=jax.ShapeDtypeStruct(rows.shape, rows.dtype),
        input_output_aliases={2: 0},
        compiler_params=_cparams(1),
        name="ffn",
    )(block_expert, n_used, rows, wg, wu, wd)


def _combine_kernel(piece_ref, x1_ref, route_ref, g_ref, y_ref, o_ref, ybuf, sem, *, tm):
    i = pl.program_id(0)
    nt = pl.num_programs(0)
    slot = i % 2
    kbuf = ybuf.shape[1]
    pieces = kbuf // GROUP_ROWS

    def fetch(step, dst_slot, first, last):
        for f in range(first, last):
            pltpu.make_async_copy(y_ref.at[pl.ds(pl.multiple_of(piece_ref[step * pieces + f], GROUP_ROWS), GROUP_ROWS)],
                                  ybuf.at[dst_slot, pl.ds(f * GROUP_ROWS, GROUP_ROWS)], sem.at[dst_slot]).start()

    def wait_all(the_slot):
        pltpu.make_async_copy(y_ref.at[pl.ds(0, kbuf)], ybuf.at[the_slot], sem.at[the_slot]).wait()

    @pl.when(i == 0)
    def _():
        fetch(0, 0, 0, pieces)

    wait_all(slot)

    route = route_ref[...]
    kio = lax.broadcasted_iota(jnp.int32, (tm, kbuf), 1).astype(F32)
    pw = (jnp.where(kio == route[:, 4:5], route[:, 2:3], 0.0)
          + jnp.where(kio == route[:, 5:6], route[:, 3:4], 0.0)).astype(BF16)
    fetch(jnp.minimum(i + 1, nt - 1), 1 - slot, 0, pieces)
    moe = jnp.dot(pw, ybuf[slot], preferred_element_type=F32)
    o_ref[...] = _rms(x1_ref[...] + moe) * g_ref[...]

    @pl.when(i == nt - 1)
    def _():
        wait_all(1 - slot)


def _combine(piece_rows, x1, route, g, y_rows, tm):
    t, d = x1.shape
    kbuf = _tile_rows(tm)
    grid_spec = pltpu.PrefetchScalarGridSpec(
        num_scalar_prefetch=1,
        grid=(t // tm,),
        in_specs=[
            pl.BlockSpec((tm, d), lambda i, p: (i, 0)),
            pl.BlockSpec((tm, ROUTE_LANES), lambda i, p: (i, 0)),
            pl.BlockSpec((1, d), lambda i, p: (0, 0)),
            pl.BlockSpec(memory_space=pl.ANY),
        ],
        out_specs=pl.BlockSpec((tm, d), lambda i, p: (i, 0)),
        scratch_shapes=[pltpu.VMEM((2, kbuf, d), BF16), pltpu.SemaphoreType.DMA((2,))],
    )
    return pl.pallas_call(
        functools.partial(_combine_kernel, tm=tm),
        grid_spec=grid_spec,
        out_shape=jax.ShapeDtypeStruct((t, d), F32),
        compiler_params=_cparams(1),
        name="combine",
    )(piece_rows, x1, route, g, y_rows)


def _dispatch_plan(cnt_tiles, n_blocks, pieces):
    n_pc = (cnt_tiles + GROUP_ROWS - 1) // GROUP_ROWS
    group = n_pc * GROUP_ROWS
    total = jnp.sum(group, axis=0)
    cap = ((total + MOE_BLOCK - 1) // MOE_BLOCK) * MOE_BLOCK
    pend = jnp.cumsum(cap)
    pstart = pend - cap
    tile_base = pstart[None, :] + jnp.cumsum(group, axis=0) - group
    block_start = jnp.arange(n_blocks, dtype=jnp.int32) * MOE_BLOCK
    block_expert = jnp.minimum(jnp.sum(pend[None, :] <= block_start[:, None], axis=1), N_EXPERTS - 1)
    n_used = jnp.maximum(pend[-1] // MOE_BLOCK, 1).reshape(1)
    fill = jnp.concatenate([pstart + total, (cap - total) // GROUP_ROWS, n_used])

    pc_end = jnp.cumsum(n_pc, axis=1)
    pc_start = pc_end - n_pc
    f = jnp.arange(pieces, dtype=jnp.int32)
    owner = jnp.sum(pc_end[:, :, None] <= f[None, None, :], axis=1)
    is_owner = owner[:, None, :] == jnp.arange(N_EXPERTS, dtype=jnp.int32)[None, :, None]
    piece_rows = jnp.sum(jnp.where(is_owner, (tile_base - pc_start * GROUP_ROWS)[:, :, None] + f * GROUP_ROWS, 0), axis=1)
    dump_rows = n_blocks * MOE_BLOCK + f * GROUP_ROWS
    to_rows = jnp.where(owner < N_EXPERTS, piece_rows, dump_rows[None, :])
    from_rows = jnp.where(owner < N_EXPERTS, piece_rows, (f * GROUP_ROWS)[None, :])
    return (to_rows.reshape(-1).astype(jnp.int32), from_rows.reshape(-1).astype(jnp.int32), fill.astype(jnp.int32),
            block_expert.astype(jnp.int32), n_used.astype(jnp.int32))


def kernel(x, mem, norm_mix_g, w_in, pool_w, pool_scale, lam_q1, lam_k1, lam_q2, lam_k2, subln_g, norm_mem_g,
           w_mem_kv, p_pool, p_diff, p_mem, w_o, norm_ffn_g, w_router_group, b_router_group, w_router_expert,
           b_router_expert, w_expert_gate, w_expert_up, w_expert_down, final_g):
    b, s, d = x.shape
    n_mem = mem.shape[1]
    t = b * s
    layer = 0
    x2 = x.reshape(t, d)
    mem2 = mem.reshape(b * n_mem, d)

    w_in_l = w_in[layer]
    w_proj = w_in_l[:, :PROJ_COLS].astype(BF16)
    w_gates = w_in_l[:, PROJ_COLS:].astype(BF16)
    n_win = len(POOL_WINDOWS)
    eye = jnp.eye(n_win, dtype=F32)
    w_bd = (pool_w[layer][:, :, None, :] * eye[:, None, :, None]).reshape(POOL_WIDTH, POOL_WIDTH).astype(BF16)
    n_route = N_GROUPS + N_EXPERTS
    w_r = jnp.concatenate([w_router_group[layer], w_router_expert[layer].reshape(d, N_EXPERTS),
                           jnp.zeros((d, ROUTE_LANES - n_route), F32)], axis=1).astype(BF16)
    b_r = jnp.concatenate([b_router_group[layer], b_router_expert[layer].reshape(N_EXPERTS),
                           jnp.zeros((ROUTE_LANES - n_route,), F32)]).reshape(1, ROUTE_LANES)

    def row(v):
        return v.reshape(1, -1)

    tm = MOE_TILE
    assert s % ATTN_TILE == 0 and s % tm == 0 and t % PROJ_TILE == 0 and (b * n_mem) % MEMKV_TILE == 0
    u_pool, qkv, mq = _proj(x2, row(norm_mix_g[layer]), w_proj, tm=PROJ_TILE)
    kv = _memkv(mem2, row(norm_mem_g[layer]), w_mem_kv[layer].astype(BF16), tm=MEMKV_TILE)
    diff = _diff_attn(qkv, row(lam_q1[layer]), row(lam_k1[layer]), row(lam_q2[layer]), row(lam_k2[layer]),
                      subln_g[layer].reshape(-1, 1), b, s, tq=ATTN_TILE)
    x1, hn, route, cnt = _merge(
        x2, u_pool, mq, kv, diff, row(norm_mix_g[layer]), w_gates, w_bd, row(pool_scale[layer]),
        p_pool[layer].astype(BF16), p_diff[layer].astype(BF16), p_mem[layer].astype(BF16), w_o[layer].astype(BF16),
        row(norm_ffn_g[layer]), w_r, b_r, s, n_mem, tm=tm)

    cnt_tiles = cnt[::8, :N_EXPERTS].astype(jnp.int32)
    n_groups = (t // tm) * N_EXPERTS
    n_blocks = (2 * t + n_groups * (GROUP_ROWS - 1) + N_EXPERTS * (MOE_BLOCK - 1)) // MOE_BLOCK + 1
    to_rows, from_rows, fill, block_expert, n_used = _dispatch_plan(cnt_tiles, n_blocks, _tile_rows(tm) // GROUP_ROWS)
    rows = _dispatch(to_rows, fill, hn, route, n_blocks, tm=tm)
    y_rows = _ffn(block_expert, n_used, rows, w_expert_gate[layer], w_expert_up[layer], w_expert_down[layer])
    out = _combine(from_rows, x1, route, row(final_g), y_rows, tm=tm)
    return out.reshape(b, s, d)
```

```python
import functools

import jax
import jax.numpy as jnp
from jax import lax
from jax.experimental import pallas as pl
from jax.experimental.pallas import tpu as pltpu

F32 = jnp.float32
BF16 = jnp.bfloat16

RMS_EPS = 1e-6
POOL_WINDOWS = (2, 4, 8, 16)
POOL_GROUP = 64
POOL_WIDTH = 256
POOL_HALO = 16
DA_HEADS = 4
DA_HEAD_DIM = 64
DA_V_DIM = 128
SUM_ROWS = 16
DA_QK_WIDTH = 512
DA_WIDTH = 512
MEM_HEADS = 4
MEM_HEAD_DIM = 64
MEM_WIDTH = 256
N_GROUPS = 4
EXPERTS_PER_GROUP = 4
N_EXPERTS = 16
MOE_BLOCK = 1024
MOE_TILE = 256
PROJ_TILE = 1024
MEMKV_TILE = 512
ATTN_TILE = 256
GROUP_ROWS = 16
LAMBDA_INIT = 0.2
QK_SCALE = 0.125
LOG2_E = 1.4426950408889634
ROUTE_LANES = 128

PROJ_COLS = POOL_WIDTH + 3 * DA_QK_WIDTH + MEM_WIDTH

VMEM_LIMIT = 56 * 1024 * 1024


def _cparams(n_axes):
    return pltpu.CompilerParams(dimension_semantics=("arbitrary",) * n_axes, vmem_limit_bytes=VMEM_LIMIT)


def _rms(x):
    return x * lax.rsqrt(jnp.mean(x * x, axis=-1, keepdims=True) + RMS_EPS)


def _proj_kernel(x_ref, g_ref, w_ref, pool_ref, qkv_ref, mq_ref):
    h = (_rms(x_ref[...]) * g_ref[...]).astype(BF16)
    p = jnp.dot(h, w_ref[...], preferred_element_type=F32)
    pool_ref[...] = p[:, :POOL_WIDTH]
    q_end = POOL_WIDTH + DA_QK_WIDTH
    kv_end = q_end + 2 * DA_QK_WIDTH
    qkv_ref[:, :DA_QK_WIDTH] = (p[:, POOL_WIDTH:q_end] * (QK_SCALE * LOG2_E)).astype(BF16)
    qkv_ref[:, DA_QK_WIDTH:] = p[:, q_end:kv_end].astype(BF16)
    mq_ref[...] = (p[:, kv_end:] * QK_SCALE).astype(BF16)


def _proj(x2, g, w, tm):
    t, d = x2.shape
    return pl.pallas_call(
        _proj_kernel,
        grid=(t // tm,),
        in_specs=[
            pl.BlockSpec((tm, d), lambda i: (i, 0)),
            pl.BlockSpec((1, d), lambda i: (0, 0)),
            pl.BlockSpec((d, PROJ_COLS), lambda i: (0, 0)),
        ],
        out_specs=[
            pl.BlockSpec((tm, POOL_WIDTH), lambda i: (i, 0)),
            pl.BlockSpec((tm, 3 * DA_QK_WIDTH), lambda i: (i, 0)),
            pl.BlockSpec((tm, MEM_WIDTH), lambda i: (i, 0)),
        ],
        out_shape=[
            jax.ShapeDtypeStruct((t, POOL_WIDTH), F32),
            jax.ShapeDtypeStruct((t, 3 * DA_QK_WIDTH), BF16),
            jax.ShapeDtypeStruct((t, MEM_WIDTH), BF16),
        ],
        compiler_params=_cparams(1),
        name="proj",
    )(x2, g, w)


def _memkv_kernel(m_ref, g_ref, w_ref, kv_ref):
    h = (_rms(m_ref[...]) * g_ref[...]).astype(BF16)
    kv_ref[...] = jnp.dot(h, w_ref[...], preferred_element_type=F32).astype(BF16)


def _memkv(mem2, g, w, tm):
    t, d = mem2.shape
    n = w.shape[1]
    return pl.pallas_call(
        _memkv_kernel,
        grid=(t // tm,),
        in_specs=[
            pl.BlockSpec((tm, d), lambda i: (i, 0)),
            pl.BlockSpec((1, d), lambda i: (0, 0)),
            pl.BlockSpec((d, n), lambda i: (0, 0)),
        ],
        out_specs=pl.BlockSpec((tm, n), lambda i: (i, 0)),
        out_shape=jax.ShapeDtypeStruct((t, n), BF16),
        compiler_params=_cparams(1),
        name="mem_kv",
    )(mem2, g, w)


def _diff_attn_kernel(q_ref, k_ref, v_ref, lq1_ref, lk1_ref, lq2_ref, lk2_ref, sg_ref, o_ref,
                      vt_ref, s_ref, m_ref, acc_ref, *, tq):
    i = pl.program_id(1)
    n_chunks = vt_ref.shape[1]
    hw = 2 * DA_HEAD_DIM

    @pl.when(i == 0)
    def _():
        for h in range(DA_HEADS):
            for c in range(n_chunks):
                vt_ref[h, c, :DA_V_DIM] = v_ref[c * tq:(c + 1) * tq, h * hw:(h + 1) * hw].astype(F32).T.astype(BF16)
                vt_ref[h, c, DA_V_DIM:] = jnp.ones((SUM_ROWS, tq), BF16)

    lane = lax.broadcasted_iota(jnp.int32, (tq, hw), 1)

    def stacked_queries(tile):
        row0 = pl.multiple_of(tile * tq, tq)
        out = []
        for h in range(DA_HEADS):
            q = q_ref[pl.ds(row0, tq), h * hw:(h + 1) * hw]
            zero = jnp.zeros_like(q)
            out.append(jnp.concatenate([jnp.where(lane < DA_HEAD_DIM, q, zero), jnp.where(lane >= DA_HEAD_DIM, q, zero)],
                                       axis=0))
        return out

    q2 = stacked_queries(i)

    m_ref[...] = jnp.full(m_ref.shape, -jnp.inf, F32)
    acc_ref[...] = jnp.zeros(acc_ref.shape, F32)

    heads = range(DA_HEADS)

    def score_chunk(j, slot, queries=q2):
        start = pl.multiple_of(j * tq, tq)
        for h in heads:
            k = k_ref[pl.ds(start, tq), h * hw:(h + 1) * hw]
            s_ref[slot, h] = lax.dot_general(k, queries[h], (((1,), (1,)), ((), ())), preferred_element_type=F32)

    def softmax_chunk(slot, masked):
        probs, alphas = [], []
        for h in heads:
            s = s_ref[slot, h]
            if masked:
                key = lax.broadcasted_iota(jnp.int32, s.shape, 0)
                qry = lax.broadcasted_iota(jnp.int32, s.shape, 1)
                qry = jnp.where(qry >= tq, qry - tq, qry)
                s = jnp.where(key <= qry, s, -jnp.inf)
            m = m_ref[h]
            m_new = jnp.maximum(m, jnp.max(s, axis=0, keepdims=True))
            alpha = jnp.exp2(m - m_new)
            p = jnp.exp2(s - m_new)
            m_ref[h] = m_new
            alphas.append(alpha)
            probs.append(p.astype(BF16))
        return probs, alphas

    def value_chunk(j, probs, alphas):
        for h in heads:
            acc_ref[h] = alphas[h] * acc_ref[h] + jnp.dot(vt_ref[h, j], probs[h], preferred_element_type=F32)

    def step(j, slot, masked, prefetch):
        probs, alphas = softmax_chunk(slot, masked)
        if prefetch:
            score_chunk(j + 1, 1 - slot)
        value_chunk(j, probs, alphas)

    @pl.when(i == 0)
    def _():
        score_chunk(0, 0)

    def body(jj, carry):
        step(2 * jj, 0, False, True)
        step(2 * jj + 1, 1, False, True)
        return carry

    lax.fori_loop(0, i // 2, body, 0)

    @pl.when(i % 2 == 1)
    def _():
        step(i - 1, 0, False, True)
        step(i, 1, True, False)

    @pl.when(i % 2 == 0)
    def _():
        step(i, 0, True, False)

    score_chunk(0, 0, stacked_queries(jnp.minimum(i + 1, n_chunks - 1)))

    lam = (jnp.exp(jnp.sum(lq1_ref[...] * lk1_ref[...], axis=-1, keepdims=True))
           - jnp.exp(jnp.sum(lq2_ref[...] * lk2_ref[...], axis=-1, keepdims=True)) + LAMBDA_INIT)
    for h in range(DA_HEADS):
        o = acc_ref[h, :DA_V_DIM] / acc_ref[h, DA_V_DIM:DA_V_DIM + 1]
        o = o[:, :tq] - lam * o[:, tq:]
        o = o * lax.rsqrt(jnp.mean(o * o, axis=0, keepdims=True) + RMS_EPS) * sg_ref[...] * (1.0 - LAMBDA_INIT)
        o_ref[:, h * DA_V_DIM:(h + 1) * DA_V_DIM] = o.T.astype(o_ref.dtype)


def _diff_attn(qkv, lq1, lk1, lq2, lk2, subln_g_col, batch, seq, tq):
    t = qkv.shape[0]
    nq = seq // tq
    lam_spec = pl.BlockSpec((1, DA_HEAD_DIM), lambda b, i: (0, 0))
    return pl.pallas_call(
        functools.partial(_diff_attn_kernel, tq=tq),
        grid=(batch, nq),
        in_specs=[
            pl.BlockSpec((seq, DA_QK_WIDTH), lambda b, i: (b, 0)),
            pl.BlockSpec((seq, DA_QK_WIDTH), lambda b, i: (b, 1)),
            pl.BlockSpec((seq, DA_WIDTH), lambda b, i: (b, 2)),
            lam_spec, lam_spec, lam_spec, lam_spec,
            pl.BlockSpec((DA_V_DIM, 1), lambda b, i: (0, 0)),
        ],
        out_specs=pl.BlockSpec((tq, DA_WIDTH), lambda b, i: (b * nq + i, 0)),
        out_shape=jax.ShapeDtypeStruct((t, DA_WIDTH), BF16),
        scratch_shapes=[
            pltpu.VMEM((DA_HEADS, nq, DA_V_DIM + SUM_ROWS, tq), BF16),
            pltpu.VMEM((2, DA_HEADS, tq, 2 * tq), F32),
            pltpu.VMEM((DA_HEADS, 1, 2 * tq), F32),
            pltpu.VMEM((DA_HEADS, DA_V_DIM + SUM_ROWS, 2 * tq), F32),
        ],
        compiler_params=_cparams(2),
        name="diff_attn",
    )(qkv, qkv, qkv, lq1, lk1, lq2, lk2, subln_g_col)


def _pool_mixer(u, halo, seq_pos0):
    tm = u.shape[0]
    ext = jnp.concatenate([halo, u], axis=0)
    n = ext.shape[0]
    sums = []
    acc = ext
    for shift in (1, 2, 4, 8):
        acc = acc + pltpu.roll(acc, shift, 0)
        sums.append(acc[POOL_HALO:n])
    lane = lax.broadcasted_iota(jnp.int32, (tm, POOL_WIDTH), 1)
    pos = seq_pos0 + lax.broadcasted_iota(jnp.int32, (tm, POOL_WIDTH), 0)
    win_sum = sums[3]
    win = jnp.full((tm, POOL_WIDTH), POOL_WINDOWS[3], jnp.int32)
    for gi in (2, 1, 0):
        in_group = lane < (gi + 1) * POOL_GROUP
        win_sum = jnp.where(in_group, sums[gi], win_sum)
        win = jnp.where(in_group, POOL_WINDOWS[gi], win)
    cnt = jnp.minimum(pos + 1, win).astype(F32)
    return win_sum / cnt - u


def _mem_probs(mq, mk):
    head = lax.broadcasted_iota(jnp.int32, mq.shape, 1) // MEM_HEAD_DIM
    zero = jnp.zeros_like(mq)
    q4 = jnp.concatenate([jnp.where(head == h, mq, zero) for h in range(MEM_HEADS)], axis=0)
    s = lax.dot_general(q4, mk, (((1,), (1,)), ((), ())), preferred_element_type=F32)
    p = jnp.exp(s - jnp.max(s, axis=-1, keepdims=True))
    return p.astype(BF16), jnp.sum(p, axis=-1, keepdims=True)


def _mem_output(p, l, mv):
    tm = p.shape[0] // MEM_HEADS
    head = lax.broadcasted_iota(jnp.int32, (tm, MEM_WIDTH), 1) // MEM_HEAD_DIM
    o4 = jnp.dot(p, mv, preferred_element_type=F32) / l
    out = jnp.zeros((tm, MEM_WIDTH), F32)
    for h in range(MEM_HEADS):
        out = jnp.where(head == h, o4[h * tm:(h + 1) * tm], out)
    return out


def _router_select(logits):
    lane = lax.broadcasted_iota(jnp.int32, logits.shape, 1).astype(F32)
    neg = -jnp.inf

    def first_argmax(vals, vmax):
        return jnp.min(jnp.where(vals == vmax, lane, float(ROUTE_LANES)), axis=-1, keepdims=True)

    lg = jnp.where(lane < N_GROUPS, logits, neg)
    mg = jnp.max(lg, axis=-1, keepdims=True)
    g_p = 1.0 / jnp.sum(jnp.exp(lg - mg), axis=-1, keepdims=True)
    g_idx = first_argmax(lg, mg)
    lo = N_GROUPS + EXPERTS_PER_GROUP * g_idx
    le = jnp.where((lane >= lo) & (lane < lo + EXPERTS_PER_GROUP), logits, neg)
    m1 = jnp.max(le, axis=-1, keepdims=True)
    i1 = first_argmax(le, m1)
    le2 = jnp.where(lane == i1, neg, le)
    m2 = jnp.max(le2, axis=-1, keepdims=True)
    i2 = first_argmax(le2, m2)
    r = jnp.exp(m2 - m1)
    w1 = g_p / (1.0 + r)
    w2 = w1 * r
    return i1 - N_GROUPS, i2 - N_GROUPS, w1, w2


def _router_place(e1, e2, w1, w2):
    tm = e1.shape[0]
    lane = lax.broadcasted_iota(jnp.int32, (tm, ROUTE_LANES), 1).astype(F32)
    oh1 = lane == e1
    oh2 = lane == e2
    oh = jnp.where(oh1 | oh2, 1.0, 0.0)
    earlier = (lax.broadcasted_iota(jnp.int32, (tm, tm), 0) > lax.broadcasted_iota(jnp.int32, (tm, tm), 1))
    rank = jnp.dot(earlier.astype(BF16), oh.astype(BF16), preferred_element_type=F32)
    cnt = jnp.sum(oh, axis=0, keepdims=True)
    group = jnp.floor((cnt + (GROUP_ROWS - 1)) * (1.0 / GROUP_ROWS)) * GROUP_ROWS
    lower = (lax.broadcasted_iota(jnp.int32, (ROUTE_LANES, ROUTE_LANES), 0)
             < lax.broadcasted_iota(jnp.int32, (ROUTE_LANES, ROUTE_LANES), 1)).astype(BF16)
    offs = jnp.dot(jnp.broadcast_to(group, (8, ROUTE_LANES)).astype(BF16), lower,
                   preferred_element_type=F32)
    pos = offs[0:1] + rank

    def pick(onehot):
        return jnp.sum(jnp.where(onehot, pos, 0.0), axis=-1, keepdims=True)

    cols = (e1, e2, w1, w2, pick(oh1), pick(oh2))
    out = jnp.zeros((tm, ROUTE_LANES), F32)
    for n, col in enumerate(cols):
        out = jnp.where(lane == n, col, out)
    return out, cnt


def _merge_kernel(x_ref, u_ref, halo_ref, mq_ref, kv_ref, diff_ref,
                  gmix_ref, wg_ref, wbd_ref, pscale_ref, pp_ref, pd_ref, pm_ref, wo_ref,
                  gffn_ref, wr_ref, br_ref,
                  x1_ref, hn_ref, route_ref, cnt_ref, x1_prev_ref, *, tm, seq):
    i = pl.program_id(0)
    d = x_ref.shape[1]

    @pl.when(i == 0)
    def _():
        x1_prev_ref[...] = jnp.zeros(x1_prev_ref.shape, F32)

    hn = (_rms(x1_prev_ref[...]) * gffn_ref[...]).astype(BF16)
    hn_ref[...] = hn
    logits = jnp.dot(hn, wr_ref[...], preferred_element_type=F32) + br_ref[...]
    selection = _router_select(logits)

    x = x_ref[...]
    h = (_rms(x) * gmix_ref[...]).astype(BF16)

    seq_pos0 = (jnp.minimum(i, pl.num_programs(0) - 2) * tm) % seq
    halo = jnp.where(seq_pos0 == 0, 0.0, halo_ref[...])
    pooled = _pool_mixer(u_ref[...], halo, seq_pos0).astype(BF16)
    pool_out = jnp.dot(pooled, wbd_ref[...], preferred_element_type=F32) * pscale_ref[...]
    mem_p, mem_l = _mem_probs(mq_ref[...], kv_ref[:, :MEM_WIDTH])

    def branch(n, b_in, p_ref):
        gate = jax.nn.sigmoid(jnp.dot(h, wg_ref[:, n * d:(n + 1) * d], preferred_element_type=F32))
        return gate * jnp.dot(b_in, p_ref[...], preferred_element_type=F32)

    merged = branch(0, pool_out.astype(BF16), pp_ref) + branch(1, diff_ref[...], pd_ref)
    mem_out = _mem_output(mem_p, mem_l, kv_ref[:, MEM_WIDTH:])
    merged = merged + branch(2, mem_out.astype(BF16), pm_ref)
    route, cnt = _router_place(*selection)
    route_ref[...] = route
    cnt_ref[...] = jnp.broadcast_to(cnt, cnt_ref.shape)

    x1 = x + jnp.dot(merged.astype(BF16), wo_ref[...], preferred_element_type=F32)
    x1_ref[...] = x1
    x1_prev_ref[...] = x1


def _merge(x2, u_pool, mq, kv, diff, gmix, wg, wbd, pscale, pp, pd, pm, wo, gffn, wr, br, seq, n_mem, tm):
    t, d = x2.shape
    nt = t // tm
    tiles_per_seq = seq // tm
    halo_blocks = tm // POOL_HALO

    def tile(i):
        return (jnp.minimum(i, nt - 1), 0)

    def routed(i):
        return (jnp.maximum(i - 1, 0), 0)

    def const(i):
        return (0, 0)

    def wspec(a):
        return pl.BlockSpec(a.shape, const)

    return pl.pallas_call(
        functools.partial(_merge_kernel, tm=tm, seq=seq),
        grid=(nt + 1,),
        in_specs=[
            pl.BlockSpec((tm, d), tile),
            pl.BlockSpec((tm, POOL_WIDTH), tile),
            pl.BlockSpec((POOL_HALO, POOL_WIDTH), lambda i: (jnp.maximum(jnp.minimum(i, nt - 1) * halo_blocks - 1, 0), 0)),
            pl.BlockSpec((tm, MEM_WIDTH), tile),
            pl.BlockSpec((n_mem, 2 * MEM_WIDTH), lambda i: (jnp.minimum(i, nt - 1) // tiles_per_seq, 0)),
            pl.BlockSpec((tm, DA_WIDTH), tile),
            wspec(gmix), wspec(wg), wspec(wbd), wspec(pscale), wspec(pp), wspec(pd), wspec(pm), wspec(wo),
            wspec(gffn), wspec(wr), wspec(br),
        ],
        out_specs=[
            pl.BlockSpec((tm, d), tile),
            pl.BlockSpec((tm, d), routed),
            pl.BlockSpec((tm, ROUTE_LANES), routed),
            pl.BlockSpec((8, ROUTE_LANES), routed),
        ],
        out_shape=[
            jax.ShapeDtypeStruct((t, d), F32),
            jax.ShapeDtypeStruct((t, d), BF16),
            jax.ShapeDtypeStruct((t, ROUTE_LANES), F32),
            jax.ShapeDtypeStruct((nt * 8, ROUTE_LANES), F32),
        ],
        scratch_shapes=[pltpu.VMEM((tm, d), F32)],
        compiler_params=_cparams(1),
        name="merge",
    )(x2, u_pool, u_pool, mq, kv, diff, gmix, wg, wbd, pscale, pp, pd, pm, wo, gffn, wr, br)


def _dispatch_kernel(piece_ref, fill_ref, hn_ref, route_ref, rows_ref, rbuf, zbuf, sem, zsem, *, tm, n_blocks):
    i = pl.program_id(0)
    last = pl.num_programs(0) - 1
    k = rbuf.shape[1]
    pieces = k // GROUP_ROWS
    slot = i % 2
    prev = jnp.maximum(i - 1, 0)

    def wait_all():
        pltpu.make_async_copy(rbuf.at[0], rows_ref.at[pl.ds(0, k)], sem).wait()

    @pl.when(i == 0)
    def _():
        rbuf[1] = jnp.zeros(rbuf.shape[1:], rbuf.dtype)

    @pl.when(i > 0)
    def _():
        wait_all()

    route = route_ref[...]
    kio = lax.broadcasted_iota(jnp.int32, (tm, k), 1).astype(F32)
    onehot = jnp.where((kio == route[:, 4:5]) | (kio == route[:, 5:6]), 1.0, 0.0).astype(BF16)
    for f in range(pieces):
        pltpu.make_async_copy(
            rbuf.at[1 - slot, pl.ds(f * GROUP_ROWS, GROUP_ROWS)],
            rows_ref.at[pl.ds(pl.multiple_of(piece_ref[prev * pieces + f], GROUP_ROWS), GROUP_ROWS)], sem
        ).start(priority=f % 2)
    rbuf[slot] = lax.dot_general(onehot, hn_ref[...], (((0,), (0,)), ((), ())),
                                 preferred_element_type=F32).astype(BF16)

    @pl.when(i == last)
    def _():
        wait_all()
        zbuf[...] = jnp.zeros(zbuf.shape, zbuf.dtype)

        def tail_copy(dst_row):
            return pltpu.make_async_copy(zbuf.at[pl.ds(0, GROUP_ROWS)], rows_ref.at[pl.ds(dst_row, GROUP_ROWS)], zsem.at[0])

        def block_copy(blk):
            return pltpu.make_async_copy(zbuf, rows_ref.at[pl.ds(pl.multiple_of(blk * MOE_BLOCK, MOE_BLOCK), MOE_BLOCK)],
                                         zsem.at[1])

        n_used = fill_ref[2 * N_EXPERTS]
        for e in range(N_EXPERTS):
            start = fill_ref[e]

            def body(c, carry, start=start):
                tail_copy(pl.multiple_of(start + c * GROUP_ROWS, GROUP_ROWS)).start()
                return carry

            lax.fori_loop(0, fill_ref[N_EXPERTS + e], body, 0)

        def blk_body(b, carry):
            block_copy(b).start()
            return carry

        lax.fori_loop(n_used, n_blocks, blk_body, 0)

        for e in range(N_EXPERTS):
            def wbody(c, carry):
                tail_copy(0).wait()
                return carry

            lax.fori_loop(0, fill_ref[N_EXPERTS + e], wbody, 0)

        def blk_wait(b, carry):
            block_copy(0).wait()
            return carry

        lax.fori_loop(n_used, n_blocks, blk_wait, 0)

        dump_copy = pltpu.make_async_copy(zbuf.at[pl.ds(0, k)], rows_ref.at[pl.ds(n_blocks * MOE_BLOCK, k)], zsem.at[2])
        dump_copy.start()
        dump_copy.wait()


def _tile_rows(tm):
    return 2 * tm + N_EXPERTS * GROUP_ROWS


def _dispatch(piece_rows, fill, hn, route, n_blocks, tm):
    t, d = hn.shape
    nt = t // tm
    k = _tile_rows(tm)
    assert MOE_BLOCK >= k

    def tile(i, p, f):
        return (jnp.minimum(i, nt - 1), 0)

    grid_spec = pltpu.PrefetchScalarGridSpec(
        num_scalar_prefetch=2,
        grid=(nt + 1,),
        in_specs=[pl.BlockSpec((tm, d), tile), pl.BlockSpec((tm, ROUTE_LANES), tile)],
        out_specs=pl.BlockSpec(memory_space=pl.ANY),
        scratch_shapes=[pltpu.VMEM((2, k, d), BF16), pltpu.VMEM((MOE_BLOCK, d), BF16),
                        pltpu.SemaphoreType.DMA(()), pltpu.SemaphoreType.DMA((3,))],
    )
    return pl.pallas_call(
        functools.partial(_dispatch_kernel, tm=tm, n_blocks=n_blocks),
        grid_spec=grid_spec,
        out_shape=jax.ShapeDtypeStruct((n_blocks * MOE_BLOCK + k, d), BF16),
        compiler_params=_cparams(1),
        name="dispatch",
    )(piece_rows, fill, hn, route)


def _ffn_kernel(be_ref, nused_ref, rows_ref, wg_ref, wu_ref, wd_ref, y_ref):
    in_use = pl.program_id(0) < nused_ref[0]

    @pl.when(in_use)
    def _():
        xb = rows_ref[...]
        hg = jnp.dot(xb, wg_ref[...].astype(BF16), preferred_element_type=F32)
        hu = jnp.dot(xb, wu_ref[...].astype(BF16), preferred_element_type=F32)
        hid = (hg * jax.nn.sigmoid(hg) * hu).astype(BF16)
        y_ref[...] = jnp.dot(hid, wd_ref[...].astype(BF16), preferred_element_type=F32).astype(y_ref.dtype)


def _ffn(block_expert, n_used, rows, wg, wu, wd):
    n_blocks = block_expert.shape[0]
    d = rows.shape[1]
    hidden = wg.shape[2]

    def blk(i, be, nu):
        return (jnp.minimum(i, nu[0] - 1), 0)

    def wblk(i, be, nu):
        return (be[jnp.minimum(i, nu[0] - 1)], 0, 0)

    grid_spec = pltpu.PrefetchScalarGridSpec(
        num_scalar_prefetch=2,
        grid=(n_blocks,),
        in_specs=[
            pl.BlockSpec((MOE_BLOCK, d), blk),
            pl.BlockSpec((None, d, hidden), wblk),
            pl.BlockSpec((None, d, hidden), wblk),
            pl.BlockSpec((None, hidden, d), wblk),
        ],
        out_specs=pl.BlockSpec((MOE_BLOCK, d), blk),
    )
    return pl.pallas_call(
        _ffn_kernel,
        grid_spec=grid_spec,
        out_shape=jax.ShapeDtypeStruct(rows.shape, rows.dtype),
        input_output_aliases={2: 0},
        compiler_params=_cparams(1),
        name="ffn",
    )(block_expert, n_used, rows, wg, wu, wd)


def _combine_kernel(piece_ref, x1_ref, route_ref, g_ref, y_ref, o_ref, ybuf, sem, *, tm):
    i = pl.program_id(0)
    nt = pl.num_programs(0)
    slot = i % 2
    kbuf = ybuf.shape[1]
    pieces = kbuf // GROUP_ROWS

    def fetch(step, dst_slot, first, last):
        for f in range(first, last):
            pltpu.make_async_copy(y_ref.at[pl.ds(pl.multiple_of(piece_ref[step * pieces + f], GROUP_ROWS), GROUP_ROWS)],
                                  ybuf.at[dst_slot, pl.ds(f * GROUP_ROWS, GROUP_ROWS)], sem.at[dst_slot]
                                  ).start(priority=f % 2)

    def wait_all(the_slot):
        pltpu.make_async_copy(y_ref.at[pl.ds(0, kbuf)], ybuf.at[the_slot], sem.at[the_slot]).wait()

    @pl.when(i == 0)
    def _():
        fetch(0, 0, 0, pieces)

    wait_all(slot)

    route = route_ref[...]
    kio = lax.broadcasted_iota(jnp.int32, (tm, kbuf), 1).astype(F32)
    pw = (jnp.where(kio == route[:, 4:5], route[:, 2:3], 0.0)
          + jnp.where(kio == route[:, 5:6], route[:, 3:4], 0.0)).astype(BF16)
    fetch(jnp.minimum(i + 1, nt - 1), 1 - slot, 0, pieces)
    moe = jnp.dot(pw, ybuf[slot], preferred_element_type=F32)
    o_ref[...] = _rms(x1_ref[...] + moe) * g_ref[...]

    @pl.when(i == nt - 1)
    def _():
        wait_all(1 - slot)


def _combine(piece_rows, x1, route, g, y_rows, tm):
    t, d = x1.shape
    kbuf = _tile_rows(tm)
    grid_spec = pltpu.PrefetchScalarGridSpec(
        num_scalar_prefetch=1,
        grid=(t // tm,),
        in_specs=[
            pl.BlockSpec((tm, d), lambda i, p: (i, 0)),
            pl.BlockSpec((tm, ROUTE_LANES), lambda i, p: (i, 0)),
            pl.BlockSpec((1, d), lambda i, p: (0, 0)),
            pl.BlockSpec(memory_space=pl.ANY),
        ],
        out_specs=pl.BlockSpec((tm, d), lambda i, p: (i, 0)),
        scratch_shapes=[pltpu.VMEM((2, kbuf, d), BF16), pltpu.SemaphoreType.DMA((2,))],
    )
    return pl.pallas_call(
        functools.partial(_combine_kernel, tm=tm),
        grid_spec=grid_spec,
        out_shape=jax.ShapeDtypeStruct((t, d), F32),
        compiler_params=_cparams(1),
        name="combine",
    )(piece_rows, x1, route, g, y_rows)


def _dispatch_plan(cnt_tiles, n_blocks, pieces):
    n_pc = (cnt_tiles + GROUP_ROWS - 1) // GROUP_ROWS
    group = n_pc * GROUP_ROWS
    total = jnp.sum(group, axis=0)
    cap = ((total + MOE_BLOCK - 1) // MOE_BLOCK) * MOE_BLOCK
    pend = jnp.cumsum(cap)
    pstart = pend - cap
    tile_base = pstart[None, :] + jnp.cumsum(group, axis=0) - group
    block_start = jnp.arange(n_blocks, dtype=jnp.int32) * MOE_BLOCK
    block_expert = jnp.minimum(jnp.sum(pend[None, :] <= block_start[:, None], axis=1), N_EXPERTS - 1)
    n_used = jnp.maximum(pend[-1] // MOE_BLOCK, 1).reshape(1)
    fill = jnp.concatenate([pstart + total, (cap - total) // GROUP_ROWS, n_used])

    pc_end = jnp.cumsum(n_pc, axis=1)
    pc_start = pc_end - n_pc
    f = jnp.arange(pieces, dtype=jnp.int32)
    owner = jnp.sum(pc_end[:, :, None] <= f[None, None, :], axis=1)
    is_owner = owner[:, None, :] == jnp.arange(N_EXPERTS, dtype=jnp.int32)[None, :, None]
    piece_rows = jnp.sum(jnp.where(is_owner, (tile_base - pc_start * GROUP_ROWS)[:, :, None] + f * GROUP_ROWS, 0), axis=1)
    dump_rows = n_blocks * MOE_BLOCK + f * GROUP_ROWS
    to_rows = jnp.where(owner < N_EXPERTS, piece_rows, dump_rows[None, :])
    from_rows = jnp.where(owner < N_EXPERTS, piece_rows, (f * GROUP_ROWS)[None, :])
    return (to_rows.reshape(-1).astype(jnp.int32), from_rows.reshape(-1).astype(jnp.int32), fill.astype(jnp.int32),
            block_expert.astype(jnp.int32), n_used.astype(jnp.int32))


def kernel(x, mem, norm_mix_g, w_in, pool_w, pool_scale, lam_q1, lam_k1, lam_q2, lam_k2, subln_g, norm_mem_g,
           w_mem_kv, p_pool, p_diff, p_mem, w_o, norm_ffn_g, w_router_group, b_router_group, w_router_expert,
           b_router_expert, w_expert_gate, w_expert_up, w_expert_down, final_g):
    b, s, d = x.shape
    n_mem = mem.shape[1]
    t = b * s
    layer = 0
    x2 = x.reshape(t, d)
    mem2 = mem.reshape(b * n_mem, d)

    w_in_l = w_in[layer]
    w_proj = w_in_l[:, :PROJ_COLS].astype(BF16)
    w_gates = w_in_l[:, PROJ_COLS:].astype(BF16)
    n_win = len(POOL_WINDOWS)
    eye = jnp.eye(n_win, dtype=F32)
    w_bd = (pool_w[layer][:, :, None, :] * eye[:, None, :, None]).reshape(POOL_WIDTH, POOL_WIDTH).astype(BF16)
    n_route = N_GROUPS + N_EXPERTS
    w_r = jnp.concatenate([w_router_group[layer], w_router_expert[layer].reshape(d, N_EXPERTS),
                           jnp.zeros((d, ROUTE_LANES - n_route), F32)], axis=1).astype(BF16)
    b_r = jnp.concatenate([b_router_group[layer], b_router_expert[layer].reshape(N_EXPERTS),
                           jnp.zeros((ROUTE_LANES - n_route,), F32)]).reshape(1, ROUTE_LANES)

    def row(v):
        return v.reshape(1, -1)

    tm = MOE_TILE
    assert s % ATTN_TILE == 0 and s % tm == 0 and t % PROJ_TILE == 0 and (b * n_mem) % MEMKV_TILE == 0
    u_pool, qkv, mq = _proj(x2, row(norm_mix_g[layer]), w_proj, tm=PROJ_TILE)
    kv = _memkv(mem2, row(norm_mem_g[layer]), w_mem_kv[layer].astype(BF16), tm=MEMKV_TILE)
    diff = _diff_attn(qkv, row(lam_q1[layer]), row(lam_k1[layer]), row(lam_q2[layer]), row(lam_k2[layer]),
                      subln_g[layer].reshape(-1, 1), b, s, tq=ATTN_TILE)
    x1, hn, route, cnt = _merge(
        x2, u_pool, mq, kv, diff, row(norm_mix_g[layer]), w_gates, w_bd, row(pool_scale[layer]),
        p_pool[layer].astype(BF16), p_diff[layer].astype(BF16), p_mem[layer].astype(BF16), w_o[layer].astype(BF16),
        row(norm_ffn_g[layer]), w_r, b_r, s, n_mem, tm=tm)

    cnt_tiles = cnt[::8, :N_EXPERTS].astype(jnp.int32)
    n_groups = (t // tm) * N_EXPERTS
    n_blocks = (2 * t + n_groups * (GROUP_ROWS - 1) + N_EXPERTS * (MOE_BLOCK - 1)) // MOE_BLOCK + 1
    to_rows, from_rows, fill, block_expert, n_used = _dispatch_plan(cnt_tiles, n_blocks, _tile_rows(tm) // GROUP_ROWS)
    rows = _dispatch(to_rows, fill, hn, route, n_blocks, tm=tm)
    y_rows = _ffn(block_expert, n_used, rows, w_expert_gate[layer], w_expert_up[layer], w_expert_down[layer])
    out = _combine(from_rows, x1, route, row(final_g), y_rows, tm=tm)
    return out.reshape(b, s, d)
```
